```python
import math
import jax
import jax.numpy as jnp
from jax import lax
import numpy as np

D_MODEL = 1024
BATCH = 1
SEQ = 16384
DEPTH = 1
DEC_BATCH = 32
DEC_SEQ = 1
PAST_LEN = 16384
PAGE_SIZE = 128

HEAD_DIM = 64
ROPE_DIM = HEAD_DIM // 4
ROPE_THETA = 500000.0
A_HEADS = 8
A_KV_HEADS = 4
A_BLOCK = 256
A_TOPK = 3
B_HEADS = 8
B_KV_HEADS = 2
B_GROUP = B_HEADS // B_KV_HEADS
CMP_LEN = 32
CMP_STRIDE = 16
CMP_HIDDEN = 128
SEL_BLOCK = 64
SEL_TOPN = 16
WINDOW = 512
D_FF = 2816
CONV_W = 3
Q_BLOCK = 128
RMS_EPS = 1e-6
IN_SIZES = (A_HEADS * HEAD_DIM, A_KV_HEADS * HEAD_DIM, A_KV_HEADS * HEAD_DIM,
            B_HEADS * HEAD_DIM) + (B_KV_HEADS * HEAD_DIM,) * 6 + (3 * B_HEADS,)
D_IN = sum(IN_SIZES)
MIX_WIDTH = (A_HEADS + B_HEADS) * HEAD_DIM

kernel_name = 'moba_nsa_hybrid_step'


def rmsnorm(x, g):
    xf = x.astype(jnp.float32)
    y = xf * lax.rsqrt(jnp.mean(xf * xf, axis=-1, keepdims=True) + RMS_EPS)
    return (y * g.astype(jnp.float32)).astype(x.dtype)


def partial_rope(x, pos):
    half = ROPE_DIM // 2
    inv = ROPE_THETA ** (-2.0 * jnp.arange(half, dtype=jnp.float32) / ROPE_DIM)
    ang = pos.astype(jnp.float32)[:, None] * inv[None, :]
    cos, sin = jnp.cos(ang)[:, None, :], jnp.sin(ang)[:, None, :]
    xr = x[..., :ROPE_DIM].astype(jnp.float32)
    x1, x2 = xr[..., :half], xr[..., half:]
    rot = jnp.concatenate([x1 * cos - x2 * sin, x2 * cos + x1 * sin], axis=-1).astype(x.dtype)
    return jnp.concatenate([rot, x[..., ROPE_DIM:]], axis=-1)


def masked_softmax(s, mask):
    s = jnp.where(mask, s.astype(jnp.float32), -jnp.inf)
    m = jnp.max(s, axis=-1, keepdims=True)
    m = jnp.where(jnp.isfinite(m), m, 0.0)
    e = jnp.where(mask, jnp.exp(s - m), 0.0)
    d = jnp.sum(e, axis=-1, keepdims=True)
    return e / jnp.where(d > 0, d, 1.0)


def pad_rows(a, front, back):
    cfg = [(0, 0)] * a.ndim
    cfg[1] = (front, back)
    return jnp.pad(a, cfg)


def join(past, new):
    return new if past is None else jnp.concatenate([past, new], axis=1)


def sweep_query_blocks(fn, q_arrays):
    b, tq = q_arrays[0].shape[:2]
    qb = min(Q_BLOCK, tq)
    nb = -(-tq // qb)
    blocks = tuple(pad_rows(a, 0, nb * qb - tq).reshape(b, nb, qb, *a.shape[2:]).swapaxes(0, 1)
                   for a in q_arrays)
    starts = jnp.arange(nb, dtype=jnp.int32) * qb
    out = lax.map(lambda xs: fn(xs[0], *xs[1:]), (starts,) + blocks)
    out = out.swapaxes(0, 1).reshape(b, nb * qb, *out.shape[3:])
    return out[:, :tq]


def moba_attention(q, k, v, q0):
    b, tk = k.shape[:2]
    nblk = -(-tk // A_BLOCK)

    def to_blocks(a):
        a = pad_rows(a, 0, nblk * A_BLOCK - tk)
        return a.reshape(b, nblk, A_BLOCK, A_KV_HEADS, HEAD_DIM).transpose(0, 3, 1, 2, 4)

    kb, vb = to_blocks(k), to_blocks(v)
    kv_of_head = jnp.arange(A_HEADS) // (A_HEADS // A_KV_HEADS)
    kmean = jnp.mean(kb.astype(jnp.float32), axis=3)[:, kv_of_head]
    ntop = min(A_TOPK, nblk)
    bi = jnp.arange(b)[:, None, None, None]
    hi = kv_of_head[None, None, :, None]
    blk_ids = jnp.arange(nblk)
    offs = jnp.arange(A_BLOCK)
    scale = HEAD_DIM ** -0.5

    def block(start, qblk):
        qlen = qblk.shape[1]
        qpos = q0 + start + jnp.arange(qlen)
        cur = jnp.minimum(qpos // A_BLOCK, nblk - 1)
        gate = jnp.einsum('bqhd,bhnd->bqhn', qblk.astype(jnp.float32), kmean)
        past = blk_ids[None, :] < cur[:, None]
        gate = jnp.where(past[None, :, None, :], gate, -jnp.inf)
        top_s, top_i = lax.top_k(gate, ntop)
        own = jnp.broadcast_to(cur[None, :, None, None], top_i.shape[:3] + (1,))
        idx = jnp.concatenate([top_i, own], axis=-1)
        ok = jnp.concatenate([jnp.isfinite(top_s), jnp.ones(top_s.shape[:3] + (1,), dtype=bool)], axis=-1)
        kg = kb[bi, hi, idx]
        vg = vb[bi, hi, idx]
        kpos = idx[..., None] * A_BLOCK + offs
        mask = ok[..., None] & (kpos <= qpos[None, :, None, None, None])
        nk = idx.shape[-1] * A_BLOCK
        s = jnp.einsum('bqhd,bqhskd->bqhsk', qblk * scale, kg)
        p = masked_softmax(s.reshape(*s.shape[:3], nk), mask.reshape(*mask.shape[:3], nk))
        return jnp.einsum('bqhn,bqhnd->bqhd', p.astype(vg.dtype), vg.reshape(*vg.shape[:3], nk, HEAD_DIM))

    return sweep_query_blocks(block, (q,))


def compress_rows(x, pe, w1, b1, w2):
    b, t = x.shape[:2]
    xc = x.reshape(b, t // CMP_STRIDE, CMP_STRIDE, B_KV_HEADS, HEAD_DIM)
    lo = jnp.einsum('bcskd,sdf->bckf', xc + pe[:CMP_STRIDE, None, :], w1[:CMP_STRIDE])
    hi = jnp.einsum('bcskd,sdf->bckf', xc + pe[CMP_STRIDE:, None, :], w1[CMP_STRIDE:])
    hid = jax.nn.gelu(lo[:, :-1] + hi[:, 1:] + b1)
    return jnp.einsum('bckf,fd->bckd', hid, w2)


def nsa_attention(q, gates, kc, vc, ks, vs, kw, vw, q0, k0w, cmp_pe, cmp_w1, cmp_b1, cmp_w2):
    b, tk = kc.shape[:2]
    tq = q.shape[1]
    nsel = -(-tk // SEL_BLOCK)
    tpad = nsel * SEL_BLOCK
    kc, vc, ks, vs = (pad_rows(a, 0, tpad - tk) for a in (kc, vc, ks, vs))
    kcmp = compress_rows(kc, cmp_pe[0], cmp_w1[0], cmp_b1[0], cmp_w2[0])
    vcmp = compress_rows(vc, cmp_pe[1], cmp_w1[1], cmp_b1[1], cmp_w2[1])
    cmp_end = jnp.arange(kcmp.shape[1]) * CMP_STRIDE + (CMP_LEN - 1)
    ksb = ks.reshape(b, nsel, SEL_BLOCK, B_KV_HEADS, HEAD_DIM).transpose(0, 3, 1, 2, 4)
    vsb = vs.reshape(b, nsel, SEL_BLOCK, B_KV_HEADS, HEAD_DIM).transpose(0, 3, 1, 2, 4)
    topn = min(SEL_TOPN, nsel)
    qb_len = min(Q_BLOCK, tq)
    back = -(-tq // qb_len) * qb_len - tq
    kw_pad, vw_pad = pad_rows(kw, WINDOW, back), pad_rows(vw, WINDOW, back)
    band_off = q0 - k0w
    bi = jnp.arange(b)[:, None, None, None]
    gi = jnp.arange(B_KV_HEADS)[None, None, :, None]
    sel_ids = jnp.arange(nsel)[None, None, None, :]
    offs = jnp.arange(SEL_BLOCK)
    ratio = SEL_BLOCK // CMP_STRIDE
    scale = HEAD_DIM ** -0.5

    def block(start, qblk, gblk):
        qlen = qblk.shape[1]
        qpos = q0 + start + jnp.arange(qlen)
        qs = qblk * scale
        qg = qs.reshape(b, qlen, B_KV_HEADS, B_GROUP, HEAD_DIM)
        qg_rot = partial_rope(qs, qpos).reshape(b, qlen, B_KV_HEADS, B_GROUP, HEAD_DIM)
        s = jnp.einsum('bqkgd,bnkd->bqkgn', qg, kcmp)
        cmask = (cmp_end[None, :] <= qpos[:, None])[None, :, None, None, :]
        p_cmp = masked_softmax(s, cmask)
        o_cmp = jnp.einsum('bqkgn,bnkd->bqkgd', p_cmp.astype(vcmp.dtype), vcmp)
        imp = jnp.pad(p_cmp.sum(axis=3), [(0, 0), (0, 0), (0, 0), (0, 1)])
        imp = imp.reshape(b, qlen, B_KV_HEADS, nsel, ratio)
        spill = 0.5 * imp[..., -1]
        imp_sel = imp[..., :-1].sum(axis=-1) + spill
        imp_sel = imp_sel + jnp.pad(spill[..., :-1], [(0, 0), (0, 0), (0, 0), (1, 0)])
        cur = jnp.minimum(qpos // SEL_BLOCK, nsel - 1)[None, :, None, None]
        forced = (sel_ids == 0) | (sel_ids == cur) | (sel_ids == cur - 1)
        imp_sel = jnp.where(forced, jnp.inf, imp_sel)
        imp_sel = jnp.where(sel_ids > cur, -jnp.inf, imp_sel)
        top_s, top_i = lax.top_k(imp_sel, topn)
        ok = top_s > -jnp.inf
        kg = ksb[bi, gi, top_i]
        vg = vsb[bi, gi, top_i]
        kpos = top_i[..., None] * SEL_BLOCK + offs
        nk = topn * SEL_BLOCK
        smask = (ok[..., None] & (kpos <= qpos[None, :, None, None, None])).reshape(b, qlen, B_KV_HEADS, 1, nk)
        s = jnp.einsum('bqkgd,bqknld->bqkgnl', qg_rot, kg).reshape(b, qlen, B_KV_HEADS, B_GROUP, nk)
        p = masked_softmax(s, smask)
        o_sel = jnp.einsum('bqkgn,bqknd->bqkgd', p.astype(vg.dtype), vg.reshape(b, qlen, B_KV_HEADS, nk, HEAD_DIM))
        kband = lax.dynamic_slice_in_dim(kw_pad, band_off + start, WINDOW + qlen, axis=1)
        vband = lax.dynamic_slice_in_dim(vw_pad, band_off + start, WINDOW + qlen, axis=1)
        kpos_w = q0 + start - WINDOW + jnp.arange(WINDOW + qlen)
        wmask = ((kpos_w[None, :] <= qpos[:, None]) & (kpos_w[None, :] >= qpos[:, None] - WINDOW)
                 & (kpos_w[None, :] >= k0w))
        s = jnp.einsum('bqkgd,bnkd->bqkgn', qg_rot, kband)
        p = masked_softmax(s, wmask[None, :, None, None, :])
        o_win = jnp.einsum('bqkgn,bnkd->bqkgd', p.astype(vband.dtype), vband)
        g = gblk.reshape(b, qlen, B_KV_HEADS, B_GROUP, 3).astype(o_cmp.dtype)
        o = g[..., 0:1] * o_cmp + g[..., 1:2] * o_sel + g[..., 2:3] * o_win
        return o.reshape(b, qlen, B_HEADS, HEAD_DIM)

    return sweep_query_blocks(block, (q, gates))


def hybrid_layer(x, q0, past, weights):
    (g_mix_pre, w_in, b_gate, cmp_pe, cmp_w1, cmp_b1, cmp_w2, w_out, g_mix_post,
     g_ffn_pre, w_up, w_conv, b_conv, w_down, g_ffn_post) = weights
    b, t, _ = x.shape
    pos = q0 + jnp.arange(t)
    h = rmsnorm(x, g_mix_pre)
    parts = jnp.split(h @ w_in, np.cumsum(IN_SIZES)[:-1].tolist(), axis=-1)
    qa, ka, va, qb, kc, vc, ks, vs, kw, vw = (p.reshape(b, t, -1, HEAD_DIM) for p in parts[:-1])
    gates = jax.nn.sigmoid(parts[-1] + b_gate).reshape(b, t, B_HEADS, 3)
    qa, ka, ks, kw = (partial_rope(a, pos) for a in (qa, ka, ks, kw))
    p_ka, p_va, p_kc, p_vc, p_ks, p_vs, p_kw, p_vw, p_conv = past
    all_kw, all_vw = join(p_kw, kw), join(p_vw, vw)
    k0w = q0 + t - all_kw.shape[1]
    o_a = moba_attention(qa, join(p_ka, ka), join(p_va, va), q0)
    o_b = nsa_attention(qb, gates, join(p_kc, kc), join(p_vc, vc), join(p_ks, ks), join(p_vs, vs),
                        all_kw, all_vw, q0, k0w, cmp_pe, cmp_w1, cmp_b1, cmp_w2)
    mix = jnp.concatenate([o_a.reshape(b, t, -1), o_b.reshape(b, t, -1)], axis=-1) @ w_out
    x = x + rmsnorm(mix, g_mix_post)
    up = rmsnorm(x, g_ffn_pre) @ w_up
    if p_conv is None:
        p_conv = jnp.zeros((b, CONV_W - 1, up.shape[-1]), up.dtype)
    up_all = jnp.concatenate([p_conv, up], axis=1)
    conv = sum((up_all[:, i:i + t] * w_conv[i] for i in range(CONV_W)), b_conv)
    gate, val = jnp.split(conv, 2, axis=-1)
    x = x + rmsnorm((jax.nn.gelu(gate) * val) @ w_down, g_ffn_post)
    keep = min(WINDOW, all_kw.shape[1])
    return (x, jnp.stack([ka, va], axis=2), jnp.stack([kc, vc, ks, vs], axis=2),
            jnp.stack([all_kw[:, -keep:], all_vw[:, -keep:]], axis=2), up_all[:, -(CONV_W - 1):])


def gather_pages(pool, layer, page_table):
    g = pool[layer, page_table]
    return g.reshape(g.shape[0], -1, *g.shape[3:])


def setup_inputs(seed: int = 0) -> dict:
    key = jax.random.key(seed)
    ks = jax.random.split(key, 24)
    f32 = jnp.float32

    def nrm(k, shape, s=1.0):
        return jax.random.normal(k, shape, f32) * s

    n_pages = PAST_LEN // PAGE_SIZE
    used = DEC_BATCH * n_pages
    n_pool = used + max(1, used // 4)
    wb = min(WINDOW, PAST_LEN)
    page_table = jax.random.permutation(ks[0], n_pool)[:used].reshape(DEC_BATCH, n_pages).astype(jnp.int32)
    return {
        'x_prompt': nrm(ks[1], (BATCH, SEQ, D_MODEL)),
        'x_sample': nrm(ks[2], (DEC_BATCH, DEC_SEQ, D_MODEL)),
        'cache_moba_kv': nrm(ks[3], (DEPTH, n_pool, PAGE_SIZE, 2, A_KV_HEADS, HEAD_DIM)),
        'cache_nsa_kv': nrm(ks[4], (DEPTH, n_pool, PAGE_SIZE, 4, B_KV_HEADS, HEAD_DIM)),
        'cache_nsa_win_kv': nrm(ks[5], (DEPTH, DEC_BATCH, wb, 2, B_KV_HEADS, HEAD_DIM)),
        'state_ffn_conv': nrm(ks[6], (DEPTH, DEC_BATCH, CONV_W - 1, 2 * D_FF)),
        'page_table': page_table,
        'g_mix_pre': 1.0 + nrm(ks[7], (DEPTH, D_MODEL), 0.05),
        'w_in': nrm(ks[8], (DEPTH, D_MODEL, D_IN), D_MODEL ** -0.5),
        'b_gate': nrm(ks[9], (DEPTH, 3 * B_HEADS), 0.1),
        'cmp_pe': nrm(ks[10], (DEPTH, 2, CMP_LEN, HEAD_DIM), 0.5),
        'cmp_w1': nrm(ks[11], (DEPTH, 2, CMP_LEN, HEAD_DIM, CMP_HIDDEN), (CMP_LEN * HEAD_DIM) ** -0.5),
        'cmp_b1': nrm(ks[12], (DEPTH, 2, CMP_HIDDEN), 0.02),
        'cmp_w2': nrm(ks[13], (DEPTH, 2, CMP_HIDDEN, HEAD_DIM), CMP_HIDDEN ** -0.5),
        'w_out': nrm(ks[14], (DEPTH, MIX_WIDTH, D_MODEL), MIX_WIDTH ** -0.5),
        'g_mix_post': 1.0 + nrm(ks[15], (DEPTH, D_MODEL), 0.05),
        'g_ffn_pre': 1.0 + nrm(ks[16], (DEPTH, D_MODEL), 0.05),
        'w_up': nrm(ks[17], (DEPTH, D_MODEL, 2 * D_FF), D_MODEL ** -0.5),
        'w_conv': nrm(ks[18], (DEPTH, CONV_W, 2 * D_FF), CONV_W ** -0.5),
        'b_conv': nrm(ks[19], (DEPTH, 2 * D_FF), 0.02),
        'w_down': nrm(ks[20], (DEPTH, D_FF, D_MODEL), D_FF ** -0.5),
        'g_ffn_post': 1.0 + nrm(ks[21], (DEPTH, D_MODEL), 0.05),
    }


def reference(x_prompt, x_sample, cache_moba_kv, cache_nsa_kv, cache_nsa_win_kv, state_ffn_conv,
              page_table, g_mix_pre, w_in, b_gate, cmp_pe, cmp_w1, cmp_b1, cmp_w2, w_out, g_mix_post,
              g_ffn_pre, w_up, w_conv, b_conv, w_down, g_ffn_post):
    past_len = page_table.shape[1] * cache_moba_kv.shape[2]
    y_prompt, y_sample = x_prompt, x_sample
    moba_p, moba_s, nsa_p, nsa_s, win_p, win_s, conv_p, conv_s = [], [], [], [], [], [], [], []
    for l in range(DEPTH):
        weights = (g_mix_pre[l], w_in[l], b_gate[l], cmp_pe[l], cmp_w1[l], cmp_b1[l], cmp_w2[l],
                   w_out[l], g_mix_post[l], g_ffn_pre[l], w_up[l], w_conv[l], b_conv[l],
                   w_down[l], g_ffn_post[l])
        y_prompt, ra, rn, rw, rc = hybrid_layer(y_prompt, 0, (None,) * 9, weights)
        moba_p.append(ra)
        nsa_p.append(rn)
        win_p.append(rw)
        conv_p.append(rc)
        ga = gather_pages(cache_moba_kv, l, page_table)
        gn = gather_pages(cache_nsa_kv, l, page_table)
        win = cache_nsa_win_kv[l]
        past = (ga[:, :, 0], ga[:, :, 1], gn[:, :, 0], gn[:, :, 1], gn[:, :, 2], gn[:, :, 3],
                win[:, :, 0], win[:, :, 1], state_ffn_conv[l])
        y_sample, sa, sn, sw, sc = hybrid_layer(y_sample, past_len, past, weights)
        moba_s.append(sa)
        nsa_s.append(sn)
        win_s.append(sw)
        conv_s.append(sc)
    return (y_prompt, y_sample, jnp.stack(moba_p), jnp.stack(moba_s), jnp.stack(nsa_p), jnp.stack(nsa_s),
            jnp.stack(win_p), jnp.stack(win_s), jnp.stack(conv_p), jnp.stack(conv_s))
```

```python
import functools
import math

import numpy as np
import jax
import jax.numpy as jnp
from jax import lax
from jax.experimental import pallas as pl
from jax.experimental.pallas import tpu as pltpu

HEAD_DIM = 64
ROPE_DIM = HEAD_DIM // 4
ROPE_THETA = 500000.0
A_HEADS = 8
A_KV_HEADS = 4
A_BLOCK = 256
A_TOPK = 3
B_HEADS = 8
B_KV_HEADS = 2
B_GROUP = B_HEADS // B_KV_HEADS
CMP_LEN = 32
CMP_STRIDE = 16
CMP_HIDDEN = 128
SEL_BLOCK = 64
SEL_TOPN = 16
WINDOW = 512
CONV_W = 3
RMS_EPS = 1e-6
SCALE = HEAD_DIM ** -0.5

LANES = 128
VMEM_LIMIT = 56 * 1024 * 1024
MASK_BIAS = -32768.0
NEG_INF = float("-inf")

BF16 = jnp.bfloat16
F32 = jnp.float32


def _cparams(*sem):
    return pltpu.CompilerParams(dimension_semantics=sem, vmem_limit_bytes=VMEM_LIMIT)


def _dot(a, b):
    return jnp.dot(a, b, preferred_element_type=F32)


def _dot_t(a, b):
    return lax.dot_general(a, b, (((1,), (1,)), ((), ())), preferred_element_type=F32)


def _rms(x, g):
    y = x * lax.rsqrt(jnp.mean(x * x, axis=-1, keepdims=True) + RMS_EPS)
    return y * g


def _gelu(x):
    c = math.sqrt(2.0 / math.pi)
    return 0.5 * x * (1.0 + jnp.tanh(c * (x + 0.044715 * (x * x * x))))


def _rope128(x, c, sa, sb):
    return x * c + pltpu.roll(x, LANES - ROPE_DIM // 2, 1) * sa + pltpu.roll(x, ROPE_DIM // 2, 1) * sb


C_QA, C_KA, C_VA, C_QB = 0, 512, 768, 1024
C_KC, C_VC, C_KS, C_VS, C_KW, C_VW, C_G = 1536, 1664, 1792, 1920, 2048, 2176, 2304
D_IN_PAD = 2432


def _inproj_kernel(x_ref, g_ref, w_ref, bg_ref, c_ref, sa_ref, sb_ref,
                   kva_ref, nsa_ref, win_ref, gates_ref,
                   qa_ref, kaug_ref, va_ref, qb_ref, qbr_ref, ksaug_ref, vs_ref, kw_ref, vw_ref,
                   kcvc_ref, kmean_ref, *, tm, with_kmean):
    i = pl.program_id(0)
    h = _rms(x_ref[...], g_ref[...]).astype(BF16)
    c, sa, sb = c_ref[...], sa_ref[...], sb_ref[...]

    def proj(c0, width):
        return _dot(h, w_ref[:, c0:c0 + width])

    lane = lax.broadcasted_iota(jnp.int32, (tm, LANES), 1)
    row = lax.broadcasted_iota(jnp.int32, (tm, LANES), 0) + i * tm
    zeros64 = jnp.zeros((tm, HEAD_DIM), BF16)

    for p in range(4):
        q = _rope128(proj(C_QA + p * LANES, LANES), c, sa, sb)
        qa_ref[2 * p] = jnp.concatenate([q[:, :HEAD_DIM].astype(BF16), zeros64], axis=1)
        qa_ref[2 * p + 1] = jnp.concatenate([q[:, HEAD_DIM:].astype(BF16), zeros64], axis=1)

    a_onehot = (lane - HEAD_DIM == row // A_BLOCK).astype(BF16)
    ksum = []
    for p in range(2):
        k = _rope128(proj(C_KA + p * LANES, LANES), c, sa, sb)
        v = proj(C_VA + p * LANES, LANES)
        kva_ref[:, p * LANES:(p + 1) * LANES] = k
        kva_ref[:, 256 + p * LANES:256 + (p + 1) * LANES] = v
        kb = k.astype(BF16)
        vb = v.astype(BF16)
        kaug_ref[2 * p] = jnp.concatenate([kb[:, :HEAD_DIM], a_onehot[:, HEAD_DIM:]], axis=1)
        kaug_ref[2 * p + 1] = jnp.concatenate([kb[:, HEAD_DIM:], a_onehot[:, HEAD_DIM:]], axis=1)
        va_ref[2 * p] = vb[:, :HEAD_DIM]
        va_ref[2 * p + 1] = vb[:, HEAD_DIM:]
        if with_kmean:
            ksum.append(jnp.sum(k.reshape(tm // A_BLOCK, A_BLOCK, LANES), axis=1))
    if with_kmean:
        kmean_ref[0] = jnp.concatenate(ksum, axis=1) * (1.0 / A_BLOCK)
    else:
        kmean_ref[...] = jnp.zeros(kmean_ref.shape, F32)

    for p in range(4):
        q = proj(C_QB + p * LANES, LANES) * SCALE
        qr = _rope128(q, c, sa, sb)
        qb_ref[2 * p] = q[:, :HEAD_DIM].astype(BF16)
        qb_ref[2 * p + 1] = q[:, HEAD_DIM:].astype(BF16)
        qbr_ref[2 * p] = jnp.concatenate([qr[:, :HEAD_DIM].astype(BF16), zeros64], axis=1)
        qbr_ref[2 * p + 1] = jnp.concatenate([qr[:, HEAD_DIM:].astype(BF16), zeros64], axis=1)

    kc = proj(C_KC, LANES)
    vc = proj(C_VC, LANES)
    ks = _rope128(proj(C_KS, LANES), c, sa, sb)
    vs = proj(C_VS, LANES)
    nsa_ref[:, 0:128] = kc
    nsa_ref[:, 128:256] = vc
    nsa_ref[:, 256:384] = ks
    nsa_ref[:, 384:512] = vs
    kcvc_ref[0] = kc[:, :HEAD_DIM]
    kcvc_ref[1] = kc[:, HEAD_DIM:]
    kcvc_ref[2] = vc[:, :HEAD_DIM]
    kcvc_ref[3] = vc[:, HEAD_DIM:]
    s_onehot = (lane - HEAD_DIM == (row // SEL_BLOCK) % HEAD_DIM).astype(BF16)
    ksb = ks.astype(BF16)
    vsb = vs.astype(BF16)
    ksaug_ref[0] = jnp.concatenate([ksb[:, :HEAD_DIM], s_onehot[:, HEAD_DIM:]], axis=1)
    ksaug_ref[1] = jnp.concatenate([ksb[:, HEAD_DIM:], s_onehot[:, HEAD_DIM:]], axis=1)
    vs_ref[0] = vsb[:, :HEAD_DIM]
    vs_ref[1] = vsb[:, HEAD_DIM:]

    kw = _rope128(proj(C_KW, LANES), c, sa, sb)
    vw = proj(C_VW, LANES)
    win_ref[:, 0:128] = kw
    win_ref[:, 128:256] = vw
    kwb = kw.astype(BF16)
    vwb = vw.astype(BF16)
    kw_ref[0] = kwb[:, :HEAD_DIM]
    kw_ref[1] = kwb[:, HEAD_DIM:]
    vw_ref[0] = vwb[:, :HEAD_DIM]
    vw_ref[1] = vwb[:, HEAD_DIM:]

    gates_ref[...] = jax.nn.sigmoid(proj(C_G, LANES) + bg_ref[...])


def _rope_tables(pos):
    half = ROPE_DIM // 2
    inv = ROPE_THETA ** (-2.0 * jnp.arange(half, dtype=F32) / ROPE_DIM)
    ang = pos.astype(F32)[:, None] * inv[None, :]
    cos, sin = jnp.cos(ang), jnp.sin(ang)
    t = pos.shape[0]
    ones = jnp.ones((t, HEAD_DIM - ROPE_DIM), F32)
    zeros = jnp.zeros((t, HEAD_DIM - ROPE_DIM), F32)
    zh = jnp.zeros((t, half), F32)
    c = jnp.concatenate([cos, cos, ones], axis=1)
    sa = jnp.concatenate([-sin, zh, zeros], axis=1)
    sb = jnp.concatenate([zh, sin, zeros], axis=1)
    return tuple(jnp.concatenate([a, a], axis=1) for a in (c, sa, sb))


def _inproj(x2d, pos, g, w_pad, bg_pad, *, tm, with_kmean):
    t, d = x2d.shape
    nt = t // tm
    c, sa, sb = _rope_tables(pos)
    row_spec = lambda w: pl.BlockSpec((tm, w), lambda i: (i, 0))
    head_spec = lambda n, w: pl.BlockSpec((n, tm, w), lambda i: (0, i, 0))
    full = lambda a: pl.BlockSpec(a.shape, lambda i: (0,) * a.ndim)
    nkm = max(tm // A_BLOCK, 1)
    out_shape = (
        jax.ShapeDtypeStruct((t, 512), F32),
        jax.ShapeDtypeStruct((t, 512), F32),
        jax.ShapeDtypeStruct((t, 256), F32),
        jax.ShapeDtypeStruct((t, LANES), F32),
        jax.ShapeDtypeStruct((A_HEADS, t, LANES), BF16),
        jax.ShapeDtypeStruct((A_KV_HEADS, t, LANES), BF16),
        jax.ShapeDtypeStruct((A_KV_HEADS, t, HEAD_DIM), BF16),
        jax.ShapeDtypeStruct((B_HEADS, t, HEAD_DIM), BF16),
        jax.ShapeDtypeStruct((B_HEADS, t, LANES), BF16),
        jax.ShapeDtypeStruct((B_KV_HEADS, t, LANES), BF16),
        jax.ShapeDtypeStruct((B_KV_HEADS, t, HEAD_DIM), BF16),
        jax.ShapeDtypeStruct((B_KV_HEADS, t, HEAD_DIM), BF16),
        jax.ShapeDtypeStruct((B_KV_HEADS, t, HEAD_DIM), BF16),
        jax.ShapeDtypeStruct((4, t, HEAD_DIM), F32),
        jax.ShapeDtypeStruct((nt, nkm, 256), F32),
    )
    out_specs = (
        row_spec(512), row_spec(512), row_spec(256), row_spec(LANES),
        head_spec(A_HEADS, LANES), head_spec(A_KV_HEADS, LANES), head_spec(A_KV_HEADS, HEAD_DIM),
        head_spec(B_HEADS, HEAD_DIM), head_spec(B_HEADS, LANES), head_spec(B_KV_HEADS, LANES),
        head_spec(B_KV_HEADS, HEAD_DIM), head_spec(B_KV_HEADS, HEAD_DIM), head_spec(B_KV_HEADS, HEAD_DIM),
        head_spec(4, HEAD_DIM),
        pl.BlockSpec((1, nkm, 256), lambda i: (i, 0, 0)),
    )
    return pl.pallas_call(
        functools.partial(_inproj_kernel, tm=tm, with_kmean=with_kmean),
        out_shape=out_shape,
        grid=(nt,),
        in_specs=[row_spec(d), full(g), full(w_pad), full(bg_pad),
                  row_spec(LANES), row_spec(LANES), row_spec(LANES)],
        out_specs=out_specs,
        compiler_params=_cparams("parallel"),
        name="inproj",
    )(x2d, g, w_pad, bg_pad, c, sa, sb)


def _outproj_kernel(oa_ref, ob_ref, x_ref, wa_ref, wb_ref, gpost_ref, gpre_ref, x1_ref, h2_ref):
    mix = _dot(oa_ref[...].astype(BF16), wa_ref[...]) + _dot(ob_ref[...].astype(BF16), wb_ref[...])
    x1 = x_ref[...] + _rms(mix, gpost_ref[...])
    x1_ref[...] = x1
    h2_ref[...] = _rms(x1, gpre_ref[...]).astype(BF16)


def _outproj(oa, ob, x2d, w_out_bf, g_post, g_pre, *, tm):
    t, d = x2d.shape
    half = oa.shape[1]
    row = lambda w: pl.BlockSpec((tm, w), lambda i: (i, 0))
    full = lambda a: pl.BlockSpec(a.shape, lambda i: (0,) * a.ndim)
    wa, wb = w_out_bf[:half], w_out_bf[half:]
    return pl.pallas_call(
        _outproj_kernel,
        out_shape=(jax.ShapeDtypeStruct((t, d), F32), jax.ShapeDtypeStruct((t, d), BF16)),
        grid=(t // tm,),
        in_specs=[row(half), row(half), row(d), full(wa), full(wb), full(g_post), full(g_pre)],
        out_specs=(row(d), row(d)),
        compiler_params=_cparams("parallel"),
        name="outproj",
    )(oa, ob, x2d, wa, wb, g_post, g_pre)


HALO = 8


def _ffn_seq_kernel(h_ref, halo_ref, x1_ref, wg_ref, wv_ref, cg_ref, cv_ref, bg_ref, bv_ref,
                    wd_ref, gpost_ref, y_ref, tailg_ref, tailv_ref, acc_ref, ug_ref, uv_ref, *, tm):
    i, c = pl.program_id(0), pl.program_id(1)

    @pl.when(c == 0)
    def _():
        acc_ref[...] = jnp.zeros(acc_ref.shape, F32)

    keep = (i > 0).astype(F32)

    def conv(w_ref, u_ref, cw_ref, cb_ref, tail_ref):
        u_ref[0:HALO] = _dot(halo_ref[...], w_ref[...]) * keep
        u_ref[HALO:HALO + tm] = _dot(h_ref[...], w_ref[...])
        tail_ref[0] = u_ref[tm:tm + HALO]
        cw = cw_ref[...]
        return (u_ref[pl.ds(HALO - 2, tm), :] * cw[0:1] + u_ref[pl.ds(HALO - 1, tm), :] * cw[1:2]
                + u_ref[pl.ds(HALO, tm), :] * cw[2:3] + cb_ref[...])

    gate = conv(wg_ref, ug_ref, cg_ref, bg_ref, tailg_ref)
    val = conv(wv_ref, uv_ref, cv_ref, bv_ref, tailv_ref)
    acc_ref[...] += _dot((_gelu(gate) * val).astype(BF16), wd_ref[...])

    @pl.when(c == pl.num_programs(1) - 1)
    def _():
        y_ref[...] = x1_ref[...] + _rms(acc_ref[...], gpost_ref[...])


def _ffn_step_kernel(h_ref, p0g_ref, p1g_ref, p0v_ref, p1v_ref, x1_ref, wg_ref, wv_ref, cg_ref, cv_ref,
                     bg_ref, bv_ref, wd_ref, gpost_ref, y_ref, upg_ref, upv_ref, acc_ref):
    c = pl.program_id(0)

    @pl.when(c == 0)
    def _():
        acc_ref[...] = jnp.zeros(acc_ref.shape, F32)

    def conv(w_ref, p0_ref, p1_ref, cw_ref, cb_ref, up_ref):
        u = _dot(h_ref[...], w_ref[...])
        up_ref[...] = u
        cw = cw_ref[...]
        return p0_ref[...] * cw[0:1] + p1_ref[...] * cw[1:2] + u * cw[2:3] + cb_ref[...]

    gate = conv(wg_ref, p0g_ref, p1g_ref, cg_ref, bg_ref, upg_ref)
    val = conv(wv_ref, p0v_ref, p1v_ref, cv_ref, bv_ref, upv_ref)
    acc_ref[...] += _dot((_gelu(gate) * val).astype(BF16), wd_ref[...])

    @pl.when(c == pl.num_programs(0) - 1)
    def _():
        y_ref[...] = x1_ref[...] + _rms(acc_ref[...], gpost_ref[...])


def _ffn_seq(h2, x1, w_up_bf, w_conv, b_conv2d, w_down_bf, g_post, *, tm, ck):
    t, d = x1.shape
    dff = w_down_bf.shape[0]
    nff = dff // ck
    nt = t // tm
    hb = tm // HALO
    gcol = lambda r: pl.BlockSpec((r, ck), lambda i, c: (0, c))
    vcol = lambda r: pl.BlockSpec((r, ck), lambda i, c: (0, nff + c))
    row = pl.BlockSpec((tm, d), lambda i, c: (i, 0))
    tail = pl.BlockSpec((1, HALO, ck), lambda i, c: (i, 0, c))
    y, tg, tv = pl.pallas_call(
        functools.partial(_ffn_seq_kernel, tm=tm),
        out_shape=(jax.ShapeDtypeStruct((t, d), F32),
                   jax.ShapeDtypeStruct((nt, HALO, dff), F32), jax.ShapeDtypeStruct((nt, HALO, dff), F32)),
        grid=(nt, nff),
        in_specs=[row, pl.BlockSpec((HALO, d), lambda i, c: (jnp.maximum(i * hb - 1, 0), 0)), row,
                  gcol(d), vcol(d), gcol(CONV_W), vcol(CONV_W), gcol(1), vcol(1),
                  pl.BlockSpec((ck, d), lambda i, c: (c, 0)),
                  pl.BlockSpec(g_post.shape, lambda i, c: (0, 0))],
        out_specs=(row, tail, tail),
        scratch_shapes=[pltpu.VMEM((tm, d), F32), pltpu.VMEM((tm + HALO, ck), F32),
                        pltpu.VMEM((tm + HALO, ck), F32)],
        compiler_params=_cparams("parallel", "arbitrary"),
        name="ffn_seq",
    )(h2, h2, x1, w_up_bf, w_up_bf, w_conv, w_conv, b_conv2d, b_conv2d, w_down_bf, g_post)
    state = jnp.concatenate([tg[-1, HALO - 2:], tv[-1, HALO - 2:]], axis=1)
    return y, state


def _ffn_step(h2, x1, prev, w_up_bf, w_conv, b_conv2d, w_down_bf, g_post, *, ck):
    t, d = x1.shape
    dff = w_down_bf.shape[0]
    nff = dff // ck
    p0, p1 = prev[:, 0], prev[:, 1]
    gcol = lambda r: pl.BlockSpec((r, ck), lambda c: (0, c))
    vcol = lambda r: pl.BlockSpec((r, ck), lambda c: (0, nff + c))
    row = pl.BlockSpec((t, d), lambda c: (0, 0))
    y, ug, uv = pl.pallas_call(
        _ffn_step_kernel,
        out_shape=(jax.ShapeDtypeStruct((t, d), F32),
                   jax.ShapeDtypeStruct((t, dff), F32), jax.ShapeDtypeStruct((t, dff), F32)),
        grid=(nff,),
        in_specs=[row, gcol(t), gcol(t), vcol(t), vcol(t), row,
                  gcol(d), vcol(d), gcol(CONV_W), vcol(CONV_W), gcol(1), vcol(1),
                  pl.BlockSpec((ck, d), lambda c: (c, 0)),
                  pl.BlockSpec(g_post.shape, lambda c: (0, 0))],
        out_specs=(row, pl.BlockSpec((t, ck), lambda c: (0, c)), pl.BlockSpec((t, ck), lambda c: (0, c))),
        scratch_shapes=[pltpu.VMEM((t, d), F32)],
        compiler_params=_cparams("arbitrary"),
        name="ffn_step",
    )(h2, p0, p1, p0, p1, x1, w_up_bf, w_up_bf, w_conv, w_conv, b_conv2d, b_conv2d, w_down_bf, g_post)
    state = jnp.stack([p1, jnp.concatenate([ug, uv], axis=1)], axis=1)
    return y, state


def _softmax_init(m_sc, l_sc, acc_sc):
    m_sc[...] = jnp.full(m_sc.shape, NEG_INF, F32)
    l_sc[...] = jnp.zeros(l_sc.shape, F32)
    acc_sc[...] = jnp.zeros(acc_sc.shape, F32)


def _softmax_step(s, v, m_sc, l_sc, acc_sc):
    m_old = m_sc[...]
    m_new = jnp.maximum(m_old, jnp.max(s, axis=1, keepdims=True))
    p = jnp.exp(s - m_new)
    alpha = jnp.exp(m_old - m_new)
    l_sc[...] = alpha * l_sc[...] + jnp.sum(p, axis=1, keepdims=True)
    acc_sc[...] = alpha * acc_sc[...] + _dot(p.astype(BF16), v)
    m_sc[...] = m_new


def _top_select(v, lane, forced, rounds):
    sel = forced
    for _ in range(rounds):
        mx = jnp.max(v, axis=1, keepdims=True)
        idx = jnp.min(jnp.where(v == mx, lane, 1 << 20), axis=1, keepdims=True)
        pick = (lane == idx) & (mx > NEG_INF)
        sel = sel | pick
        v = jnp.where(pick, NEG_INF, v)
    return sel


def _masked_softmax(s, mask):
    s = jnp.where(mask, s, NEG_INF)
    m = jnp.max(s, axis=1, keepdims=True)
    m = jnp.where(m > NEG_INF, m, 0.0)
    e = jnp.where(mask, jnp.exp(s - m), 0.0)
    d = jnp.sum(e, axis=1, keepdims=True)
    return e / jnp.where(d > 0, d, 1.0)


def _moba_prompt_kernel(q_ref, km_ref, k_ref, v_ref, o_ref, m_sc, l_sc, acc_sc):
    cur = pl.program_id(1)
    rows = 2 * A_BLOCK
    q = q_ref[...].reshape(rows, LANES)
    gate = _dot(q, km_ref[0])
    lane = lax.broadcasted_iota(jnp.int32, (rows, LANES), 1)
    past = (lane >= HEAD_DIM) & (lane < HEAD_DIM + cur)
    sel = _top_select(jnp.where(past, gate, NEG_INF), lane, lane == HEAD_DIM + cur, A_TOPK)
    bias = jnp.where(sel | (lane < HEAD_DIM), 0.0, MASK_BIAS)
    q_aug = (q.astype(F32) * SCALE + bias).astype(BF16)

    _softmax_init(m_sc, l_sc, acc_sc)

    def body(b, carry):
        off = pl.multiple_of(b * A_BLOCK, A_BLOCK)
        s = _dot_t(q_aug, k_ref[0, pl.ds(off, A_BLOCK), :])
        _softmax_step(s, v_ref[0, pl.ds(off, A_BLOCK), :], m_sc, l_sc, acc_sc)
        return carry

    lax.fori_loop(0, cur, body, 0)

    off = pl.multiple_of(cur * A_BLOCK, A_BLOCK)
    s = _dot_t(q_aug, k_ref[0, pl.ds(off, A_BLOCK), :])
    r = lax.broadcasted_iota(jnp.int32, (rows, A_BLOCK), 0) % A_BLOCK
    col = lax.broadcasted_iota(jnp.int32, (rows, A_BLOCK), 1)
    s = jnp.where(col <= r, s, NEG_INF)
    _softmax_step(s, v_ref[0, pl.ds(off, A_BLOCK), :], m_sc, l_sc, acc_sc)

    o = acc_sc[...] / l_sc[...]
    o_ref[:, 0:HEAD_DIM] = o[:A_BLOCK]
    o_ref[:, HEAD_DIM:LANES] = o[A_BLOCK:]


def _gate_matrix(kmean):
    nblk = kmean.shape[0]
    km = kmean.reshape(nblk, A_KV_HEADS, HEAD_DIM).transpose(1, 2, 0)
    km = jnp.pad(km, ((0, 0), (0, LANES - HEAD_DIM), (HEAD_DIM, LANES - HEAD_DIM - nblk)))
    return km.astype(BF16)


def _moba_prompt(qa, km, kaug, va):
    t = qa.shape[1]
    nt = t // A_BLOCK
    assert nt <= HEAD_DIM, "key-block one-hot occupies 64 lanes"
    rows = 2 * A_BLOCK
    return pl.pallas_call(
        _moba_prompt_kernel,
        out_shape=jax.ShapeDtypeStruct((t, A_HEADS * HEAD_DIM), F32),
        grid=(A_KV_HEADS, nt),
        in_specs=[pl.BlockSpec((2, A_BLOCK, LANES), lambda g, i: (g, i, 0)),
                  pl.BlockSpec((1, LANES, LANES), lambda g, i: (g, 0, 0)),
                  pl.BlockSpec((1, t, LANES), lambda g, i: (g, 0, 0)),
                  pl.BlockSpec((1, t, HEAD_DIM), lambda g, i: (g, 0, 0))],
        out_specs=pl.BlockSpec((A_BLOCK, LANES), lambda g, i: (i, g)),
        scratch_shapes=[pltpu.VMEM((rows, 1), F32), pltpu.VMEM((rows, 1), F32),
                        pltpu.VMEM((rows, HEAD_DIM), F32)],
        compiler_params=_cparams("parallel", "parallel"),
        name="moba_prompt",
    )(qa, km, kaug, va)


def _compress_kernel(x_ref, pelo_ref, pehi_ref, wlo_ref, whi_ref, b1_ref, w2_ref, o_ref, hi_sc, *, nch):
    x = x_ref[0]
    lo = _dot((x + pelo_ref[0]).astype(BF16), wlo_ref[0])
    hi_sc[0:nch] = _dot((x + pehi_ref[0]).astype(BF16), whi_ref[0])
    hi_sc[nch:nch + 8] = jnp.zeros((8, CMP_HIDDEN), F32)
    hid = _gelu(lo + hi_sc[pl.ds(1, nch), :] + b1_ref[0])
    o_ref[0] = _dot(hid.astype(BF16), w2_ref[0]).astype(o_ref.dtype)


def _compress_weights(cmp_pe, cmp_w1, cmp_b1, cmp_w2):
    flat = CMP_STRIDE * HEAD_DIM
    pelo = cmp_pe[:, :CMP_STRIDE].reshape(2, 1, flat)
    pehi = cmp_pe[:, CMP_STRIDE:].reshape(2, 1, flat)
    wlo = cmp_w1[:, :CMP_STRIDE].reshape(2, flat, CMP_HIDDEN).astype(BF16)
    whi = cmp_w1[:, CMP_STRIDE:].reshape(2, flat, CMP_HIDDEN).astype(BF16)
    return pelo, pehi, wlo, whi, cmp_b1.reshape(2, 1, CMP_HIDDEN), cmp_w2.astype(BF16)


def _compress(xch, cw):
    _, nch, flat = xch.shape
    pelo, pehi, wlo, whi, b1, w2 = cw
    kind = lambda shape: pl.BlockSpec((1,) + shape, lambda j: (j // B_KV_HEADS, 0, 0))
    return pl.pallas_call(
        functools.partial(_compress_kernel, nch=nch),
        out_shape=jax.ShapeDtypeStruct((4, nch, HEAD_DIM), BF16),
        grid=(4,),
        in_specs=[pl.BlockSpec((1, nch, flat), lambda j: (j, 0, 0)),
                  kind((1, flat)), kind((1, flat)), kind((flat, CMP_HIDDEN)), kind((flat, CMP_HIDDEN)),
                  kind((1, CMP_HIDDEN)), kind((CMP_HIDDEN, HEAD_DIM))],
        out_specs=pl.BlockSpec((1, nch, HEAD_DIM), lambda j: (j, 0, 0)),
        scratch_shapes=[pltpu.VMEM((nch + 8, CMP_HIDDEN), F32)],
        compiler_params=_cparams("parallel"),
        name="compress",
    )(xch, pelo, pehi, wlo, whi, b1, w2)


def _importance_matrix(ncmp, nsel):
    ratio = SEL_BLOCK // CMP_STRIDE
    n = np.arange(ncmp)[:, None]
    j = np.arange(nsel)[None, :]
    own = (n // ratio == j)
    last = (n % ratio == ratio - 1)
    m = np.where(own & ~last, 1.0, 0.0) + np.where(last & (own | (n // ratio == j - 1)), 0.5, 0.0)
    return jnp.asarray(m, BF16)


NSA_TQ = 128
NSA_KT = 256
WIN_BLOCKS = WINDOW // NSA_TQ + 1


def _nsa_prompt_kernel(*refs, nsel, ncmp):
    (qb_ref, qbr_ref, kc_ref, vc_ref, imp_ref, ks_ref, vs_ref) = refs[:7]
    kw_refs = refs[7:7 + WIN_BLOCKS]
    vw_refs = refs[7 + WIN_BLOCKS:7 + 2 * WIN_BLOCKS]
    g_ref, o_ref, m_sc, l_sc, acc_sc = refs[7 + 2 * WIN_BLOCKS:]
    i = pl.program_id(1)
    rows = B_GROUP * NSA_TQ
    q0 = i * NSA_TQ
    q = qb_ref[...].reshape(rows, HEAD_DIM)
    qr = qbr_ref[...].reshape(rows, LANES)

    s = _dot_t(q, kc_ref[0])
    qpos_c = q0 + lax.broadcasted_iota(jnp.int32, (rows, ncmp), 0) % NSA_TQ
    n_c = lax.broadcasted_iota(jnp.int32, (rows, ncmp), 1)
    p = _masked_softmax(s, n_c * CMP_STRIDE + (CMP_LEN - 1) <= qpos_c)
    o_cmp = _dot(p.astype(BF16), vc_ref[0])

    psum = p[0:NSA_TQ]
    for j in range(1, B_GROUP):
        psum = psum + p[j * NSA_TQ:(j + 1) * NSA_TQ]
    p_hi = psum.astype(BF16)
    p_lo = (psum - p_hi.astype(F32)).astype(BF16)
    imp = _dot(p_hi, imp_ref[...]) + _dot(p_lo, imp_ref[...])
    blk = lax.broadcasted_iota(jnp.int32, (NSA_TQ, nsel), 1)
    cur = (q0 + lax.broadcasted_iota(jnp.int32, (NSA_TQ, nsel), 0)) // SEL_BLOCK
    forced = (blk == 0) | (blk == cur) | (blk == cur - 1)
    cand = jnp.where(blk > cur, NEG_INF, jnp.where(forced, jnp.inf, imp))
    sel = _top_select(cand, blk, jnp.zeros((NSA_TQ, nsel), jnp.bool_), min(SEL_TOPN, nsel))
    selbias = jnp.where(sel, 0.0, MASK_BIAS)

    _softmax_init(m_sc, l_sc, acc_sc)
    kt_last = (q0 + NSA_TQ - 1) // NSA_KT
    tiles_per_win = HEAD_DIM * SEL_BLOCK // NSA_KT
    qr32 = qr.astype(F32)
    for w in range(-(-nsel // HEAD_DIM)):
        nb = min(HEAD_DIM, nsel - w * HEAD_DIM)
        pieces = [jnp.zeros((NSA_TQ, HEAD_DIM), F32), selbias[:, w * HEAD_DIM:w * HEAD_DIM + nb]]
        if nb < HEAD_DIM:
            pieces.append(jnp.zeros((NSA_TQ, HEAD_DIM - nb), F32))
        bias_w = jnp.concatenate(pieces, axis=1)
        q_aug = (qr32 + jnp.concatenate([bias_w] * B_GROUP, axis=0)).astype(BF16)

        def body(kt, carry, q_aug=q_aug):
            off = pl.multiple_of(kt * NSA_KT, NSA_KT)
            s = _dot_t(q_aug, ks_ref[0, pl.ds(off, NSA_KT), :])
            _softmax_step(s, vs_ref[0, pl.ds(off, NSA_KT), :], m_sc, l_sc, acc_sc)
            return carry

        lax.fori_loop(w * tiles_per_win, jnp.minimum((w + 1) * tiles_per_win, kt_last), body, 0)

        @pl.when(kt_last // tiles_per_win == w)
        def _(q_aug=q_aug):
            off = pl.multiple_of(kt_last * NSA_KT, NSA_KT)
            s = _dot_t(q_aug, ks_ref[0, pl.ds(off, NSA_KT), :])
            qpos = q0 + lax.broadcasted_iota(jnp.int32, (rows, NSA_KT), 0) % NSA_TQ
            kpos = off + lax.broadcasted_iota(jnp.int32, (rows, NSA_KT), 1)
            s = jnp.where(kpos <= qpos, s, NEG_INF)
            _softmax_step(s, vs_ref[0, pl.ds(off, NSA_KT), :], m_sc, l_sc, acc_sc)

    o_sel = acc_sc[...] / l_sc[...]

    kband = jnp.concatenate([r[0] for r in kw_refs], axis=0)
    vband = jnp.concatenate([r[0] for r in vw_refs], axis=0)
    nband = WIN_BLOCKS * NSA_TQ
    s = _dot_t(qr[:, :HEAD_DIM], kband)
    qpos = q0 + lax.broadcasted_iota(jnp.int32, (rows, nband), 0) % NSA_TQ
    kpos = q0 - WINDOW + lax.broadcasted_iota(jnp.int32, (rows, nband), 1)
    p = _masked_softmax(s, (kpos <= qpos) & (kpos >= qpos - WINDOW) & (kpos >= 0))
    o_win = _dot(p.astype(BF16), vband)

    gts = g_ref[0]
    for j in range(B_GROUP):
        sl = slice(j * NSA_TQ, (j + 1) * NSA_TQ)
        o = (gts[:, 3 * j:3 * j + 1] * o_cmp[sl] + gts[:, 3 * j + 1:3 * j + 2] * o_sel[sl]
             + gts[:, 3 * j + 2:3 * j + 3] * o_win[sl])
        o_ref[:, j * HEAD_DIM:(j + 1) * HEAD_DIM] = o


def _nsa_prompt(qb, qbr, kvcmp, ksaug, vs, kw, vw, gates):
    t = qb.shape[1]
    nt = t // NSA_TQ
    nsel = t // SEL_BLOCK
    ncmp = kvcmp.shape[1]
    rows = B_GROUP * NSA_TQ
    impm = _importance_matrix(ncmp, nsel)
    res = lambda w: pl.BlockSpec((1, t, w), lambda g, i: (g, 0, 0))
    band = [pl.BlockSpec((1, NSA_TQ, HEAD_DIM),
                         functools.partial(lambda g, i, j: (g, jnp.maximum(i - (WIN_BLOCKS - 1) + j, 0), 0), j=j))
            for j in range(WIN_BLOCKS)]
    return pl.pallas_call(
        functools.partial(_nsa_prompt_kernel, nsel=nsel, ncmp=ncmp),
        out_shape=jax.ShapeDtypeStruct((t, B_HEADS * HEAD_DIM), F32),
        grid=(B_KV_HEADS, nt),
        in_specs=[pl.BlockSpec((B_GROUP, NSA_TQ, HEAD_DIM), lambda g, i: (g, i, 0)),
                  pl.BlockSpec((B_GROUP, NSA_TQ, LANES), lambda g, i: (g, i, 0)),
                  pl.BlockSpec((1, ncmp, HEAD_DIM), lambda g, i: (g, 0, 0)),
                  pl.BlockSpec((1, ncmp, HEAD_DIM), lambda g, i: (B_KV_HEADS + g, 0, 0)),
                  pl.BlockSpec(impm.shape, lambda g, i: (0, 0)),
                  res(LANES), res(HEAD_DIM)] + band + band +
                 [pl.BlockSpec((1, NSA_TQ, 3 * B_GROUP), lambda g, i: (g, i, 0))],
        out_specs=pl.BlockSpec((NSA_TQ, B_GROUP * HEAD_DIM), lambda g, i: (i, g)),
        scratch_shapes=[pltpu.VMEM((rows, 1), F32), pltpu.VMEM((rows, 1), F32),
                        pltpu.VMEM((rows, HEAD_DIM), F32)],
        compiler_params=_cparams("parallel", "parallel"),
        name="nsa_prompt",
    )(qb, qbr, kvcmp, kvcmp, impm, ksaug, vs, *([kw] * WIN_BLOCKS), *([vw] * WIN_BLOCKS), gates)


def _top_indices(v, lane, rounds):
    out = jnp.zeros((v.shape[0], LANES), jnp.int32)
    slot = lax.broadcasted_iota(jnp.int32, out.shape, 1)
    for r in range(rounds):
        mx = jnp.max(v, axis=1, keepdims=True)
        idx = jnp.min(jnp.where(v == mx, lane, 1 << 20), axis=1, keepdims=True)
        out = jnp.where(slot == r, idx, out)
        v = jnp.where(lane == idx, NEG_INF, v)
    return out


def _bf16_round(x):
    return x.astype(BF16).astype(F32)


def _both_halves(x):
    return x + pltpu.roll(x, HEAD_DIM, 1)


def _moba_kmean_kernel(pt_ref, *refs, pg):
    pages, o_ref = refs[:pg], refs[pg]
    per_blk = A_BLOCK // pages[0].shape[1]
    for j in range(pg // per_blk):
        acc = jnp.zeros((1, pages[0].shape[2]), F32)
        for r in pages[j * per_blk:(j + 1) * per_blk]:
            acc = acc + jnp.sum(r[0], axis=0, keepdims=True)
        o_ref[0, j:j + 1, :] = acc * (1.0 / A_BLOCK)


def _moba_kmean(cache2d, page_table, *, pg):
    db, npg = page_table.shape
    page = cache2d.shape[1]
    kw = A_KV_HEADS * HEAD_DIM
    nb = pg * page // A_BLOCK
    specs = [pl.BlockSpec((1, page, kw), functools.partial(lambda b, s, pt, j: (pt[b, s * pg + j], 0, 0), j=j))
             for j in range(pg)]
    return pl.pallas_call(
        functools.partial(_moba_kmean_kernel, pg=pg),
        out_shape=jax.ShapeDtypeStruct((db, npg * page // A_BLOCK, kw), F32),
        grid_spec=pltpu.PrefetchScalarGridSpec(
            num_scalar_prefetch=1, grid=(db, npg // pg), in_specs=specs,
            out_specs=pl.BlockSpec((1, nb, kw), lambda b, s, pt: (b, s, 0))),
        compiler_params=_cparams("parallel", "arbitrary"),
        name="moba_kmean",
    )(page_table, *([cache2d] * pg))


def _moba_gate_kernel(q_ref, km_ref, idx_ref, *, nblk):
    q = q_ref[0][:, :HEAD_DIM].astype(BF16)
    head = lax.broadcasted_iota(jnp.int32, (A_HEADS, nblk), 0)
    gate = jnp.zeros((A_HEADS, nblk), F32)
    for g in range(A_KV_HEADS):
        kmg = km_ref[0][:, g * HEAD_DIM:(g + 1) * HEAD_DIM].astype(BF16)
        gate = jnp.where(head // (A_HEADS // A_KV_HEADS) == g, _dot_t(q, kmg), gate)
    lane = lax.broadcasted_iota(jnp.int32, (A_HEADS, nblk), 1)
    idx_ref[0] = _top_indices(gate, lane, A_TOPK)


def _moba_gate(qa_rows, kmean):
    db, nblk, kw = kmean.shape
    return pl.pallas_call(
        functools.partial(_moba_gate_kernel, nblk=nblk),
        out_shape=jax.ShapeDtypeStruct((db, A_HEADS, LANES), jnp.int32),
        grid=(db,),
        in_specs=[pl.BlockSpec((1, A_HEADS, LANES), lambda b: (b, 0, 0)),
                  pl.BlockSpec((1, nblk, kw), lambda b: (b, 0, 0))],
        out_specs=pl.BlockSpec((1, A_HEADS, LANES), lambda b: (b, 0, 0)),
        compiler_params=_cparams("parallel"),
        name="moba_gate",
    )(qa_rows, kmean)


def _vec_softmax_step(s, v, m_sc, l_sc, acc_sc):
    m_old = m_sc[...]
    m_new = jnp.maximum(m_old, jnp.max(s, axis=0, keepdims=True))
    p = jnp.exp(s - m_new)
    alpha = jnp.exp(m_old - m_new)
    l_sc[...] = alpha * l_sc[...] + jnp.sum(p, axis=0, keepdims=True)
    acc_sc[...] = alpha * acc_sc[...] + jnp.sum(_bf16_round(p) * v, axis=0, keepdims=True)
    m_sc[...] = m_new


def _moba_sample_kernel(pt_ref, ix_ref, q_ref, *refs, npage):
    k_refs, v_refs = refs[:npage], refs[npage:2 * npage]
    kn_ref, vn_ref, o_ref, m_sc, l_sc, acc_sc = refs[2 * npage:]
    h, s_id = pl.program_id(1), pl.program_id(2)
    half = (h // (A_HEADS // A_KV_HEADS)) % 2
    lane = lax.broadcasted_iota(jnp.int32, (1, LANES), 1)
    in_half = lane // HEAD_DIM == half
    q = jnp.where(in_half, _both_halves(q_ref[0, pl.ds(h, 1), :]) * SCALE, 0.0)

    @pl.when(s_id == 0)
    def _():
        _softmax_init(m_sc, l_sc, acc_sc)

    for k_ref, v_ref in zip(k_refs, v_refs):
        s = jnp.sum(_bf16_round(k_ref[0]) * q, axis=1, keepdims=True)
        _vec_softmax_step(s, _bf16_round(v_ref[0]), m_sc, l_sc, acc_sc)

    @pl.when(s_id == pl.num_programs(2) - 1)
    def _():
        s = jnp.sum(_bf16_round(kn_ref[0, 0]) * q, axis=1, keepdims=True)
        _vec_softmax_step(s, _bf16_round(vn_ref[0, 0]), m_sc, l_sc, acc_sc)
        o = jnp.where(in_half, acc_sc[...] / l_sc[...], 0.0)
        o_ref[0, pl.ds(h, 1), :] = _both_halves(o)[:, :HEAD_DIM]


def _moba_sample(cache2d, page_table, idx, qa_rows, kva_new):
    db = page_table.shape[0]
    page = cache2d.shape[1]
    npage = A_BLOCK // page
    pair = A_HEADS // 2

    def kv_spec(j, lane_blk0):
        def imap(b, h, s, pt, ix):
            blk = ix[(b * A_HEADS + h) * A_TOPK + s]
            return (pt[b, blk * npage + j], 0, lane_blk0 + h // pair)
        return pl.BlockSpec((1, page, LANES), imap)

    new_spec = lambda off: pl.BlockSpec((1, 1, 1, LANES), lambda b, h, s, pt, ix: (b, off + h // pair, 0, 0))
    return pl.pallas_call(
        functools.partial(_moba_sample_kernel, npage=npage),
        out_shape=jax.ShapeDtypeStruct((db, A_HEADS, HEAD_DIM), F32),
        grid_spec=pltpu.PrefetchScalarGridSpec(
            num_scalar_prefetch=2, grid=(db, A_HEADS, A_TOPK),
            in_specs=[pl.BlockSpec((1, A_HEADS, LANES), lambda b, h, s, pt, ix: (b, 0, 0))]
                     + [kv_spec(j, 0) for j in range(npage)] + [kv_spec(j, 2) for j in range(npage)]
                     + [new_spec(0), new_spec(2)],
            out_specs=pl.BlockSpec((1, A_HEADS, HEAD_DIM), lambda b, h, s, pt, ix: (b, 0, 0)),
            scratch_shapes=[pltpu.VMEM((1, 1), F32), pltpu.VMEM((1, 1), F32), pltpu.VMEM((1, LANES), F32)]),
        compiler_params=_cparams("parallel", "arbitrary", "arbitrary"),
        name="moba_sample",
    )(page_table, idx, qa_rows, *([cache2d] * (2 * npage)), kva_new, kva_new)


def _nsa_flatten_kernel(pt_ref, *refs, pg, nch):
    pages = refs[:pg]
    pelo_ref, pehi_ref, wlo_ref, whi_ref, b1_ref, w2_ref, o_ref, x_sc, hi_sc, rows_sc = refs[pg:]
    step = pl.program_id(1)
    page = pages[0].shape[1]
    nc = pg * page // CMP_STRIDE
    base = pl.multiple_of(step * nc, nc)
    lane = lax.broadcasted_iota(jnp.int32, (nc, LANES), 1)
    lo_half = lane < HEAD_DIM
    for j, r in enumerate(pages):
        for pr in range(2):
            rows_sc[pr, j * page:(j + 1) * page, :] = r[0, :, pr * LANES:(pr + 1) * LANES]
    for u in range(CMP_STRIDE // 2):
        for pr in range(2):
            ap = rows_sc[pr, pl.ds(2 * u, nc, stride=CMP_STRIDE), :]
            bp = rows_sc[pr, pl.ds(2 * u + 1, nc, stride=CMP_STRIDE), :]
            x_sc[2 * pr, pl.ds(base, nc), u * LANES:(u + 1) * LANES] = jnp.where(
                lo_half, ap, pltpu.roll(bp, HEAD_DIM, 1))
            x_sc[2 * pr + 1, pl.ds(base, nc), u * LANES:(u + 1) * LANES] = jnp.where(
                lo_half, pltpu.roll(ap, HEAD_DIM, 1), bp)

    @pl.when(step == pl.num_programs(1) - 1)
    def _():
        hi_sc[nch:nch + 8] = jnp.zeros((8, CMP_HIDDEN), F32)
        for j in range(4):
            c = j // B_KV_HEADS
            x = x_sc[j]
            lo = _dot((x + pelo_ref[c]).astype(BF16), wlo_ref[c])
            hi_sc[0:nch] = _dot((x + pehi_ref[c]).astype(BF16), whi_ref[c])
            hid = _gelu(lo + hi_sc[pl.ds(1, nch), :] + b1_ref[c])
            o_ref[0, j] = _dot(hid.astype(BF16), w2_ref[c]).astype(o_ref.dtype)


def _nsa_sample_compress(cache2d, page_table, cw, *, pg):
    db, npg = page_table.shape
    page = cache2d.shape[1]
    nch = npg * page // CMP_STRIDE
    flat = CMP_STRIDE * HEAD_DIM
    pelo, pehi, wlo, whi, b1, w2 = cw
    full = lambda a: pl.BlockSpec(a.shape, lambda b, s, pt: (0,) * a.ndim)
    specs = [pl.BlockSpec((1, page, 2 * LANES), functools.partial(lambda b, s, pt, j: (pt[b, s * pg + j], 0, 0), j=j))
             for j in range(pg)]
    return pl.pallas_call(
        functools.partial(_nsa_flatten_kernel, pg=pg, nch=nch),
        out_shape=jax.ShapeDtypeStruct((db, 4, nch, HEAD_DIM), BF16),
        grid_spec=pltpu.PrefetchScalarGridSpec(
            num_scalar_prefetch=1, grid=(db, npg // pg),
            in_specs=specs + [full(a) for a in (pelo, pehi, wlo, whi, b1, w2)],
            out_specs=pl.BlockSpec((1, 4, nch, HEAD_DIM), lambda b, s, pt: (b, 0, 0, 0)),
            scratch_shapes=[pltpu.VMEM((4, nch, flat), F32), pltpu.VMEM((nch + 8, CMP_HIDDEN), F32),
                            pltpu.VMEM((2, pg * page, LANES), F32)]),
        compiler_params=_cparams("parallel", "arbitrary"),
        name="nsa_sample_compress",
    )(page_table, *([cache2d] * pg), pelo, pehi, wlo, whi, b1, w2)


def _rows_by_group(per_group):
    head = lax.broadcasted_iota(jnp.int32, per_group[0].shape, 0)
    out = per_group[0]
    for g in range(1, B_KV_HEADS):
        out = jnp.where(head // B_GROUP == g, per_group[g], out)
    return out


def _nsa_sample_select_kernel(q_ref, qr_ref, kv_ref, imp_ref, win_ref, wnew_ref, g_ref, idx_ref, ocw_ref,
                              *, past, nsel_past, ncmp):
    q = q_ref[0][:, :HEAD_DIM].astype(BF16)
    qr = qr_ref[0][:, :HEAD_DIM]
    gts = g_ref[0]
    glane = lax.broadcasted_iota(jnp.int32, (B_HEADS, LANES), 1)
    ghead = lax.broadcasted_iota(jnp.int32, (B_HEADS, LANES), 0)
    gate = lambda c: jnp.sum(jnp.where(glane == 3 * ghead + c, gts, 0.0), axis=1, keepdims=True)

    s = _rows_by_group([_dot_t(q, kv_ref[0, g]) for g in range(B_KV_HEADS)])
    n_c = lax.broadcasted_iota(jnp.int32, (B_HEADS, ncmp), 1)
    p = _masked_softmax(s, n_c * CMP_STRIDE + (CMP_LEN - 1) <= past)
    o_cmp = _rows_by_group([_dot(p.astype(BF16), kv_ref[0, B_KV_HEADS + g]) for g in range(B_KV_HEADS)])

    head = lax.broadcasted_iota(jnp.int32, (B_HEADS, ncmp), 0)
    psum = jnp.zeros((B_HEADS, ncmp), F32)
    for g in range(B_KV_HEADS):
        tot = jnp.sum(jnp.where(head // B_GROUP == g, p, 0.0), axis=0, keepdims=True)
        psum = jnp.where(head == g, tot, psum)
    p_hi = psum.astype(BF16)
    p_lo = (psum - p_hi.astype(F32)).astype(BF16)
    imp = _dot(p_hi, imp_ref[...]) + _dot(p_lo, imp_ref[...])
    blk = lax.broadcasted_iota(jnp.int32, (B_HEADS, nsel_past), 1)
    cand = jnp.where((blk == 0) | (blk == nsel_past - 1), jnp.inf, imp)
    idx_ref[0] = _top_indices(cand, blk, min(SEL_TOPN, nsel_past + 1) - 1)

    wk = [win_ref[0][:, g * HEAD_DIM:(g + 1) * HEAD_DIM] for g in range(B_KV_HEADS)]
    wv = [win_ref[0][:, (B_KV_HEADS + g) * HEAD_DIM:(B_KV_HEADS + g + 1) * HEAD_DIM] for g in range(B_KV_HEADS)]
    nk = [wnew_ref[0][:, g * HEAD_DIM:(g + 1) * HEAD_DIM] for g in range(B_KV_HEADS)]
    nv = [wnew_ref[0][:, (B_KV_HEADS + g) * HEAD_DIM:(B_KV_HEADS + g + 1) * HEAD_DIM] for g in range(B_KV_HEADS)]
    s_w = _rows_by_group([_dot_t(qr.astype(BF16), k.astype(BF16)) for k in wk])
    s_n = _rows_by_group([jnp.sum(qr * _bf16_round(k), axis=1, keepdims=True) for k in nk])
    m = jnp.maximum(jnp.max(s_w, axis=1, keepdims=True), s_n)
    e_w, e_n = jnp.exp(s_w - m), jnp.exp(s_n - m)
    d = jnp.sum(e_w, axis=1, keepdims=True) + e_n
    o_win = _rows_by_group([_dot(e_w.astype(BF16), v.astype(BF16)) + _bf16_round(e_n) * _bf16_round(nvg)
                            for v, nvg in zip(wv, nv)]) / d

    ocw_ref[0] = jnp.concatenate([gate(0) * o_cmp, gate(2) * o_win], axis=1)


def _nsa_sample_select(qb_rows, qbr_rows, kvcmp, win_cache, win_new, gates, *, past):
    db, _, ncmp, _ = kvcmp.shape
    nsel_past = past // SEL_BLOCK
    impm = _importance_matrix(ncmp, nsel_past)
    nwin = win_cache.shape[1]
    row3 = lambda a: pl.BlockSpec((1,) + a.shape[1:], lambda b: (b,) + (0,) * (a.ndim - 1))
    kern = functools.partial(_nsa_sample_select_kernel, past=past, nsel_past=nsel_past, ncmp=ncmp)
    return pl.pallas_call(
        kern,
        out_shape=(jax.ShapeDtypeStruct((db, B_HEADS, LANES), jnp.int32),
                   jax.ShapeDtypeStruct((db, B_HEADS, LANES), F32)),
        grid=(db,),
        in_specs=[row3(qb_rows), row3(qbr_rows), row3(kvcmp), pl.BlockSpec(impm.shape, lambda b: (0, 0)),
                  row3(win_cache), row3(win_new), row3(gates)],
        out_specs=(pl.BlockSpec((1, B_HEADS, LANES), lambda b: (b, 0, 0)),
                   pl.BlockSpec((1, B_HEADS, LANES), lambda b: (b, 0, 0))),
        compiler_params=_cparams("parallel"),
        name="nsa_sample_select",
    )(qb_rows, qbr_rows, kvcmp, impm, win_cache, win_new, gates)


def _nsa_sample_attend_kernel(pt_ref, ix_ref, qr_ref, k_ref, v_ref, new_ref, g_ref, ocw_ref, o_ref,
                              m_sc, l_sc, acc_sc):
    g, s_id = pl.program_id(1), pl.program_id(2)
    lane = lax.broadcasted_iota(jnp.int32, (1, LANES), 1)
    in_half = lane // HEAD_DIM == g
    qs = jnp.where(in_half, _both_halves(qr_ref[0, pl.ds(g * B_GROUP, B_GROUP), :]), 0.0)

    @pl.when(s_id == 0)
    def _():
        _softmax_init(m_sc, l_sc, acc_sc)

    def step(k, v):
        for j in range(B_GROUP):
            s = jnp.sum(k * qs[j:j + 1], axis=1, keepdims=True)
            _vec_softmax_step(s, v, m_sc.at[j:j + 1], l_sc.at[j:j + 1], acc_sc.at[j:j + 1])

    step(_bf16_round(k_ref[0]), _bf16_round(v_ref[0]))

    @pl.when(s_id == pl.num_programs(2) - 1)
    def _():
        new = _bf16_round(new_ref[0])
        step(new[:, 2 * LANES:3 * LANES], new[:, 3 * LANES:4 * LANES])
        o_sel = jnp.where(in_half, acc_sc[...] / l_sc[...], 0.0)
        o_sel = _both_halves(o_sel)[:, :HEAD_DIM]
        gts = g_ref[0]
        hrow = lax.broadcasted_iota(jnp.int32, (B_GROUP, LANES), 0) + g * B_GROUP
        glane = lax.broadcasted_iota(jnp.int32, (B_GROUP, LANES), 1)
        g1 = jnp.sum(jnp.where(glane == 3 * hrow + 1, gts, 0.0), axis=1, keepdims=True)
        ocw = ocw_ref[0, pl.ds(g * B_GROUP, B_GROUP), :]
        o_ref[0, pl.ds(g * B_GROUP, B_GROUP), :] = g1 * o_sel + ocw[:, :HEAD_DIM] + ocw[:, HEAD_DIM:]


def _nsa_sample_attend(cache_half, page_table, idx, qbr_rows, nsa_new, gates, ocw, *, nslot, per_page):
    db = page_table.shape[0]

    def kv_spec(lane_blk):
        def imap(b, g, s, pt, ix):
            blk = ix[(b * B_KV_HEADS + g) * nslot + s]
            return (pt[b, blk // per_page] * per_page + blk % per_page, 0, lane_blk)
        return pl.BlockSpec((1, SEL_BLOCK, LANES), imap)

    row = lambda a: pl.BlockSpec((1,) + a.shape[1:], lambda b, g, s, pt, ix: (b,) + (0,) * (a.ndim - 1))
    return pl.pallas_call(
        _nsa_sample_attend_kernel,
        out_shape=jax.ShapeDtypeStruct((db, B_HEADS, HEAD_DIM), F32),
        grid_spec=pltpu.PrefetchScalarGridSpec(
            num_scalar_prefetch=2, grid=(db, B_KV_HEADS, nslot),
            in_specs=[row(qbr_rows), kv_spec(2), kv_spec(3), row(nsa_new), row(gates), row(ocw)],
            out_specs=pl.BlockSpec((1, B_HEADS, HEAD_DIM), lambda b, g, s, pt, ix: (b, 0, 0)),
            scratch_shapes=[pltpu.VMEM((B_GROUP, 1), F32), pltpu.VMEM((B_GROUP, 1), F32),
                            pltpu.VMEM((B_GROUP, LANES), F32)]),
        compiler_params=_cparams("parallel", "arbitrary", "arbitrary"),
        name="nsa_sample_attend",
    )(page_table, idx, qbr_rows, cache_half, cache_half, nsa_new, gates, ocw)


def _prep_weights(l, g_mix_pre, w_in, b_gate, cmp_pe, cmp_w1, cmp_b1, cmp_w2, w_out, g_mix_post,
                  g_ffn_pre, w_up, w_conv, b_conv, w_down, g_ffn_post):
    d_in = w_in.shape[-1]
    n_gate = b_gate.shape[-1]
    return dict(
        g_mix_pre=g_mix_pre[l][None],
        w_in=jnp.pad(w_in[l], ((0, 0), (0, D_IN_PAD - d_in))).astype(BF16),
        b_gate=jnp.pad(b_gate[l], (0, LANES - n_gate))[None],
        cmp=_compress_weights(cmp_pe[l], cmp_w1[l], cmp_b1[l], cmp_w2[l]),
        w_out=w_out[l].astype(BF16), g_mix_post=g_mix_post[l][None], g_ffn_pre=g_ffn_pre[l][None],
        w_up=w_up[l].astype(BF16), w_conv=w_conv[l], b_conv=b_conv[l][None],
        w_down=w_down[l].astype(BF16), g_ffn_post=g_ffn_post[l][None],
    )


def _group_gates(gates):
    t = gates.shape[0]
    return gates[:, :3 * B_HEADS].reshape(t, B_KV_HEADS, 3 * B_GROUP).transpose(1, 0, 2)


FFN_CK = 256


def _prompt_layer(x2d, w):
    t = x2d.shape[0]
    (kva, nsa, win, gates, qa, kaug, va, qb, qbr, ksaug, vs, kw, vw, kcvc, kmean) = _inproj(
        x2d, jnp.arange(t), w["g_mix_pre"], w["w_in"], w["b_gate"], tm=512, with_kmean=True)
    o_a = _moba_prompt(qa, _gate_matrix(kmean.reshape(t // A_BLOCK, A_KV_HEADS * HEAD_DIM)), kaug, va)
    kvcmp = _compress(kcvc.reshape(4, t // CMP_STRIDE, CMP_STRIDE * HEAD_DIM), w["cmp"])
    o_b = _nsa_prompt(qb, qbr, kvcmp, ksaug, vs, kw, vw, _group_gates(gates))
    x1, h2 = _outproj(o_a, o_b, x2d, w["w_out"], w["g_mix_post"], w["g_ffn_pre"], tm=512)
    y, conv_state = _ffn_seq(h2, x1, w["w_up"], w["w_conv"], w["b_conv"], w["w_down"], w["g_ffn_post"],
                             tm=512, ck=FFN_CK)
    keep = min(WINDOW, t)
    return (y, kva.reshape(t, 2, A_KV_HEADS, HEAD_DIM), nsa.reshape(t, 4, B_KV_HEADS, HEAD_DIM),
            win[t - keep:].reshape(keep, 2, B_KV_HEADS, HEAD_DIM), conv_state)


PAGES_PER_STEP = 16


def _sample_layer(x2d, cache_moba, cache_nsa, win_cache, conv_state, page_table, w):
    db = x2d.shape[0]
    n_pool, page = cache_moba.shape[:2]
    npg = page_table.shape[1]
    past = npg * page
    wb = win_cache.shape[1]
    assert wb == WINDOW and past % A_BLOCK == 0 and past // A_BLOCK >= A_TOPK and A_BLOCK % page == 0
    assert page % SEL_BLOCK == 0
    pg = min(PAGES_PER_STEP, npg)
    (kva, nsa, win, gates, qa, _, _, qb, qbr, _, _, _, _, _, _) = _inproj(
        x2d, jnp.full((db,), past, jnp.int32), w["g_mix_pre"], w["w_in"], w["b_gate"], tm=db, with_kmean=False)
    rows = lambda a: a.transpose(1, 0, 2).astype(F32)
    qa_rows, qb_rows, qbr_rows = rows(qa), rows(qb), rows(qbr)

    moba2d = cache_moba.reshape(n_pool, page, 2 * A_KV_HEADS * HEAD_DIM)
    kmean = _moba_kmean(moba2d, page_table, pg=pg)
    idx_a = _moba_gate(qa_rows, kmean)[:, :, :A_TOPK].reshape(-1)
    o_a = _moba_sample(moba2d, page_table, idx_a, qa_rows, kva.reshape(db, 4, 1, LANES))

    nsa2d = cache_nsa.reshape(n_pool, page, 4 * B_KV_HEADS * HEAD_DIM)
    kvcmp = _nsa_sample_compress(nsa2d, page_table, w["cmp"], pg=pg)
    gates3 = gates[:, None, :]
    win2d = win_cache.reshape(db, wb, 2 * B_KV_HEADS * HEAD_DIM)
    win_new = win[:, None, :]
    idx_b, ocw = _nsa_sample_select(qb_rows, qbr_rows, kvcmp, win2d, win_new, gates3, past=past)
    nslot = min(SEL_TOPN, past // SEL_BLOCK + 1) - 1
    per_page = page // SEL_BLOCK
    o_b = _nsa_sample_attend(cache_nsa.reshape(n_pool * per_page, SEL_BLOCK, 4 * B_KV_HEADS * HEAD_DIM),
                             page_table, idx_b[:, :B_KV_HEADS, :nslot].reshape(-1), qbr_rows,
                             nsa[:, None, :], gates3, ocw, nslot=nslot, per_page=per_page)

    x1, h2 = _outproj(o_a.reshape(db, -1), o_b.reshape(db, -1), x2d, w["w_out"], w["g_mix_post"],
                      w["g_ffn_pre"], tm=db)
    y, conv_new = _ffn_step(h2, x1, conv_state, w["w_up"], w["w_conv"], w["b_conv"], w["w_down"],
                            w["g_ffn_post"], ck=FFN_CK)
    keep = min(WINDOW, wb + 1)
    win_all = jnp.concatenate([win2d, win_new], axis=1)[:, wb + 1 - keep:]
    return (y, kva.reshape(db, 2, A_KV_HEADS, HEAD_DIM), nsa.reshape(db, 4, B_KV_HEADS, HEAD_DIM),
            win_all.reshape(db, keep, 2, B_KV_HEADS, HEAD_DIM), conv_new)


def kernel(x_prompt, x_sample, cache_moba_kv, cache_nsa_kv, cache_nsa_win_kv, state_ffn_conv, page_table,
           g_mix_pre, w_in, b_gate, cmp_pe, cmp_w1, cmp_b1, cmp_w2, w_out, g_mix_post, g_ffn_pre, w_up,
           w_conv, b_conv, w_down, g_ffn_post):
    depth = w_in.shape[0]
    assert depth == 1 and x_prompt.shape[0] == 1
    w = _prep_weights(0, g_mix_pre, w_in, b_gate, cmp_pe, cmp_w1, cmp_b1, cmp_w2, w_out, g_mix_post,
                      g_ffn_pre, w_up, w_conv, b_conv, w_down, g_ffn_post)
    y_p, moba_p, nsa_p, win_p, conv_p = _prompt_layer(x_prompt[0], w)
    assert x_sample.shape[1] == 1
    y_s, moba_s, nsa_s, win_s, conv_s = _sample_layer(
        x_sample[:, 0], cache_moba_kv[0], cache_nsa_kv[0], cache_nsa_win_kv[0], state_ffn_conv[0],
        page_table, w)
    return (y_p[None], y_s[:, None], moba_p[None, None], moba_s[None, :, None],
            nsa_p[None, None], nsa_s[None, :, None], win_p[None, None], win_s[None],
            conv_p[None, None], conv_s[None])
```

```python
import functools
import math

import numpy as np
import jax
import jax.numpy as jnp
from jax import lax
from jax.experimental import pallas as pl
from jax.experimental.pallas import tpu as pltpu

HEAD_DIM = 64
ROPE_DIM = HEAD_DIM // 4
ROPE_THETA = 500000.0
A_HEADS = 8
A_KV_HEADS = 4
A_BLOCK = 256
A_TOPK = 3
B_HEADS = 8
B_KV_HEADS = 2
B_GROUP = B_HEADS // B_KV_HEADS
CMP_LEN = 32
CMP_STRIDE = 16
CMP_HIDDEN = 128
SEL_BLOCK = 64
SEL_TOPN = 16
WINDOW = 512
CONV_W = 3
RMS_EPS = 1e-6
SCALE = HEAD_DIM ** -0.5

LANES = 128
VMEM_LIMIT = 56 * 1024 * 1024
MASK_BIAS = -32768.0
NEG_INF = float("-inf")

BF16 = jnp.bfloat16
F32 = jnp.float32


def _cparams(*sem):
    return pltpu.CompilerParams(dimension_semantics=sem, vmem_limit_bytes=VMEM_LIMIT)


def _dot(a, b):
    return jnp.dot(a, b, preferred_element_type=F32)


def _dot_t(a, b):
    return lax.dot_general(a, b, (((1,), (1,)), ((), ())), preferred_element_type=F32)


def _rms(x, g):
    y = x * lax.rsqrt(jnp.mean(x * x, axis=-1, keepdims=True) + RMS_EPS)
    return y * g


def _gelu(x):
    c = math.sqrt(2.0 / math.pi)
    return 0.5 * x * (1.0 + jnp.tanh(c * (x + 0.044715 * (x * x * x))))


def _rope128(x, c, sa, sb):
    return x * c + pltpu.roll(x, LANES - ROPE_DIM // 2, 1) * sa + pltpu.roll(x, ROPE_DIM // 2, 1) * sb


C_QA, C_KA, C_VA, C_QB = 0, 512, 768, 1024
C_KC, C_VC, C_KS, C_VS, C_KW, C_VW, C_G = 1536, 1664, 1792, 1920, 2048, 2176, 2304
D_IN_PAD = 2432


def _inproj_kernel(x_ref, g_ref, w_ref, bg_ref, c_ref, sa_ref, sb_ref,
                   kva_ref, nsa_ref, win_ref, gates_ref,
                   qa_ref, kaug_ref, va_ref, qb_ref, qbr_ref, ksaug_ref, vs_ref, kw_ref, vw_ref,
                   kcvc_ref, kmean_ref, *, tm, with_kmean):
    i = pl.program_id(0)
    h = _rms(x_ref[...], g_ref[...]).astype(BF16)
    c, sa, sb = c_ref[...], sa_ref[...], sb_ref[...]

    def proj(c0, width):
        return _dot(h, w_ref[:, c0:c0 + width])

    lane = lax.broadcasted_iota(jnp.int32, (tm, LANES), 1)
    row = lax.broadcasted_iota(jnp.int32, (tm, LANES), 0) + i * tm
    zeros64 = jnp.zeros((tm, HEAD_DIM), BF16)

    for p in range(4):
        q = _rope128(proj(C_QA + p * LANES, LANES), c, sa, sb)
        qa_ref[2 * p] = jnp.concatenate([q[:, :HEAD_DIM].astype(BF16), zeros64], axis=1)
        qa_ref[2 * p + 1] = jnp.concatenate([q[:, HEAD_DIM:].astype(BF16), zeros64], axis=1)

    a_onehot = (lane - HEAD_DIM == row // A_BLOCK).astype(BF16)
    ksum = []
    for p in range(2):
        k = _rope128(proj(C_KA + p * LANES, LANES), c, sa, sb)
        v = proj(C_VA + p * LANES, LANES)
        kva_ref[:, p * LANES:(p + 1) * LANES] = k
        kva_ref[:, 256 + p * LANES:256 + (p + 1) * LANES] = v
        kb = k.astype(BF16)
        vb = v.astype(BF16)
        kaug_ref[2 * p] = jnp.concatenate([kb[:, :HEAD_DIM], a_onehot[:, HEAD_DIM:]], axis=1)
        kaug_ref[2 * p + 1] = jnp.concatenate([kb[:, HEAD_DIM:], a_onehot[:, HEAD_DIM:]], axis=1)
        va_ref[2 * p] = vb[:, :HEAD_DIM]
        va_ref[2 * p + 1] = vb[:, HEAD_DIM:]
        if with_kmean:
            ksum.append(jnp.sum(k.reshape(tm // A_BLOCK, A_BLOCK, LANES), axis=1))
    if with_kmean:
        kmean_ref[0] = jnp.concatenate(ksum, axis=1) * (1.0 / A_BLOCK)
    else:
        kmean_ref[...] = jnp.zeros(kmean_ref.shape, F32)

    for p in range(4):
        q = proj(C_QB + p * LANES, LANES) * SCALE
        qr = _rope128(q, c, sa, sb)
        qb_ref[2 * p] = q[:, :HEAD_DIM].astype(BF16)
        qb_ref[2 * p + 1] = q[:, HEAD_DIM:].astype(BF16)
        qbr_ref[2 * p] = jnp.concatenate([qr[:, :HEAD_DIM].astype(BF16), zeros64], axis=1)
        qbr_ref[2 * p + 1] = jnp.concatenate([qr[:, HEAD_DIM:].astype(BF16), zeros64], axis=1)

    kc = proj(C_KC, LANES)
    vc = proj(C_VC, LANES)
    ks = _rope128(proj(C_KS, LANES), c, sa, sb)
    vs = proj(C_VS, LANES)
    nsa_ref[:, 0:128] = kc
    nsa_ref[:, 128:256] = vc
    nsa_ref[:, 256:384] = ks
    nsa_ref[:, 384:512] = vs
    kcvc_ref[0] = kc[:, :HEAD_DIM]
    kcvc_ref[1] = kc[:, HEAD_DIM:]
    kcvc_ref[2] = vc[:, :HEAD_DIM]
    kcvc_ref[3] = vc[:, HEAD_DIM:]
    s_onehot = (lane - HEAD_DIM == (row // SEL_BLOCK) % HEAD_DIM).astype(BF16)
    ksb = ks.astype(BF16)
    vsb = vs.astype(BF16)
    ksaug_ref[0] = jnp.concatenate([ksb[:, :HEAD_DIM], s_onehot[:, HEAD_DIM:]], axis=1)
    ksaug_ref[1] = jnp.concatenate([ksb[:, HEAD_DIM:], s_onehot[:, HEAD_DIM:]], axis=1)
    vs_ref[0] = vsb[:, :HEAD_DIM]
    vs_ref[1] = vsb[:, HEAD_DIM:]

    kw = _rope128(proj(C_KW, LANES), c, sa, sb)
    vw = proj(C_VW, LANES)
    win_ref[:, 0:128] = kw
    win_ref[:, 128:256] = vw
    kwb = kw.astype(BF16)
    vwb = vw.astype(BF16)
    kw_ref[0] = kwb[:, :HEAD_DIM]
    kw_ref[1] = kwb[:, HEAD_DIM:]
    vw_ref[0] = vwb[:, :HEAD_DIM]
    vw_ref[1] = vwb[:, HEAD_DIM:]

    gates_ref[...] = jax.nn.sigmoid(proj(C_G, LANES) + bg_ref[...])


def _rope_tables(pos):
    half = ROPE_DIM // 2
    inv = ROPE_THETA ** (-2.0 * jnp.arange(half, dtype=F32) / ROPE_DIM)
    ang = pos.astype(F32)[:, None] * inv[None, :]
    cos, sin = jnp.cos(ang), jnp.sin(ang)
    t = pos.shape[0]
    ones = jnp.ones((t, HEAD_DIM - ROPE_DIM), F32)
    zeros = jnp.zeros((t, HEAD_DIM - ROPE_DIM), F32)
    zh = jnp.zeros((t, half), F32)
    c = jnp.concatenate([cos, cos, ones], axis=1)
    sa = jnp.concatenate([-sin, zh, zeros], axis=1)
    sb = jnp.concatenate([zh, sin, zeros], axis=1)
    return tuple(jnp.concatenate([a, a], axis=1) for a in (c, sa, sb))


def _inproj(x2d, pos, g, w_pad, bg_pad, *, tm, with_kmean):
    t, d = x2d.shape
    nt = t // tm
    c, sa, sb = _rope_tables(pos)
    row_spec = lambda w: pl.BlockSpec((tm, w), lambda i: (i, 0))
    head_spec = lambda n, w: pl.BlockSpec((n, tm, w), lambda i: (0, i, 0))
    full = lambda a: pl.BlockSpec(a.shape, lambda i: (0,) * a.ndim)
    nkm = max(tm // A_BLOCK, 1)
    out_shape = (
        jax.ShapeDtypeStruct((t, 512), F32),
        jax.ShapeDtypeStruct((t, 512), F32),
        jax.ShapeDtypeStruct((t, 256), F32),
        jax.ShapeDtypeStruct((t, LANES), F32),
        jax.ShapeDtypeStruct((A_HEADS, t, LANES), BF16),
        jax.ShapeDtypeStruct((A_KV_HEADS, t, LANES), BF16),
        jax.ShapeDtypeStruct((A_KV_HEADS, t, HEAD_DIM), BF16),
        jax.ShapeDtypeStruct((B_HEADS, t, HEAD_DIM), BF16),
        jax.ShapeDtypeStruct((B_HEADS, t, LANES), BF16),
        jax.ShapeDtypeStruct((B_KV_HEADS, t, LANES), BF16),
        jax.ShapeDtypeStruct((B_KV_HEADS, t, HEAD_DIM), BF16),
        jax.ShapeDtypeStruct((B_KV_HEADS, t, HEAD_DIM), BF16),
        jax.ShapeDtypeStruct((B_KV_HEADS, t, HEAD_DIM), BF16),
        jax.ShapeDtypeStruct((4, t, HEAD_DIM), F32),
        jax.ShapeDtypeStruct((nt, nkm, 256), F32),
    )
    out_specs = (
        row_spec(512), row_spec(512), row_spec(256), row_spec(LANES),
        head_spec(A_HEADS, LANES), head_spec(A_KV_HEADS, LANES), head_spec(A_KV_HEADS, HEAD_DIM),
        head_spec(B_HEADS, HEAD_DIM), head_spec(B_HEADS, LANES), head_spec(B_KV_HEADS, LANES),
        head_spec(B_KV_HEADS, HEAD_DIM), head_spec(B_KV_HEADS, HEAD_DIM), head_spec(B_KV_HEADS, HEAD_DIM),
        head_spec(4, HEAD_DIM),
        pl.BlockSpec((1, nkm, 256), lambda i: (i, 0, 0)),
    )
    return pl.pallas_call(
        functools.partial(_inproj_kernel, tm=tm, with_kmean=with_kmean),
        out_shape=out_shape,
        grid=(nt,),
        in_specs=[row_spec(d), full(g), full(w_pad), full(bg_pad),
                  row_spec(LANES), row_spec(LANES), row_spec(LANES)],
        out_specs=out_specs,
        compiler_params=_cparams("parallel"),
        name="inproj",
    )(x2d, g, w_pad, bg_pad, c, sa, sb)


def _outproj_kernel(oa_ref, ob_ref, x_ref, wa_ref, wb_ref, gpost_ref, gpre_ref, x1_ref, h2_ref):
    mix = _dot(oa_ref[...].astype(BF16), wa_ref[...]) + _dot(ob_ref[...].astype(BF16), wb_ref[...])
    x1 = x_ref[...] + _rms(mix, gpost_ref[...])
    x1_ref[...] = x1
    h2_ref[...] = _rms(x1, gpre_ref[...]).astype(BF16)


def _outproj(oa, ob, x2d, w_out_bf, g_post, g_pre, *, tm):
    t, d = x2d.shape
    half = oa.shape[1]
    row = lambda w: pl.BlockSpec((tm, w), lambda i: (i, 0))
    full = lambda a: pl.BlockSpec(a.shape, lambda i: (0,) * a.ndim)
    wa, wb = w_out_bf[:half], w_out_bf[half:]
    return pl.pallas_call(
        _outproj_kernel,
        out_shape=(jax.ShapeDtypeStruct((t, d), F32), jax.ShapeDtypeStruct((t, d), BF16)),
        grid=(t // tm,),
        in_specs=[row(half), row(half), row(d), full(wa), full(wb), full(g_post), full(g_pre)],
        out_specs=(row(d), row(d)),
        compiler_params=_cparams("parallel"),
        name="outproj",
    )(oa, ob, x2d, wa, wb, g_post, g_pre)


HALO = 8


def _ffn_seq_kernel(h_ref, halo_ref, x1_ref, wg_ref, wv_ref, cg_ref, cv_ref, bg_ref, bv_ref,
                    wd_ref, gpost_ref, y_ref, tailg_ref, tailv_ref, acc_ref, ug_ref, uv_ref, *, tm):
    i, c = pl.program_id(0), pl.program_id(1)

    @pl.when(c == 0)
    def _():
        acc_ref[...] = jnp.zeros(acc_ref.shape, F32)

    keep = (i > 0).astype(F32)

    def conv(w_ref, u_ref, cw_ref, cb_ref, tail_ref):
        u_ref[0:HALO] = _dot(halo_ref[...], w_ref[...]) * keep
        u_ref[HALO:HALO + tm] = _dot(h_ref[...], w_ref[...])
        tail_ref[0] = u_ref[tm:tm + HALO]
        cw = cw_ref[...]
        return (u_ref[pl.ds(HALO - 2, tm), :] * cw[0:1] + u_ref[pl.ds(HALO - 1, tm), :] * cw[1:2]
                + u_ref[pl.ds(HALO, tm), :] * cw[2:3] + cb_ref[...])

    gate = conv(wg_ref, ug_ref, cg_ref, bg_ref, tailg_ref)
    val = conv(wv_ref, uv_ref, cv_ref, bv_ref, tailv_ref)
    acc_ref[...] += _dot((_gelu(gate) * val).astype(BF16), wd_ref[...])

    @pl.when(c == pl.num_programs(1) - 1)
    def _():
        y_ref[...] = x1_ref[...] + _rms(acc_ref[...], gpost_ref[...])


def _ffn_step_kernel(h_ref, p0g_ref, p1g_ref, p0v_ref, p1v_ref, x1_ref, wg_ref, wv_ref, cg_ref, cv_ref,
                     bg_ref, bv_ref, wd_ref, gpost_ref, y_ref, upg_ref, upv_ref, acc_ref):
    c = pl.program_id(0)

    @pl.when(c == 0)
    def _():
        acc_ref[...] = jnp.zeros(acc_ref.shape, F32)

    def conv(w_ref, p0_ref, p1_ref, cw_ref, cb_ref, up_ref):
        u = _dot(h_ref[...], w_ref[...])
        up_ref[...] = u
        cw = cw_ref[...]
        return p0_ref[...] * cw[0:1] + p1_ref[...] * cw[1:2] + u * cw[2:3] + cb_ref[...]

    gate = conv(wg_ref, p0g_ref, p1g_ref, cg_ref, bg_ref, upg_ref)
    val = conv(wv_ref, p0v_ref, p1v_ref, cv_ref, bv_ref, upv_ref)
    acc_ref[...] += _dot((_gelu(gate) * val).astype(BF16), wd_ref[...])

    @pl.when(c == pl.num_programs(0) - 1)
    def _():
        y_ref[...] = x1_ref[...] + _rms(acc_ref[...], gpost_ref[...])


def _ffn_seq(h2, x1, w_up_bf, w_conv, b_conv2d, w_down_bf, g_post, *, tm, ck):
    t, d = x1.shape
    dff = w_down_bf.shape[0]
    nff = dff // ck
    nt = t // tm
    hb = tm // HALO
    gcol = lambda r: pl.BlockSpec((r, ck), lambda i, c: (0, c))
    vcol = lambda r: pl.BlockSpec((r, ck), lambda i, c: (0, nff + c))
    row = pl.BlockSpec((tm, d), lambda i, c: (i, 0))
    tail = pl.BlockSpec((1, HALO, ck), lambda i, c: (i, 0, c))
    y, tg, tv = pl.pallas_call(
        functools.partial(_ffn_seq_kernel, tm=tm),
        out_shape=(jax.ShapeDtypeStruct((t, d), F32),
                   jax.ShapeDtypeStruct((nt, HALO, dff), F32), jax.ShapeDtypeStruct((nt, HALO, dff), F32)),
        grid=(nt, nff),
        in_specs=[row, pl.BlockSpec((HALO, d), lambda i, c: (jnp.maximum(i * hb - 1, 0), 0)), row,
                  gcol(d), vcol(d), gcol(CONV_W), vcol(CONV_W), gcol(1), vcol(1),
                  pl.BlockSpec((ck, d), lambda i, c: (c, 0)),
                  pl.BlockSpec(g_post.shape, lambda i, c: (0, 0))],
        out_specs=(row, tail, tail),
        scratch_shapes=[pltpu.VMEM((tm, d), F32), pltpu.VMEM((tm + HALO, ck), F32),
                        pltpu.VMEM((tm + HALO, ck), F32)],
        compiler_params=_cparams("parallel", "arbitrary"),
        name="ffn_seq",
    )(h2, h2, x1, w_up_bf, w_up_bf, w_conv, w_conv, b_conv2d, b_conv2d, w_down_bf, g_post)
    state = jnp.concatenate([tg[-1, HALO - 2:], tv[-1, HALO - 2:]], axis=1)
    return y, state


def _ffn_step(h2, x1, prev, w_up_bf, w_conv, b_conv2d, w_down_bf, g_post, *, ck):
    t, d = x1.shape
    dff = w_down_bf.shape[0]
    nff = dff // ck
    p0, p1 = prev[:, 0], prev[:, 1]
    gcol = lambda r: pl.BlockSpec((r, ck), lambda c: (0, c))
    vcol = lambda r: pl.BlockSpec((r, ck), lambda c: (0, nff + c))
    row = pl.BlockSpec((t, d), lambda c: (0, 0))
    y, ug, uv = pl.pallas_call(
        _ffn_step_kernel,
        out_shape=(jax.ShapeDtypeStruct((t, d), F32),
                   jax.ShapeDtypeStruct((t, dff), F32), jax.ShapeDtypeStruct((t, dff), F32)),
        grid=(nff,),
        in_specs=[row, gcol(t), gcol(t), vcol(t), vcol(t), row,
                  gcol(d), vcol(d), gcol(CONV_W), vcol(CONV_W), gcol(1), vcol(1),
                  pl.BlockSpec((ck, d), lambda c: (c, 0)),
                  pl.BlockSpec(g_post.shape, lambda c: (0, 0))],
        out_specs=(row, pl.BlockSpec((t, ck), lambda c: (0, c)), pl.BlockSpec((t, ck), lambda c: (0, c))),
        scratch_shapes=[pltpu.VMEM((t, d), F32)],
        compiler_params=_cparams("arbitrary"),
        name="ffn_step",
    )(h2, p0, p1, p0, p1, x1, w_up_bf, w_up_bf, w_conv, w_conv, b_conv2d, b_conv2d, w_down_bf, g_post)
    state = jnp.stack([p1, jnp.concatenate([ug, uv], axis=1)], axis=1)
    return y, state


ATT_KT = 512


def _softmax_init(m_sc, l_sc, acc_sc):
    m_sc[...] = jnp.full(m_sc.shape, NEG_INF, F32)
    l_sc[...] = jnp.zeros(l_sc.shape, F32)
    acc_sc[...] = jnp.zeros(acc_sc.shape, F32)


def _two_pass_attention(q3_ref, k_ref, v_ref, n_full, tiles_per_win, qpos, m_sc, l_sc, acc_sc, s_sc):
    nwin, rows, _ = q3_ref.shape
    kt = s_sc.shape[2]
    groups = [slice(j * LANES, (j + 1) * LANES) for j in range(kt // LANES)]

    def scores(t):
        off = pl.multiple_of(t * kt, kt)
        w = t // tiles_per_win if nwin > 1 else 0
        return _dot_t(q3_ref[w], k_ref[0, pl.ds(off, kt), :])

    def causal(s):
        kpos = n_full * kt + lax.broadcasted_iota(jnp.int32, (rows, kt), 1)
        return jnp.where(kpos <= qpos, s, NEG_INF)

    def take_max(s):
        mx = s[:, groups[0]]
        for gs in groups[1:]:
            mx = jnp.maximum(mx, s[:, gs])
        m_sc[...] = jnp.maximum(m_sc[...], mx)

    def consume(t, s):
        m = m_sc[...]
        ps = [jnp.exp(s[:, gs] - m) for gs in groups]
        tot = ps[0]
        for p in ps[1:]:
            tot = tot + p
        l_sc[...] += tot
        off = pl.multiple_of(t * kt, kt)
        acc_sc[...] += _dot(jnp.concatenate(ps, axis=1).astype(BF16), v_ref[0, pl.ds(off, kt), :])

    _softmax_init(m_sc, l_sc, acc_sc)

    def max_body(t, carry):
        take_max(scores(t))
        return carry

    lax.fori_loop(0, n_full, max_body, 0)
    take_max(causal(scores(n_full)))
    m_sc[...] = jnp.broadcast_to(jnp.max(m_sc[...], axis=1, keepdims=True), m_sc.shape)

    s_sc[0] = scores(0)

    def sum_body(t, carry):
        s = s_sc[t % 2]
        s_sc[(t + 1) % 2] = scores(t + 1)
        consume(t, s)
        return carry

    lax.fori_loop(0, n_full, sum_body, 0)
    consume(n_full, causal(s_sc[n_full % 2]))
    return acc_sc[...] / jnp.sum(l_sc[...], axis=1, keepdims=True)


def _top_select(v, lane, forced, rounds):
    sel = forced
    for _ in range(rounds):
        mx = jnp.max(v, axis=1, keepdims=True)
        idx = jnp.min(jnp.where(v == mx, lane, 1 << 20), axis=1, keepdims=True)
        pick = (lane == idx) & (mx > NEG_INF)
        sel = sel | pick
        v = jnp.where(pick, NEG_INF, v)
    return sel


def _masked_softmax(s, mask):
    s = jnp.where(mask, s, NEG_INF)
    m = jnp.max(s, axis=1, keepdims=True)
    m = jnp.where(m > NEG_INF, m, 0.0)
    e = jnp.where(mask, jnp.exp(s - m), 0.0)
    d = jnp.sum(e, axis=1, keepdims=True)
    return e / jnp.where(d > 0, d, 1.0)


def _moba_prompt_kernel(q_ref, km_ref, k_ref, v_ref, o_ref, m_sc, l_sc, acc_sc, q3_sc, s_sc):
    cur = pl.program_id(1)
    rows = 2 * A_BLOCK
    q = q_ref[...].reshape(rows, LANES)
    gate = _dot(q, km_ref[0])
    lane = lax.broadcasted_iota(jnp.int32, (rows, LANES), 1)
    past = (lane >= HEAD_DIM) & (lane < HEAD_DIM + cur)
    sel = _top_select(jnp.where(past, gate, NEG_INF), lane, lane == HEAD_DIM + cur, A_TOPK)
    bias = jnp.where(sel | (lane < HEAD_DIM), 0.0, MASK_BIAS)
    q3_sc[0] = (q.astype(F32) * SCALE + bias).astype(BF16)

    qpos = cur * A_BLOCK + lax.broadcasted_iota(jnp.int32, (rows, 1), 0) % A_BLOCK
    o = _two_pass_attention(q3_sc, k_ref, v_ref, cur // (ATT_KT // A_BLOCK), None, qpos,
                            m_sc, l_sc, acc_sc, s_sc)
    o_ref[:, 0:HEAD_DIM] = o[:A_BLOCK]
    o_ref[:, HEAD_DIM:LANES] = o[A_BLOCK:]


def _gate_matrix(kmean):
    nblk = kmean.shape[0]
    km = kmean.reshape(nblk, A_KV_HEADS, HEAD_DIM).transpose(1, 2, 0)
    km = jnp.pad(km, ((0, 0), (0, LANES - HEAD_DIM), (HEAD_DIM, LANES - HEAD_DIM - nblk)))
    return km.astype(BF16)


def _moba_prompt(qa, km, kaug, va):
    t = qa.shape[1]
    nt = t // A_BLOCK
    assert nt <= HEAD_DIM, "key-block one-hot occupies 64 lanes"
    assert t % ATT_KT == 0 and ATT_KT % A_BLOCK == 0
    rows = 2 * A_BLOCK
    return pl.pallas_call(
        _moba_prompt_kernel,
        out_shape=jax.ShapeDtypeStruct((t, A_HEADS * HEAD_DIM), F32),
        grid=(A_KV_HEADS, nt),
        in_specs=[pl.BlockSpec((2, A_BLOCK, LANES), lambda g, i: (g, i, 0)),
                  pl.BlockSpec((1, LANES, LANES), lambda g, i: (g, 0, 0)),
                  pl.BlockSpec((1, t, LANES), lambda g, i: (g, 0, 0)),
                  pl.BlockSpec((1, t, HEAD_DIM), lambda g, i: (g, 0, 0))],
        out_specs=pl.BlockSpec((A_BLOCK, LANES), lambda g, i: (i, g)),
        scratch_shapes=[pltpu.VMEM((rows, LANES), F32), pltpu.VMEM((rows, LANES), F32),
                        pltpu.VMEM((rows, HEAD_DIM), F32), pltpu.VMEM((1, rows, LANES), BF16),
                        pltpu.VMEM((2, rows, ATT_KT), F32)],
        compiler_params=_cparams("parallel", "parallel"),
        name="moba_prompt",
    )(qa, km, kaug, va)


def _compress_kernel(x_ref, pelo_ref, pehi_ref, wlo_ref, whi_ref, b1_ref, w2_ref, o_ref, hi_sc, *, nch):
    x = x_ref[0]
    lo = _dot((x + pelo_ref[0]).astype(BF16), wlo_ref[0])
    hi_sc[0:nch] = _dot((x + pehi_ref[0]).astype(BF16), whi_ref[0])
    hi_sc[nch:nch + 8] = jnp.zeros((8, CMP_HIDDEN), F32)
    hid = _gelu(lo + hi_sc[pl.ds(1, nch), :] + b1_ref[0])
    o_ref[0] = _dot(hid.astype(BF16), w2_ref[0]).astype(o_ref.dtype)


def _compress_weights(cmp_pe, cmp_w1, cmp_b1, cmp_w2):
    flat = CMP_STRIDE * HEAD_DIM
    pelo = cmp_pe[:, :CMP_STRIDE].reshape(2, 1, flat)
    pehi = cmp_pe[:, CMP_STRIDE:].reshape(2, 1, flat)
    wlo = cmp_w1[:, :CMP_STRIDE].reshape(2, flat, CMP_HIDDEN).astype(BF16)
    whi = cmp_w1[:, CMP_STRIDE:].reshape(2, flat, CMP_HIDDEN).astype(BF16)
    return pelo, pehi, wlo, whi, cmp_b1.reshape(2, 1, CMP_HIDDEN), cmp_w2.astype(BF16)


def _compress(xch, cw):
    _, nch, flat = xch.shape
    pelo, pehi, wlo, whi, b1, w2 = cw
    kind = lambda shape: pl.BlockSpec((1,) + shape, lambda j: (j // B_KV_HEADS, 0, 0))
    return pl.pallas_call(
        functools.partial(_compress_kernel, nch=nch),
        out_shape=jax.ShapeDtypeStruct((4, nch, HEAD_DIM), BF16),
        grid=(4,),
        in_specs=[pl.BlockSpec((1, nch, flat), lambda j: (j, 0, 0)),
                  kind((1, flat)), kind((1, flat)), kind((flat, CMP_HIDDEN)), kind((flat, CMP_HIDDEN)),
                  kind((1, CMP_HIDDEN)), kind((CMP_HIDDEN, HEAD_DIM))],
        out_specs=pl.BlockSpec((1, nch, HEAD_DIM), lambda j: (j, 0, 0)),
        scratch_shapes=[pltpu.VMEM((nch + 8, CMP_HIDDEN), F32)],
        compiler_params=_cparams("parallel"),
        name="compress",
    )(xch, pelo, pehi, wlo, whi, b1, w2)


def _importance_matrix(ncmp, nsel):
    ratio = SEL_BLOCK // CMP_STRIDE
    n = np.arange(ncmp)[:, None]
    j = np.arange(nsel)[None, :]
    own = (n // ratio == j)
    last = (n % ratio == ratio - 1)
    m = np.where(own & ~last, 1.0, 0.0) + np.where(last & (own | (n // ratio == j - 1)), 0.5, 0.0)
    return jnp.asarray(m, BF16)


NSA_TQ = 128
WIN_BLOCKS = WINDOW // NSA_TQ + 1


def _nsa_prompt_kernel(*refs, nsel, ncmp):
    (qb_ref, qbr_ref, kc_ref, vc_ref, imp_ref, ks_ref, vs_ref) = refs[:7]
    kw_refs = refs[7:7 + WIN_BLOCKS]
    vw_refs = refs[7 + WIN_BLOCKS:7 + 2 * WIN_BLOCKS]
    g_ref, o_ref, m_sc, l_sc, acc_sc, q3_sc, s_sc = refs[7 + 2 * WIN_BLOCKS:]
    i = pl.program_id(1)
    rows = B_GROUP * NSA_TQ
    q0 = i * NSA_TQ
    q = qb_ref[...].reshape(rows, HEAD_DIM)
    qr = qbr_ref[...].reshape(rows, LANES)

    s = _dot_t(q, kc_ref[0])
    qpos_c = q0 + lax.broadcasted_iota(jnp.int32, (rows, ncmp), 0) % NSA_TQ
    n_c = lax.broadcasted_iota(jnp.int32, (rows, ncmp), 1)
    p = _masked_softmax(s, n_c * CMP_STRIDE + (CMP_LEN - 1) <= qpos_c)
    o_cmp = _dot(p.astype(BF16), vc_ref[0])

    psum = p[0:NSA_TQ]
    for j in range(1, B_GROUP):
        psum = psum + p[j * NSA_TQ:(j + 1) * NSA_TQ]
    p_hi = psum.astype(BF16)
    p_lo = (psum - p_hi.astype(F32)).astype(BF16)
    imp = _dot(p_hi, imp_ref[...]) + _dot(p_lo, imp_ref[...])
    blk = lax.broadcasted_iota(jnp.int32, (NSA_TQ, nsel), 1)
    cur = (q0 + lax.broadcasted_iota(jnp.int32, (NSA_TQ, nsel), 0)) // SEL_BLOCK
    forced = (blk == 0) | (blk == cur) | (blk == cur - 1)
    cand = jnp.where(blk > cur, NEG_INF, jnp.where(forced, jnp.inf, imp))
    sel = _top_select(cand, blk, jnp.zeros((NSA_TQ, nsel), jnp.bool_), min(SEL_TOPN, nsel))
    selbias = jnp.where(sel, 0.0, MASK_BIAS)

    kt_last = (q0 + NSA_TQ - 1) // ATT_KT
    tiles_per_win = HEAD_DIM * SEL_BLOCK // ATT_KT
    qr32 = qr.astype(F32)
    for w in range(-(-nsel // HEAD_DIM)):
        nb = min(HEAD_DIM, nsel - w * HEAD_DIM)
        pieces = [jnp.zeros((NSA_TQ, HEAD_DIM), F32), selbias[:, w * HEAD_DIM:w * HEAD_DIM + nb]]
        if nb < HEAD_DIM:
            pieces.append(jnp.zeros((NSA_TQ, HEAD_DIM - nb), F32))
        bias_w = jnp.concatenate(pieces, axis=1)
        q3_sc[w] = (qr32 + jnp.concatenate([bias_w] * B_GROUP, axis=0)).astype(BF16)

    qpos_col = q0 + lax.broadcasted_iota(jnp.int32, (rows, 1), 0) % NSA_TQ
    o_sel = _two_pass_attention(q3_sc, ks_ref, vs_ref, kt_last, tiles_per_win, qpos_col,
                                m_sc, l_sc, acc_sc, s_sc)

    kband = jnp.concatenate([r[0] for r in kw_refs], axis=0)
    vband = jnp.concatenate([r[0] for r in vw_refs], axis=0)
    nband = WIN_BLOCKS * NSA_TQ
    s = _dot_t(qr[:, :HEAD_DIM], kband)
    qpos = q0 + lax.broadcasted_iota(jnp.int32, (rows, nband), 0) % NSA_TQ
    kpos = q0 - WINDOW + lax.broadcasted_iota(jnp.int32, (rows, nband), 1)
    p = _masked_softmax(s, (kpos <= qpos) & (kpos >= qpos - WINDOW) & (kpos >= 0))
    o_win = _dot(p.astype(BF16), vband)

    gts = g_ref[0]
    for j in range(B_GROUP):
        sl = slice(j * NSA_TQ, (j + 1) * NSA_TQ)
        o = (gts[:, 3 * j:3 * j + 1] * o_cmp[sl] + gts[:, 3 * j + 1:3 * j + 2] * o_sel[sl]
             + gts[:, 3 * j + 2:3 * j + 3] * o_win[sl])
        o_ref[:, j * HEAD_DIM:(j + 1) * HEAD_DIM] = o


def _nsa_prompt(qb, qbr, kvcmp, ksaug, vs, kw, vw, gates):
    t = qb.shape[1]
    nt = t // NSA_TQ
    nsel = t // SEL_BLOCK
    ncmp = kvcmp.shape[1]
    rows = B_GROUP * NSA_TQ
    impm = _importance_matrix(ncmp, nsel)
    res = lambda w: pl.BlockSpec((1, t, w), lambda g, i: (g, 0, 0))
    band = [pl.BlockSpec((1, NSA_TQ, HEAD_DIM),
                         functools.partial(lambda g, i, j: (g, jnp.maximum(i - (WIN_BLOCKS - 1) + j, 0), 0), j=j))
            for j in range(WIN_BLOCKS)]
    return pl.pallas_call(
        functools.partial(_nsa_prompt_kernel, nsel=nsel, ncmp=ncmp),
        out_shape=jax.ShapeDtypeStruct((t, B_HEADS * HEAD_DIM), F32),
        grid=(B_KV_HEADS, nt),
        in_specs=[pl.BlockSpec((B_GROUP, NSA_TQ, HEAD_DIM), lambda g, i: (g, i, 0)),
                  pl.BlockSpec((B_GROUP, NSA_TQ, LANES), lambda g, i: (g, i, 0)),
                  pl.BlockSpec((1, ncmp, HEAD_DIM), lambda g, i: (g, 0, 0)),
                  pl.BlockSpec((1, ncmp, HEAD_DIM), lambda g, i: (B_KV_HEADS + g, 0, 0)),
                  pl.BlockSpec(impm.shape, lambda g, i: (0, 0)),
                  res(LANES), res(HEAD_DIM)] + band + band +
                 [pl.BlockSpec((1, NSA_TQ, 3 * B_GROUP), lambda g, i: (g, i, 0))],
        out_specs=pl.BlockSpec((NSA_TQ, B_GROUP * HEAD_DIM), lambda g, i: (i, g)),
        scratch_shapes=[pltpu.VMEM((rows, LANES), F32), pltpu.VMEM((rows, LANES), F32),
                        pltpu.VMEM((rows, HEAD_DIM), F32),
                        pltpu.VMEM((-(-nsel // HEAD_DIM), rows, LANES), BF16),
                        pltpu.VMEM((2, rows, ATT_KT), F32)],
        compiler_params=_cparams("parallel", "parallel"),
        name="nsa_prompt",
    )(qb, qbr, kvcmp, kvcmp, impm, ksaug, vs, *([kw] * WIN_BLOCKS), *([vw] * WIN_BLOCKS), gates)


def _top_indices(v, lane, rounds):
    out = jnp.zeros((v.shape[0], LANES), jnp.int32)
    slot = lax.broadcasted_iota(jnp.int32, out.shape, 1)
    for r in range(rounds):
        mx = jnp.max(v, axis=1, keepdims=True)
        idx = jnp.min(jnp.where(v == mx, lane, 1 << 20), axis=1, keepdims=True)
        out = jnp.where(slot == r, idx, out)
        v = jnp.where(lane == idx, NEG_INF, v)
    return out


def _bf16_round(x):
    return x.astype(BF16).astype(F32)


def _moba_kmean_kernel(pt_ref, *refs, pg):
    pages, o_ref = refs[:pg], refs[pg]
    step = pl.program_id(1)
    per_blk = A_BLOCK // pages[0].shape[2]
    nb = pg // per_blk

    @pl.when(step == 0)
    def _():
        o_ref[...] = jnp.zeros(o_ref.shape, F32)

    lane = lax.broadcasted_iota(jnp.int32, o_ref.shape[1:], 1)
    out = o_ref[0]
    for j in range(nb):
        tot = pages[j * per_blk][0]
        for r in pages[j * per_blk + 1:(j + 1) * per_blk]:
            tot = tot + r[0]
        mean = jnp.sum(tot, axis=1, keepdims=True) * (1.0 / A_BLOCK)
        out = jnp.where(lane == step * nb + j, mean, out)
    o_ref[0] = out


def _moba_kmean(cache_t, page_table, *, pg):
    db, npg = page_table.shape
    page = cache_t.shape[2]
    kw = A_KV_HEADS * HEAD_DIM
    assert npg * page // A_BLOCK <= LANES
    specs = [pl.BlockSpec((1, kw, page), functools.partial(lambda b, s, pt, j: (pt[b, s * pg + j], 0, 0), j=j))
             for j in range(pg)]
    return pl.pallas_call(
        functools.partial(_moba_kmean_kernel, pg=pg),
        out_shape=jax.ShapeDtypeStruct((db, kw, LANES), F32),
        grid_spec=pltpu.PrefetchScalarGridSpec(
            num_scalar_prefetch=1, grid=(db, npg // pg), in_specs=specs,
            out_specs=pl.BlockSpec((1, kw, LANES), lambda b, s, pt: (b, 0, 0))),
        compiler_params=_cparams("parallel", "arbitrary"),
        name="moba_kmean",
    )(page_table, *([cache_t] * pg))


def _moba_gate_kernel(q_ref, km_ref, idx_ref, *, nblk):
    q = q_ref[0][:, :HEAD_DIM].astype(BF16)
    head = lax.broadcasted_iota(jnp.int32, (A_HEADS, LANES), 0)
    lane = lax.broadcasted_iota(jnp.int32, (A_HEADS, LANES), 1)
    gate = jnp.zeros((A_HEADS, LANES), F32)
    for g in range(A_KV_HEADS):
        kmg = km_ref[0][g * HEAD_DIM:(g + 1) * HEAD_DIM, :].astype(BF16)
        gate = jnp.where(head // (A_HEADS // A_KV_HEADS) == g, _dot(q, kmg), gate)
    idx_ref[0] = _top_indices(jnp.where(lane < nblk, gate, NEG_INF), lane, A_TOPK)


def _moba_gate(qa_rows, kmean_t, *, nblk):
    db, kw, _ = kmean_t.shape
    return pl.pallas_call(
        functools.partial(_moba_gate_kernel, nblk=nblk),
        out_shape=jax.ShapeDtypeStruct((db, A_HEADS, LANES), jnp.int32),
        grid=(db,),
        in_specs=[pl.BlockSpec((1, A_HEADS, LANES), lambda b: (b, 0, 0)),
                  pl.BlockSpec((1, kw, LANES), lambda b: (b, 0, 0))],
        out_specs=pl.BlockSpec((1, A_HEADS, LANES), lambda b: (b, 0, 0)),
        compiler_params=_cparams("parallel"),
        name="moba_gate",
    )(qa_rows, kmean_t)


def _row_softmax_step(s, v_t, m_sc, l_sc, acc_sc):
    m_old = m_sc[...]
    m_new = jnp.maximum(m_old, jnp.max(s, axis=1, keepdims=True))
    p = jnp.exp(s - m_new)
    alpha = jnp.exp(m_old - m_new)
    l_sc[...] = alpha * l_sc[...] + jnp.sum(p, axis=1, keepdims=True)
    acc_sc[...] = alpha * acc_sc[...] + _dot_t(p.astype(BF16), v_t.astype(BF16))
    m_sc[...] = m_new


def _row_softmax_last(s_own, v_own, m_sc, l_sc, acc_sc):
    m_old = m_sc[...]
    m_new = jnp.maximum(m_old, s_own)
    p = jnp.exp(s_own - m_new)
    alpha = jnp.exp(m_old - m_new)
    l = alpha * l_sc[...] + p
    return (alpha * acc_sc[...] + _bf16_round(p) * _bf16_round(v_own)) / l


def _moba_sample_kernel(pt_ref, ix_ref, q_ref, *refs, npage):
    k_refs, v_refs = refs[:npage], refs[npage:2 * npage]
    kn_ref, vn_ref, o_ref, m_sc, l_sc, acc_sc = refs[2 * npage:]
    h, s_id = pl.program_id(1), pl.program_id(2)
    q = q_ref[0][:, :HEAD_DIM] * SCALE

    @pl.when(s_id == 0)
    def _():
        _softmax_init(m_sc, l_sc, acc_sc)

    for k_ref, v_ref in zip(k_refs, v_refs):
        s = _dot(q.astype(BF16), k_ref[0].astype(BF16))
        _row_softmax_step(s, v_ref[0], m_sc, l_sc, acc_sc)

    @pl.when(s_id == pl.num_programs(2) - 1)
    def _():
        s_own = jnp.sum(q * _bf16_round(kn_ref[0]), axis=1, keepdims=True)
        acc_sc[...] = _row_softmax_last(s_own, vn_ref[0], m_sc, l_sc, acc_sc)
        o_ref[0, pl.ds(h, 1), :] = acc_sc[pl.ds(h, 1), :]


def _moba_sample(cache_t, page_table, idx, qa_rows, k_new, v_new):
    db = page_table.shape[0]
    page = cache_t.shape[2]
    npage = A_BLOCK // page
    rep = A_HEADS // A_KV_HEADS

    def kv_spec(j, row_blk0):
        def imap(b, h, s, pt, ix):
            blk = ix[(b * A_HEADS + h) * A_TOPK + s]
            return (pt[b, blk * npage + j], row_blk0 + h // rep, 0)
        return pl.BlockSpec((1, HEAD_DIM, page), imap)

    per_seq = lambda w: pl.BlockSpec((1, A_HEADS, w), lambda b, h, s, pt, ix: (b, 0, 0))
    return pl.pallas_call(
        functools.partial(_moba_sample_kernel, npage=npage),
        out_shape=jax.ShapeDtypeStruct((db, A_HEADS, HEAD_DIM), F32),
        grid_spec=pltpu.PrefetchScalarGridSpec(
            num_scalar_prefetch=2, grid=(db, A_HEADS, A_TOPK),
            in_specs=[per_seq(LANES)]
                     + [kv_spec(j, 0) for j in range(npage)] + [kv_spec(j, A_KV_HEADS) for j in range(npage)]
                     + [per_seq(HEAD_DIM), per_seq(HEAD_DIM)],
            out_specs=per_seq(HEAD_DIM),
            scratch_shapes=[pltpu.VMEM((A_HEADS, 1), F32), pltpu.VMEM((A_HEADS, 1), F32),
                            pltpu.VMEM((A_HEADS, HEAD_DIM), F32)]),
        compiler_params=_cparams("parallel", "arbitrary", "arbitrary"),
        name="moba_sample",
    )(page_table, idx, qa_rows, *([cache_t] * (2 * npage)), k_new, v_new)


def _nsa_flatten_kernel(pt_ref, *refs, pg, nch):
    pages = refs[:pg]
    pelo_ref, pehi_ref, wlo_ref, whi_ref, b1_ref, w2_ref, o_ref, x_sc, hi_sc, rows_sc = refs[pg:]
    step = pl.program_id(1)
    page = pages[0].shape[2]
    nc = pg * page // CMP_STRIDE
    base = pl.multiple_of(step * nc, nc)
    lane = lax.broadcasted_iota(jnp.int32, (nc, LANES), 1)
    lo_half = lane < HEAD_DIM
    for j, r in enumerate(pages):
        for pr in range(2):
            rows_sc[pr, j * page:(j + 1) * page, :] = r[0, pr * LANES:(pr + 1) * LANES, :].T
    for u in range(CMP_STRIDE // 2):
        for pr in range(2):
            ap = rows_sc[pr, pl.ds(2 * u, nc, stride=CMP_STRIDE), :]
            bp = rows_sc[pr, pl.ds(2 * u + 1, nc, stride=CMP_STRIDE), :]
            x_sc[2 * pr, pl.ds(base, nc), u * LANES:(u + 1) * LANES] = jnp.where(
                lo_half, ap, pltpu.roll(bp, HEAD_DIM, 1))
            x_sc[2 * pr + 1, pl.ds(base, nc), u * LANES:(u + 1) * LANES] = jnp.where(
                lo_half, pltpu.roll(ap, HEAD_DIM, 1), bp)

    @pl.when(step == pl.num_programs(1) - 1)
    def _():
        hi_sc[nch:nch + 8] = jnp.zeros((8, CMP_HIDDEN), F32)
        for j in range(4):
            c = j // B_KV_HEADS
            x = x_sc[j]
            lo = _dot((x + pelo_ref[c]).astype(BF16), wlo_ref[c])
            hi_sc[0:nch] = _dot((x + pehi_ref[c]).astype(BF16), whi_ref[c])
            hid = _gelu(lo + hi_sc[pl.ds(1, nch), :] + b1_ref[c])
            o_ref[0, j] = _dot(hid.astype(BF16), w2_ref[c]).astype(o_ref.dtype)


def _nsa_sample_compress(cache_t, page_table, cw, *, pg):
    db, npg = page_table.shape
    page = cache_t.shape[2]
    assert page == LANES
    nch = npg * page // CMP_STRIDE
    flat = CMP_STRIDE * HEAD_DIM
    pelo, pehi, wlo, whi, b1, w2 = cw
    full = lambda a: pl.BlockSpec(a.shape, lambda b, s, pt: (0,) * a.ndim)
    specs = [pl.BlockSpec((1, 2 * LANES, page), functools.partial(lambda b, s, pt, j: (pt[b, s * pg + j], 0, 0), j=j))
             for j in range(pg)]
    return pl.pallas_call(
        functools.partial(_nsa_flatten_kernel, pg=pg, nch=nch),
        out_shape=jax.ShapeDtypeStruct((db, 4, nch, HEAD_DIM), BF16),
        grid_spec=pltpu.PrefetchScalarGridSpec(
            num_scalar_prefetch=1, grid=(db, npg // pg),
            in_specs=specs + [full(a) for a in (pelo, pehi, wlo, whi, b1, w2)],
            out_specs=pl.BlockSpec((1, 4, nch, HEAD_DIM), lambda b, s, pt: (b, 0, 0, 0)),
            scratch_shapes=[pltpu.VMEM((4, nch, flat), F32), pltpu.VMEM((nch + 8, CMP_HIDDEN), F32),
                            pltpu.VMEM((2, pg * page, LANES), F32)]),
        compiler_params=_cparams("parallel", "arbitrary"),
        name="nsa_sample_compress",
    )(page_table, *([cache_t] * pg), pelo, pehi, wlo, whi, b1, w2)


def _rows_by_group(per_group):
    head = lax.broadcasted_iota(jnp.int32, per_group[0].shape, 0)
    out = per_group[0]
    for g in range(1, B_KV_HEADS):
        out = jnp.where(head // B_GROUP == g, per_group[g], out)
    return out


def _nsa_sample_select_kernel(q_ref, qr_ref, kv_ref, imp_ref, win_ref, wnew_ref, g_ref, idx_ref, ocw_ref,
                              *, past, nsel_past, ncmp):
    q = q_ref[0][:, :HEAD_DIM].astype(BF16)
    qr = qr_ref[0][:, :HEAD_DIM]
    gts = g_ref[0]
    glane = lax.broadcasted_iota(jnp.int32, (B_HEADS, LANES), 1)
    ghead = lax.broadcasted_iota(jnp.int32, (B_HEADS, LANES), 0)
    gate = lambda c: jnp.sum(jnp.where(glane == 3 * ghead + c, gts, 0.0), axis=1, keepdims=True)

    s = _rows_by_group([_dot_t(q, kv_ref[0, g]) for g in range(B_KV_HEADS)])
    n_c = lax.broadcasted_iota(jnp.int32, (B_HEADS, ncmp), 1)
    p = _masked_softmax(s, n_c * CMP_STRIDE + (CMP_LEN - 1) <= past)
    o_cmp = _rows_by_group([_dot(p.astype(BF16), kv_ref[0, B_KV_HEADS + g]) for g in range(B_KV_HEADS)])

    head = lax.broadcasted_iota(jnp.int32, (B_HEADS, ncmp), 0)
    psum = jnp.zeros((B_HEADS, ncmp), F32)
    for g in range(B_KV_HEADS):
        tot = jnp.sum(jnp.where(head // B_GROUP == g, p, 0.0), axis=0, keepdims=True)
        psum = jnp.where(head == g, tot, psum)
    p_hi = psum.astype(BF16)
    p_lo = (psum - p_hi.astype(F32)).astype(BF16)
    imp = _dot(p_hi, imp_ref[...]) + _dot(p_lo, imp_ref[...])
    blk = lax.broadcasted_iota(jnp.int32, (B_HEADS, nsel_past), 1)
    cand = jnp.where((blk == 0) | (blk == nsel_past - 1), jnp.inf, imp)
    idx_ref[0] = _top_indices(cand, blk, min(SEL_TOPN, nsel_past + 1) - 1)

    wk = [win_ref[0][g * HEAD_DIM:(g + 1) * HEAD_DIM, :] for g in range(B_KV_HEADS)]
    wv = [win_ref[0][(B_KV_HEADS + g) * HEAD_DIM:(B_KV_HEADS + g + 1) * HEAD_DIM, :] for g in range(B_KV_HEADS)]
    nk = [wnew_ref[0][:, g * HEAD_DIM:(g + 1) * HEAD_DIM] for g in range(B_KV_HEADS)]
    nv = [wnew_ref[0][:, (B_KV_HEADS + g) * HEAD_DIM:(B_KV_HEADS + g + 1) * HEAD_DIM] for g in range(B_KV_HEADS)]
    s_w = _rows_by_group([_dot(qr.astype(BF16), k.astype(BF16)) for k in wk])
    s_n = _rows_by_group([jnp.sum(qr * _bf16_round(k), axis=1, keepdims=True) for k in nk])
    m = jnp.maximum(jnp.max(s_w, axis=1, keepdims=True), s_n)
    e_w, e_n = jnp.exp(s_w - m), jnp.exp(s_n - m)
    d = jnp.sum(e_w, axis=1, keepdims=True) + e_n
    o_win = _rows_by_group([_dot_t(e_w.astype(BF16), v.astype(BF16)) + _bf16_round(e_n) * _bf16_round(nvg)
                            for v, nvg in zip(wv, nv)]) / d

    ocw_ref[0] = jnp.concatenate([gate(0) * o_cmp, gate(2) * o_win], axis=1)


def _nsa_sample_select(qb_rows, qbr_rows, kvcmp, win_cache, win_new, gates, *, past):
    db, _, ncmp, _ = kvcmp.shape
    nsel_past = past // SEL_BLOCK
    impm = _importance_matrix(ncmp, nsel_past)
    row3 = lambda a: pl.BlockSpec((1,) + a.shape[1:], lambda b: (b,) + (0,) * (a.ndim - 1))
    kern = functools.partial(_nsa_sample_select_kernel, past=past, nsel_past=nsel_past, ncmp=ncmp)
    return pl.pallas_call(
        kern,
        out_shape=(jax.ShapeDtypeStruct((db, B_HEADS, LANES), jnp.int32),
                   jax.ShapeDtypeStruct((db, B_HEADS, LANES), F32)),
        grid=(db,),
        in_specs=[row3(qb_rows), row3(qbr_rows), row3(kvcmp), pl.BlockSpec(impm.shape, lambda b: (0, 0)),
                  row3(win_cache), row3(win_new), row3(gates)],
        out_specs=(pl.BlockSpec((1, B_HEADS, LANES), lambda b: (b, 0, 0)),
                   pl.BlockSpec((1, B_HEADS, LANES), lambda b: (b, 0, 0))),
        compiler_params=_cparams("parallel"),
        name="nsa_sample_select",
    )(qb_rows, qbr_rows, kvcmp, impm, win_cache, win_new, gates)


def _nsa_sample_attend_kernel(pt_ref, ix_ref, qr_ref, k_ref, v_ref, kn_ref, vn_ref, g_ref, ocw_ref, o_ref,
                              m_sc, l_sc, acc_sc, *, nslot, per_page):
    b, g, s_id = pl.program_id(0), pl.program_id(1), pl.program_id(2)
    qs = qr_ref[0, pl.ds(g * B_GROUP, B_GROUP), :][:, :HEAD_DIM]

    @pl.when(s_id == 0)
    def _():
        _softmax_init(m_sc, l_sc, acc_sc)

    blk = ix_ref[(b * B_KV_HEADS + g) * nslot + s_id]
    page = k_ref.shape[2]
    lane = lax.broadcasted_iota(jnp.int32, (B_GROUP, page), 1)
    s = _dot(qs.astype(BF16), k_ref[0].astype(BF16))
    s = jnp.where(lane // SEL_BLOCK == blk % per_page, s, NEG_INF)
    _row_softmax_step(s, v_ref[0], m_sc, l_sc, acc_sc)

    @pl.when(s_id == nslot - 1)
    def _():
        s_own = jnp.sum(qs * _bf16_round(kn_ref[0, 0]), axis=1, keepdims=True)
        o_sel = _row_softmax_last(s_own, vn_ref[0, 0], m_sc, l_sc, acc_sc)
        gts = g_ref[0]
        hrow = lax.broadcasted_iota(jnp.int32, (B_GROUP, LANES), 0) + g * B_GROUP
        glane = lax.broadcasted_iota(jnp.int32, (B_GROUP, LANES), 1)
        g1 = jnp.sum(jnp.where(glane == 3 * hrow + 1, gts, 0.0), axis=1, keepdims=True)
        ocw = ocw_ref[0, pl.ds(g * B_GROUP, B_GROUP), :]
        o_ref[0, pl.ds(g * B_GROUP, B_GROUP), :] = g1 * o_sel + ocw[:, :HEAD_DIM] + ocw[:, HEAD_DIM:]


def _nsa_sample_attend(cache_t, page_table, idx, qbr_rows, ks_new, vs_new, gates, ocw, *, nslot):
    db = page_table.shape[0]
    page = cache_t.shape[2]
    per_page = page // SEL_BLOCK

    def kv_spec(row_blk0):
        def imap(b, g, s, pt, ix):
            blk = ix[(b * B_KV_HEADS + g) * nslot + s]
            return (pt[b, blk // per_page], row_blk0 + g, 0)
        return pl.BlockSpec((1, HEAD_DIM, page), imap)

    row = lambda a: pl.BlockSpec((1,) + a.shape[1:], lambda b, g, s, pt, ix: (b,) + (0,) * (a.ndim - 1))
    new = pl.BlockSpec((1, 1, 1, HEAD_DIM), lambda b, g, s, pt, ix: (b, g, 0, 0))
    return pl.pallas_call(
        functools.partial(_nsa_sample_attend_kernel, nslot=nslot, per_page=per_page),
        out_shape=jax.ShapeDtypeStruct((db, B_HEADS, HEAD_DIM), F32),
        grid_spec=pltpu.PrefetchScalarGridSpec(
            num_scalar_prefetch=2, grid=(db, B_KV_HEADS, nslot),
            in_specs=[row(qbr_rows), kv_spec(2 * B_KV_HEADS), kv_spec(3 * B_KV_HEADS), new, new,
                      row(gates), row(ocw)],
            out_specs=pl.BlockSpec((1, B_HEADS, HEAD_DIM), lambda b, g, s, pt, ix: (b, 0, 0)),
            scratch_shapes=[pltpu.VMEM((B_GROUP, 1), F32), pltpu.VMEM((B_GROUP, 1), F32),
                            pltpu.VMEM((B_GROUP, HEAD_DIM), F32)]),
        compiler_params=_cparams("parallel", "arbitrary", "arbitrary"),
        name="nsa_sample_attend",
    )(page_table, idx, qbr_rows, cache_t, cache_t, ks_new, vs_new, gates, ocw)


def _prep_weights(l, g_mix_pre, w_in, b_gate, cmp_pe, cmp_w1, cmp_b1, cmp_w2, w_out, g_mix_post,
                  g_ffn_pre, w_up, w_conv, b_conv, w_down, g_ffn_post):
    d_in = w_in.shape[-1]
    n_gate = b_gate.shape[-1]
    return dict(
        g_mix_pre=g_mix_pre[l][None],
        w_in=jnp.pad(w_in[l], ((0, 0), (0, D_IN_PAD - d_in))).astype(BF16),
        b_gate=jnp.pad(b_gate[l], (0, LANES - n_gate))[None],
        cmp=_compress_weights(cmp_pe[l], cmp_w1[l], cmp_b1[l], cmp_w2[l]),
        w_out=w_out[l].astype(BF16), g_mix_post=g_mix_post[l][None], g_ffn_pre=g_ffn_pre[l][None],
        w_up=w_up[l].astype(BF16), w_conv=w_conv[l], b_conv=b_conv[l][None],
        w_down=w_down[l].astype(BF16), g_ffn_post=g_ffn_post[l][None],
    )


def _group_gates(gates):
    t = gates.shape[0]
    return gates[:, :3 * B_HEADS].reshape(t, B_KV_HEADS, 3 * B_GROUP).transpose(1, 0, 2)


FFN_CK = 256


def _prompt_layer(x2d, w):
    t = x2d.shape[0]
    (kva, nsa, win, gates, qa, kaug, va, qb, qbr, ksaug, vs, kw, vw, kcvc, kmean) = _inproj(
        x2d, jnp.arange(t), w["g_mix_pre"], w["w_in"], w["b_gate"], tm=512, with_kmean=True)
    o_a = _moba_prompt(qa, _gate_matrix(kmean.reshape(t // A_BLOCK, A_KV_HEADS * HEAD_DIM)), kaug, va)
    kvcmp = _compress(kcvc.reshape(4, t // CMP_STRIDE, CMP_STRIDE * HEAD_DIM), w["cmp"])
    o_b = _nsa_prompt(qb, qbr, kvcmp, ksaug, vs, kw, vw, _group_gates(gates))
    x1, h2 = _outproj(o_a, o_b, x2d, w["w_out"], w["g_mix_post"], w["g_ffn_pre"], tm=512)
    y, conv_state = _ffn_seq(h2, x1, w["w_up"], w["w_conv"], w["b_conv"], w["w_down"], w["g_ffn_post"],
                             tm=512, ck=FFN_CK)
    keep = min(WINDOW, t)
    return (y, kva.reshape(t, 2, A_KV_HEADS, HEAD_DIM), nsa.reshape(t, 4, B_KV_HEADS, HEAD_DIM),
            win[t - keep:].reshape(keep, 2, B_KV_HEADS, HEAD_DIM), conv_state)


PAGES_PER_STEP = 16


def _sample_layer(x2d, cache_moba, cache_nsa, win_cache, conv_state, page_table, w):
    db = x2d.shape[0]
    n_pool, page = cache_moba.shape[:2]
    npg = page_table.shape[1]
    past = npg * page
    wb = win_cache.shape[1]
    assert wb == WINDOW and past % A_BLOCK == 0 and past // A_BLOCK >= A_TOPK and A_BLOCK % page == 0
    assert page % SEL_BLOCK == 0
    pg = min(PAGES_PER_STEP, npg)
    (kva, nsa, win, gates, qa, _, _, qb, qbr, _, _, _, _, _, _) = _inproj(
        x2d, jnp.full((db,), past, jnp.int32), w["g_mix_pre"], w["w_in"], w["b_gate"], tm=db, with_kmean=False)
    rows = lambda a: a.transpose(1, 0, 2).astype(F32)
    qa_rows, qb_rows, qbr_rows = rows(qa), rows(qb), rows(qbr)

    tpose = lambda c: c.transpose(0, 2, 3, 4, 1).reshape(c.shape[0], -1, c.shape[1])
    moba_t, nsa_t, win_t = tpose(cache_moba), tpose(cache_nsa), tpose(win_cache)

    kmean_t = _moba_kmean(moba_t, page_table, pg=pg)
    idx_a = _moba_gate(qa_rows, kmean_t, nblk=past // A_BLOCK)[:, :, :A_TOPK].reshape(-1)
    rep = A_HEADS // A_KV_HEADS
    per_head = lambda a: jnp.repeat(a.reshape(db, A_KV_HEADS, HEAD_DIM), rep, axis=1)
    o_a = _moba_sample(moba_t, page_table, idx_a, qa_rows, per_head(kva[:, :256]), per_head(kva[:, 256:]))

    kvcmp = _nsa_sample_compress(nsa_t, page_table, w["cmp"], pg=pg)
    gates3 = gates[:, None, :]
    win2d = win_cache.reshape(db, wb, 2 * B_KV_HEADS * HEAD_DIM)
    win_new = win[:, None, :]
    idx_b, ocw = _nsa_sample_select(qb_rows, qbr_rows, kvcmp, win_t, win_new, gates3, past=past)
    nslot = min(SEL_TOPN, past // SEL_BLOCK + 1) - 1
    per_group = lambda a: a.reshape(db, B_KV_HEADS, 1, HEAD_DIM)
    o_b = _nsa_sample_attend(nsa_t, page_table, idx_b[:, :B_KV_HEADS, :nslot].reshape(-1), qbr_rows,
                             per_group(nsa[:, 256:384]), per_group(nsa[:, 384:]), gates3, ocw, nslot=nslot)

    x1, h2 = _outproj(o_a.reshape(db, -1), o_b.reshape(db, -1), x2d, w["w_out"], w["g_mix_post"],
                      w["g_ffn_pre"], tm=db)
    y, conv_new = _ffn_step(h2, x1, conv_state, w["w_up"], w["w_conv"], w["b_conv"], w["w_down"],
                            w["g_ffn_post"], ck=FFN_CK)
    keep = min(WINDOW, wb + 1)
    win_all = jnp.concatenate([win2d, win_new], axis=1)[:, wb + 1 - keep:]
    return (y, kva.reshape(db, 2, A_KV_HEADS, HEAD_DIM), nsa.reshape(db, 4, B_KV_HEADS, HEAD_DIM),
            win_all.reshape(db, keep, 2, B_KV_HEADS, HEAD_DIM), conv_new)


def kernel(x_prompt, x_sample, cache_moba_kv, cache_nsa_kv, cache_nsa_win_kv, state_ffn_conv, page_table,
           g_mix_pre, w_in, b_gate, cmp_pe, cmp_w1, cmp_b1, cmp_w2, w_out, g_mix_post, g_ffn_pre, w_up,
           w_conv, b_conv, w_down, g_ffn_post):
    depth = w_in.shape[0]
    assert depth == 1 and x_prompt.shape[0] == 1
    w = _prep_weights(0, g_mix_pre, w_in, b_gate, cmp_pe, cmp_w1, cmp_b1, cmp_w2, w_out, g_mix_post,
                      g_ffn_pre, w_up, w_conv, b_conv, w_down, g_ffn_post)
    y_p, moba_p, nsa_p, win_p, conv_p = _prompt_layer(x_prompt[0], w)
    assert x_sample.shape[1] == 1
    y_s, moba_s, nsa_s, win_s, conv_s = _sample_layer(
        x_sample[:, 0], cache_moba_kv[0], cache_nsa_kv[0], cache_nsa_win_kv[0], state_ffn_conv[0],
        page_table, w)
    return (y_p[None], y_s[:, None], moba_p[None, None], moba_s[None, :, None],
            nsa_p[None, None], nsa_s[None, :, None], win_p[None, None], win_s[None],
            conv_p[None, None], conv_s[None])
```

```python
import functools
import math

import numpy as np
import jax
import jax.numpy as jnp
from jax import lax
from jax.experimental import pallas as pl
from jax.experimental.pallas import tpu as pltpu

HEAD_DIM = 64
ROPE_DIM = HEAD_DIM // 4
ROPE_THETA = 500000.0
A_HEADS = 8
A_KV_HEADS = 4
A_BLOCK = 256
A_TOPK = 3
B_HEADS = 8
B_KV_HEADS = 2
B_GROUP = B_HEADS // B_KV_HEADS
CMP_LEN = 32
CMP_STRIDE = 16
CMP_HIDDEN = 128
SEL_BLOCK = 64
SEL_TOPN = 16
WINDOW = 512
CONV_W = 3
RMS_EPS = 1e-6
SCALE = HEAD_DIM ** -0.5

LANES = 128
VMEM_LIMIT = 56 * 1024 * 1024
MASK_BIAS = -32768.0
NEG_INF = float("-inf")

BF16 = jnp.bfloat16
F32 = jnp.float32


def _cparams(*sem):
    return pltpu.CompilerParams(dimension_semantics=sem, vmem_limit_bytes=VMEM_LIMIT)


def _dot(a, b):
    return jnp.dot(a, b, preferred_element_type=F32)


def _dot_t(a, b):
    return lax.dot_general(a, b, (((1,), (1,)), ((), ())), preferred_element_type=F32)


def _rms(x, g):
    y = x * lax.rsqrt(jnp.mean(x * x, axis=-1, keepdims=True) + RMS_EPS)
    return y * g


def _gelu(x):
    c = math.sqrt(2.0 / math.pi)
    return 0.5 * x * (1.0 + jnp.tanh(c * (x + 0.044715 * (x * x * x))))


def _rope128(x, c, sa, sb):
    return x * c + pltpu.roll(x, LANES - ROPE_DIM // 2, 1) * sa + pltpu.roll(x, ROPE_DIM // 2, 1) * sb


C_QA, C_KA, C_VA, C_QB = 0, 512, 768, 1024
C_KC, C_VC, C_KS, C_VS, C_KW, C_VW, C_G = 1536, 1664, 1792, 1920, 2048, 2176, 2304
D_IN_PAD = 2432


def _inproj_kernel(x_ref, g_ref, w_ref, bg_ref, c_ref, sa_ref, sb_ref,
                   kva_ref, nsa_ref, win_ref, gates_ref,
                   qa_ref, kaug_ref, va_ref, qb_ref, qbr_ref, ksaug_ref, vs_ref, kw_ref, vw_ref,
                   kcvc_ref, kmean_ref, *, tm, with_kmean):
    i = pl.program_id(0)
    h = _rms(x_ref[...], g_ref[...]).astype(BF16)
    c, sa, sb = c_ref[...], sa_ref[...], sb_ref[...]

    def proj(c0, width):
        return _dot(h, w_ref[:, c0:c0 + width])

    lane = lax.broadcasted_iota(jnp.int32, (tm, LANES), 1)
    row = lax.broadcasted_iota(jnp.int32, (tm, LANES), 0) + i * tm
    zeros64 = jnp.zeros((tm, HEAD_DIM), BF16)

    for p in range(4):
        q = _rope128(proj(C_QA + p * LANES, LANES), c, sa, sb)
        qa_ref[2 * p] = jnp.concatenate([q[:, :HEAD_DIM].astype(BF16), zeros64], axis=1)
        qa_ref[2 * p + 1] = jnp.concatenate([q[:, HEAD_DIM:].astype(BF16), zeros64], axis=1)

    a_onehot = (lane - HEAD_DIM == row // A_BLOCK).astype(BF16)
    ksum = []
    for p in range(2):
        k = _rope128(proj(C_KA + p * LANES, LANES), c, sa, sb)
        v = proj(C_VA + p * LANES, LANES)
        kva_ref[:, p * LANES:(p + 1) * LANES] = k
        kva_ref[:, 256 + p * LANES:256 + (p + 1) * LANES] = v
        kb = k.astype(BF16)
        vb = v.astype(BF16)
        kaug_ref[2 * p] = jnp.concatenate([kb[:, :HEAD_DIM], a_onehot[:, HEAD_DIM:]], axis=1)
        kaug_ref[2 * p + 1] = jnp.concatenate([kb[:, HEAD_DIM:], a_onehot[:, HEAD_DIM:]], axis=1)
        va_ref[2 * p] = vb[:, :HEAD_DIM]
        va_ref[2 * p + 1] = vb[:, HEAD_DIM:]
        if with_kmean:
            ksum.append(jnp.sum(k.reshape(tm // A_BLOCK, A_BLOCK, LANES), axis=1))
    if with_kmean:
        kmean_ref[0] = jnp.concatenate(ksum, axis=1) * (1.0 / A_BLOCK)
    else:
        kmean_ref[...] = jnp.zeros(kmean_ref.shape, F32)

    for p in range(4):
        q = proj(C_QB + p * LANES, LANES) * SCALE
        qr = _rope128(q, c, sa, sb)
        qb_ref[2 * p] = q[:, :HEAD_DIM].astype(BF16)
        qb_ref[2 * p + 1] = q[:, HEAD_DIM:].astype(BF16)
        qbr_ref[2 * p] = jnp.concatenate([qr[:, :HEAD_DIM].astype(BF16), zeros64], axis=1)
        qbr_ref[2 * p + 1] = jnp.concatenate([qr[:, HEAD_DIM:].astype(BF16), zeros64], axis=1)

    kc = proj(C_KC, LANES)
    vc = proj(C_VC, LANES)
    ks = _rope128(proj(C_KS, LANES), c, sa, sb)
    vs = proj(C_VS, LANES)
    nsa_ref[:, 0:128] = kc
    nsa_ref[:, 128:256] = vc
    nsa_ref[:, 256:384] = ks
    nsa_ref[:, 384:512] = vs
    kcvc_ref[0] = kc[:, :HEAD_DIM]
    kcvc_ref[1] = kc[:, HEAD_DIM:]
    kcvc_ref[2] = vc[:, :HEAD_DIM]
    kcvc_ref[3] = vc[:, HEAD_DIM:]
    s_onehot = (lane - HEAD_DIM == (row // SEL_BLOCK) % HEAD_DIM).astype(BF16)
    ksb = ks.astype(BF16)
    vsb = vs.astype(BF16)
    ksaug_ref[0] = jnp.concatenate([ksb[:, :HEAD_DIM], s_onehot[:, HEAD_DIM:]], axis=1)
    ksaug_ref[1] = jnp.concatenate([ksb[:, HEAD_DIM:], s_onehot[:, HEAD_DIM:]], axis=1)
    vs_ref[0] = vsb[:, :HEAD_DIM]
    vs_ref[1] = vsb[:, HEAD_DIM:]

    kw = _rope128(proj(C_KW, LANES), c, sa, sb)
    vw = proj(C_VW, LANES)
    win_ref[:, 0:128] = kw
    win_ref[:, 128:256] = vw
    kwb = kw.astype(BF16)
    vwb = vw.astype(BF16)
    kw_ref[0] = kwb[:, :HEAD_DIM]
    kw_ref[1] = kwb[:, HEAD_DIM:]
    vw_ref[0] = vwb[:, :HEAD_DIM]
    vw_ref[1] = vwb[:, HEAD_DIM:]

    gates_ref[...] = jax.nn.sigmoid(proj(C_G, LANES) + bg_ref[...])


def _rope_tables(pos):
    half = ROPE_DIM // 2
    inv = ROPE_THETA ** (-2.0 * jnp.arange(half, dtype=F32) / ROPE_DIM)
    ang = pos.astype(F32)[:, None] * inv[None, :]
    cos, sin = jnp.cos(ang), jnp.sin(ang)
    t = pos.shape[0]
    ones = jnp.ones((t, HEAD_DIM - ROPE_DIM), F32)
    zeros = jnp.zeros((t, HEAD_DIM - ROPE_DIM), F32)
    zh = jnp.zeros((t, half), F32)
    c = jnp.concatenate([cos, cos, ones], axis=1)
    sa = jnp.concatenate([-sin, zh, zeros], axis=1)
    sb = jnp.concatenate([zh, sin, zeros], axis=1)
    return tuple(jnp.concatenate([a, a], axis=1) for a in (c, sa, sb))


def _inproj(x2d, pos, g, w_pad, bg_pad, *, tm, with_kmean):
    t, d = x2d.shape
    nt = t // tm
    c, sa, sb = _rope_tables(pos)
    row_spec = lambda w: pl.BlockSpec((tm, w), lambda i: (i, 0))
    head_spec = lambda n, w: pl.BlockSpec((n, tm, w), lambda i: (0, i, 0))
    full = lambda a: pl.BlockSpec(a.shape, lambda i: (0,) * a.ndim)
    nkm = max(tm // A_BLOCK, 1)
    out_shape = (
        jax.ShapeDtypeStruct((t, 512), F32),
        jax.ShapeDtypeStruct((t, 512), F32),
        jax.ShapeDtypeStruct((t, 256), F32),
        jax.ShapeDtypeStruct((t, LANES), F32),
        jax.ShapeDtypeStruct((A_HEADS, t, LANES), BF16),
        jax.ShapeDtypeStruct((A_KV_HEADS, t, LANES), BF16),
        jax.ShapeDtypeStruct((A_KV_HEADS, t, HEAD_DIM), BF16),
        jax.ShapeDtypeStruct((B_HEADS, t, HEAD_DIM), BF16),
        jax.ShapeDtypeStruct((B_HEADS, t, LANES), BF16),
        jax.ShapeDtypeStruct((B_KV_HEADS, t, LANES), BF16),
        jax.ShapeDtypeStruct((B_KV_HEADS, t, HEAD_DIM), BF16),
        jax.ShapeDtypeStruct((B_KV_HEADS, t, HEAD_DIM), BF16),
        jax.ShapeDtypeStruct((B_KV_HEADS, t, HEAD_DIM), BF16),
        jax.ShapeDtypeStruct((4, t, HEAD_DIM), F32),
        jax.ShapeDtypeStruct((nt, nkm, 256), F32),
    )
    out_specs = (
        row_spec(512), row_spec(512), row_spec(256), row_spec(LANES),
        head_spec(A_HEADS, LANES), head_spec(A_KV_HEADS, LANES), head_spec(A_KV_HEADS, HEAD_DIM),
        head_spec(B_HEADS, HEAD_DIM), head_spec(B_HEADS, LANES), head_spec(B_KV_HEADS, LANES),
        head_spec(B_KV_HEADS, HEAD_DIM), head_spec(B_KV_HEADS, HEAD_DIM), head_spec(B_KV_HEADS, HEAD_DIM),
        head_spec(4, HEAD_DIM),
        pl.BlockSpec((1, nkm, 256), lambda i: (i, 0, 0)),
    )
    return pl.pallas_call(
        functools.partial(_inproj_kernel, tm=tm, with_kmean=with_kmean),
        out_shape=out_shape,
        grid=(nt,),
        in_specs=[row_spec(d), full(g), full(w_pad), full(bg_pad),
                  row_spec(LANES), row_spec(LANES), row_spec(LANES)],
        out_specs=out_specs,
        compiler_params=_cparams("parallel"),
        name="inproj",
    )(x2d, g, w_pad, bg_pad, c, sa, sb)


def _outproj_kernel(oa_ref, ob_ref, x_ref, wa_ref, wb_ref, gpost_ref, gpre_ref, x1_ref, h2_ref):
    mix = _dot(oa_ref[...].astype(BF16), wa_ref[...]) + _dot(ob_ref[...].astype(BF16), wb_ref[...])
    x1 = x_ref[...] + _rms(mix, gpost_ref[...])
    x1_ref[...] = x1
    h2_ref[...] = _rms(x1, gpre_ref[...]).astype(BF16)


def _outproj(oa, ob, x2d, w_out_bf, g_post, g_pre, *, tm):
    t, d = x2d.shape
    half = oa.shape[1]
    row = lambda w: pl.BlockSpec((tm, w), lambda i: (i, 0))
    full = lambda a: pl.BlockSpec(a.shape, lambda i: (0,) * a.ndim)
    wa, wb = w_out_bf[:half], w_out_bf[half:]
    return pl.pallas_call(
        _outproj_kernel,
        out_shape=(jax.ShapeDtypeStruct((t, d), F32), jax.ShapeDtypeStruct((t, d), BF16)),
        grid=(t // tm,),
        in_specs=[row(half), row(half), row(d), full(wa), full(wb), full(g_post), full(g_pre)],
        out_specs=(row(d), row(d)),
        compiler_params=_cparams("parallel"),
        name="outproj",
    )(oa, ob, x2d, wa, wb, g_post, g_pre)


HALO = 8


def _ffn_seq_kernel(h_ref, halo_ref, x1_ref, wg_ref, wv_ref, cg_ref, cv_ref, bg_ref, bv_ref,
                    wd_ref, gpost_ref, y_ref, tailg_ref, tailv_ref, acc_ref, ug_ref, uv_ref, *, tm):
    i, c = pl.program_id(0), pl.program_id(1)

    @pl.when(c == 0)
    def _():
        acc_ref[...] = jnp.zeros(acc_ref.shape, F32)

    keep = (i > 0).astype(F32)

    def conv(w_ref, u_ref, cw_ref, cb_ref, tail_ref):
        u_ref[0:HALO] = _dot(halo_ref[...], w_ref[...]) * keep
        u_ref[HALO:HALO + tm] = _dot(h_ref[...], w_ref[...])
        tail_ref[0] = u_ref[tm:tm + HALO]
        cw = cw_ref[...]
        return (u_ref[pl.ds(HALO - 2, tm), :] * cw[0:1] + u_ref[pl.ds(HALO - 1, tm), :] * cw[1:2]
                + u_ref[pl.ds(HALO, tm), :] * cw[2:3] + cb_ref[...])

    gate = conv(wg_ref, ug_ref, cg_ref, bg_ref, tailg_ref)
    val = conv(wv_ref, uv_ref, cv_ref, bv_ref, tailv_ref)
    acc_ref[...] += _dot((_gelu(gate) * val).astype(BF16), wd_ref[...])

    @pl.when(c == pl.num_programs(1) - 1)
    def _():
        y_ref[...] = x1_ref[...] + _rms(acc_ref[...], gpost_ref[...])


def _ffn_step_kernel(h_ref, p0g_ref, p1g_ref, p0v_ref, p1v_ref, x1_ref, wg_ref, wv_ref, cg_ref, cv_ref,
                     bg_ref, bv_ref, wd_ref, gpost_ref, y_ref, upg_ref, upv_ref, acc_ref):
    c = pl.program_id(0)

    @pl.when(c == 0)
    def _():
        acc_ref[...] = jnp.zeros(acc_ref.shape, F32)

    def conv(w_ref, p0_ref, p1_ref, cw_ref, cb_ref, up_ref):
        u = _dot(h_ref[...], w_ref[...])
        up_ref[...] = u
        cw = cw_ref[...]
        return p0_ref[...] * cw[0:1] + p1_ref[...] * cw[1:2] + u * cw[2:3] + cb_ref[...]

    gate = conv(wg_ref, p0g_ref, p1g_ref, cg_ref, bg_ref, upg_ref)
    val = conv(wv_ref, p0v_ref, p1v_ref, cv_ref, bv_ref, upv_ref)
    acc_ref[...] += _dot((_gelu(gate) * val).astype(BF16), wd_ref[...])

    @pl.when(c == pl.num_programs(0) - 1)
    def _():
        y_ref[...] = x1_ref[...] + _rms(acc_ref[...], gpost_ref[...])


def _ffn_seq(h2, x1, w_up_bf, w_conv, b_conv2d, w_down_bf, g_post, *, tm, ck):
    t, d = x1.shape
    dff = w_down_bf.shape[0]
    nff = dff // ck
    nt = t // tm
    hb = tm // HALO
    gcol = lambda r: pl.BlockSpec((r, ck), lambda i, c: (0, c))
    vcol = lambda r: pl.BlockSpec((r, ck), lambda i, c: (0, nff + c))
    row = pl.BlockSpec((tm, d), lambda i, c: (i, 0))
    tail = pl.BlockSpec((1, HALO, ck), lambda i, c: (i, 0, c))
    y, tg, tv = pl.pallas_call(
        functools.partial(_ffn_seq_kernel, tm=tm),
        out_shape=(jax.ShapeDtypeStruct((t, d), F32),
                   jax.ShapeDtypeStruct((nt, HALO, dff), F32), jax.ShapeDtypeStruct((nt, HALO, dff), F32)),
        grid=(nt, nff),
        in_specs=[row, pl.BlockSpec((HALO, d), lambda i, c: (jnp.maximum(i * hb - 1, 0), 0)), row,
                  gcol(d), vcol(d), gcol(CONV_W), vcol(CONV_W), gcol(1), vcol(1),
                  pl.BlockSpec((ck, d), lambda i, c: (c, 0)),
                  pl.BlockSpec(g_post.shape, lambda i, c: (0, 0))],
        out_specs=(row, tail, tail),
        scratch_shapes=[pltpu.VMEM((tm, d), F32), pltpu.VMEM((tm + HALO, ck), F32),
                        pltpu.VMEM((tm + HALO, ck), F32)],
        compiler_params=_cparams("parallel", "arbitrary"),
        name="ffn_seq",
    )(h2, h2, x1, w_up_bf, w_up_bf, w_conv, w_conv, b_conv2d, b_conv2d, w_down_bf, g_post)
    state = jnp.concatenate([tg[-1, HALO - 2:], tv[-1, HALO - 2:]], axis=1)
    return y, state


def _ffn_step(h2, x1, prev, w_up_bf, w_conv, b_conv2d, w_down_bf, g_post, *, ck):
    t, d = x1.shape
    dff = w_down_bf.shape[0]
    nff = dff // ck
    p0, p1 = prev[:, 0], prev[:, 1]
    gcol = lambda r: pl.BlockSpec((r, ck), lambda c: (0, c))
    vcol = lambda r: pl.BlockSpec((r, ck), lambda c: (0, nff + c))
    row = pl.BlockSpec((t, d), lambda c: (0, 0))
    y, ug, uv = pl.pallas_call(
        _ffn_step_kernel,
        out_shape=(jax.ShapeDtypeStruct((t, d), F32),
                   jax.ShapeDtypeStruct((t, dff), F32), jax.ShapeDtypeStruct((t, dff), F32)),
        grid=(nff,),
        in_specs=[row, gcol(t), gcol(t), vcol(t), vcol(t), row,
                  gcol(d), vcol(d), gcol(CONV_W), vcol(CONV_W), gcol(1), vcol(1),
                  pl.BlockSpec((ck, d), lambda c: (c, 0)),
                  pl.BlockSpec(g_post.shape, lambda c: (0, 0))],
        out_specs=(row, pl.BlockSpec((t, ck), lambda c: (0, c)), pl.BlockSpec((t, ck), lambda c: (0, c))),
        scratch_shapes=[pltpu.VMEM((t, d), F32)],
        compiler_params=_cparams("arbitrary"),
        name="ffn_step",
    )(h2, p0, p1, p0, p1, x1, w_up_bf, w_up_bf, w_conv, w_conv, b_conv2d, b_conv2d, w_down_bf, g_post)
    state = jnp.stack([p1, jnp.concatenate([ug, uv], axis=1)], axis=1)
    return y, state


ATT_KT = 512


def _softmax_init(m_sc, l_sc, acc_sc):
    m_sc[...] = jnp.full(m_sc.shape, NEG_INF, F32)
    l_sc[...] = jnp.zeros(l_sc.shape, F32)
    acc_sc[...] = jnp.zeros(acc_sc.shape, F32)


def _two_pass_attention(q3_ref, k_ref, v_ref, n_full, tiles_per_win, qpos, m_sc, l_sc, acc_sc, s_sc):
    nwin, rows, _ = q3_ref.shape
    kt = s_sc.shape[2]
    groups = [slice(j * LANES, (j + 1) * LANES) for j in range(kt // LANES)]

    def scores(t):
        off = pl.multiple_of(t * kt, kt)
        w = t // tiles_per_win if nwin > 1 else 0
        return _dot_t(q3_ref[w], k_ref[0, pl.ds(off, kt), :])

    def causal(s):
        kpos = n_full * kt + lax.broadcasted_iota(jnp.int32, (rows, kt), 1)
        return jnp.where(kpos <= qpos, s, NEG_INF)

    def take_max(s):
        mx = s[:, groups[0]]
        for gs in groups[1:]:
            mx = jnp.maximum(mx, s[:, gs])
        m_sc[...] = jnp.maximum(m_sc[...], mx)

    def consume(t, s):
        m = m_sc[...]
        ps = [jnp.exp(s[:, gs] - m) for gs in groups]
        tot = ps[0]
        for p in ps[1:]:
            tot = tot + p
        l_sc[...] += tot
        off = pl.multiple_of(t * kt, kt)
        acc_sc[...] += _dot(jnp.concatenate(ps, axis=1).astype(BF16), v_ref[0, pl.ds(off, kt), :])

    _softmax_init(m_sc, l_sc, acc_sc)

    def max_body(t, carry):
        take_max(scores(t))
        return carry

    lax.fori_loop(0, n_full, max_body, 0)
    take_max(causal(scores(n_full)))
    m_sc[...] = jnp.broadcast_to(jnp.max(m_sc[...], axis=1, keepdims=True), m_sc.shape)

    s_sc[0] = scores(0)

    def sum_body(t, carry):
        s = s_sc[t % 2]
        s_sc[(t + 1) % 2] = scores(t + 1)
        consume(t, s)
        return carry

    lax.fori_loop(0, n_full, sum_body, 0)
    consume(n_full, causal(s_sc[n_full % 2]))
    return acc_sc[...] / jnp.sum(l_sc[...], axis=1, keepdims=True)


def _top_select(v, lane, forced, rounds):
    sel = forced
    lane = lane.astype(F32)
    for _ in range(rounds):
        mx = jnp.max(v, axis=1, keepdims=True)
        idx = jnp.min(jnp.where(v == mx, lane, 1e9), axis=1, keepdims=True)
        pick = (lane == idx) & (mx > NEG_INF)
        sel = sel | pick
        v = jnp.where(pick, NEG_INF, v)
    return sel


def _masked_softmax(s, mask):
    s = jnp.where(mask, s, NEG_INF)
    m = jnp.max(s, axis=1, keepdims=True)
    m = jnp.where(m > NEG_INF, m, 0.0)
    e = jnp.where(mask, jnp.exp(s - m), 0.0)
    d = jnp.sum(e, axis=1, keepdims=True)
    return e / jnp.where(d > 0, d, 1.0)


def _moba_prompt_kernel(q_ref, km_ref, k_ref, v_ref, o_ref, m_sc, l_sc, acc_sc, q3_sc, s_sc):
    cur = pl.program_id(1)
    rows = 2 * A_BLOCK
    q = q_ref[...].reshape(rows, LANES)
    gate = _dot(q, km_ref[0])
    lane = lax.broadcasted_iota(jnp.int32, (rows, LANES), 1)
    past = (lane >= HEAD_DIM) & (lane < HEAD_DIM + cur)
    sel = _top_select(jnp.where(past, gate, NEG_INF), lane, lane == HEAD_DIM + cur, A_TOPK)
    bias = jnp.where(sel | (lane < HEAD_DIM), 0.0, MASK_BIAS)
    q3_sc[0] = (q.astype(F32) * SCALE + bias).astype(BF16)

    qpos = cur * A_BLOCK + lax.broadcasted_iota(jnp.int32, (rows, 1), 0) % A_BLOCK
    o = _two_pass_attention(q3_sc, k_ref, v_ref, cur // (ATT_KT // A_BLOCK), None, qpos,
                            m_sc, l_sc, acc_sc, s_sc)
    o_ref[:, 0:HEAD_DIM] = o[:A_BLOCK]
    o_ref[:, HEAD_DIM:LANES] = o[A_BLOCK:]


def _gate_matrix(kmean):
    nblk = kmean.shape[0]
    km = kmean.reshape(nblk, A_KV_HEADS, HEAD_DIM).transpose(1, 2, 0)
    km = jnp.pad(km, ((0, 0), (0, LANES - HEAD_DIM), (HEAD_DIM, LANES - HEAD_DIM - nblk)))
    return km.astype(BF16)


def _moba_prompt(qa, km, kaug, va):
    t = qa.shape[1]
    nt = t // A_BLOCK
    assert nt <= HEAD_DIM, "key-block one-hot occupies 64 lanes"
    assert t % ATT_KT == 0 and ATT_KT % A_BLOCK == 0
    rows = 2 * A_BLOCK
    return pl.pallas_call(
        _moba_prompt_kernel,
        out_shape=jax.ShapeDtypeStruct((t, A_HEADS * HEAD_DIM), F32),
        grid=(A_KV_HEADS, nt),
        in_specs=[pl.BlockSpec((2, A_BLOCK, LANES), lambda g, i: (g, i, 0)),
                  pl.BlockSpec((1, LANES, LANES), lambda g, i: (g, 0, 0)),
                  pl.BlockSpec((1, t, LANES), lambda g, i: (g, 0, 0)),
                  pl.BlockSpec((1, t, HEAD_DIM), lambda g, i: (g, 0, 0))],
        out_specs=pl.BlockSpec((A_BLOCK, LANES), lambda g, i: (i, g)),
        scratch_shapes=[pltpu.VMEM((rows, LANES), F32), pltpu.VMEM((rows, LANES), F32),
                        pltpu.VMEM((rows, HEAD_DIM), F32), pltpu.VMEM((1, rows, LANES), BF16),
                        pltpu.VMEM((2, rows, ATT_KT), F32)],
        compiler_params=_cparams("parallel", "parallel"),
        name="moba_prompt",
    )(qa, km, kaug, va)


def _compress_kernel(x_ref, pelo_ref, pehi_ref, wlo_ref, whi_ref, b1_ref, w2_ref, o_ref, hi_sc, *, nch):
    x = x_ref[0]
    lo = _dot((x + pelo_ref[0]).astype(BF16), wlo_ref[0])
    hi_sc[0:nch] = _dot((x + pehi_ref[0]).astype(BF16), whi_ref[0])
    hi_sc[nch:nch + 8] = jnp.zeros((8, CMP_HIDDEN), F32)
    hid = _gelu(lo + hi_sc[pl.ds(1, nch), :] + b1_ref[0])
    o_ref[0] = _dot(hid.astype(BF16), w2_ref[0]).astype(o_ref.dtype)


def _compress_weights(cmp_pe, cmp_w1, cmp_b1, cmp_w2):
    flat = CMP_STRIDE * HEAD_DIM
    pelo = cmp_pe[:, :CMP_STRIDE].reshape(2, 1, flat)
    pehi = cmp_pe[:, CMP_STRIDE:].reshape(2, 1, flat)
    wlo = cmp_w1[:, :CMP_STRIDE].reshape(2, flat, CMP_HIDDEN).astype(BF16)
    whi = cmp_w1[:, CMP_STRIDE:].reshape(2, flat, CMP_HIDDEN).astype(BF16)
    return pelo, pehi, wlo, whi, cmp_b1.reshape(2, 1, CMP_HIDDEN), cmp_w2.astype(BF16)


def _compress(xch, cw):
    _, nch, flat = xch.shape
    pelo, pehi, wlo, whi, b1, w2 = cw
    kind = lambda shape: pl.BlockSpec((1,) + shape, lambda j: (j // B_KV_HEADS, 0, 0))
    return pl.pallas_call(
        functools.partial(_compress_kernel, nch=nch),
        out_shape=jax.ShapeDtypeStruct((4, nch, HEAD_DIM), BF16),
        grid=(4,),
        in_specs=[pl.BlockSpec((1, nch, flat), lambda j: (j, 0, 0)),
                  kind((1, flat)), kind((1, flat)), kind((flat, CMP_HIDDEN)), kind((flat, CMP_HIDDEN)),
                  kind((1, CMP_HIDDEN)), kind((CMP_HIDDEN, HEAD_DIM))],
        out_specs=pl.BlockSpec((1, nch, HEAD_DIM), lambda j: (j, 0, 0)),
        scratch_shapes=[pltpu.VMEM((nch + 8, CMP_HIDDEN), F32)],
        compiler_params=_cparams("parallel"),
        name="compress",
    )(xch, pelo, pehi, wlo, whi, b1, w2)


def _importance_matrix(ncmp, nsel):
    ratio = SEL_BLOCK // CMP_STRIDE
    n = np.arange(ncmp)[:, None]
    j = np.arange(nsel)[None, :]
    own = (n // ratio == j)
    last = (n % ratio == ratio - 1)
    m = np.where(own & ~last, 1.0, 0.0) + np.where(last & (own | (n // ratio == j - 1)), 0.5, 0.0)
    return jnp.asarray(m, BF16)


NSA_TQ = 128
WIN_BLOCKS = WINDOW // NSA_TQ + 1


def _nsa_prompt_kernel(*refs, nsel, ncmp):
    (qb_ref, qbr_ref, kc_ref, vc_ref, imp_ref, ks_ref, vs_ref) = refs[:7]
    kw_refs = refs[7:7 + WIN_BLOCKS]
    vw_refs = refs[7 + WIN_BLOCKS:7 + 2 * WIN_BLOCKS]
    g_ref, o_ref, m_sc, l_sc, acc_sc, q3_sc, s_sc = refs[7 + 2 * WIN_BLOCKS:]
    i = pl.program_id(1)
    rows = B_GROUP * NSA_TQ
    q0 = i * NSA_TQ
    q = qb_ref[...].reshape(rows, HEAD_DIM)
    qr = qbr_ref[...].reshape(rows, LANES)

    s = _dot_t(q, kc_ref[0])
    qpos_c = q0 + lax.broadcasted_iota(jnp.int32, (rows, ncmp), 0) % NSA_TQ
    n_c = lax.broadcasted_iota(jnp.int32, (rows, ncmp), 1)
    p = _masked_softmax(s, n_c * CMP_STRIDE + (CMP_LEN - 1) <= qpos_c)
    o_cmp = _dot(p.astype(BF16), vc_ref[0])

    psum = p[0:NSA_TQ]
    for j in range(1, B_GROUP):
        psum = psum + p[j * NSA_TQ:(j + 1) * NSA_TQ]
    p_hi = psum.astype(BF16)
    p_lo = (psum - p_hi.astype(F32)).astype(BF16)
    imp = _dot(p_hi, imp_ref[...]) + _dot(p_lo, imp_ref[...])
    blk = lax.broadcasted_iota(jnp.int32, (NSA_TQ, nsel), 1)
    cur = (q0 + lax.broadcasted_iota(jnp.int32, (NSA_TQ, nsel), 0)) // SEL_BLOCK
    forced = (blk == 0) | (blk == cur) | (blk == cur - 1)
    cand = jnp.where(blk > cur, NEG_INF, jnp.where(forced, jnp.inf, imp))
    sel = _top_select(cand, blk, jnp.zeros((NSA_TQ, nsel), jnp.bool_), min(SEL_TOPN, nsel))
    selbias = jnp.where(sel, 0.0, MASK_BIAS)

    kt_last = (q0 + NSA_TQ - 1) // ATT_KT
    tiles_per_win = HEAD_DIM * SEL_BLOCK // ATT_KT
    qr32 = qr.astype(F32)
    for w in range(-(-nsel // HEAD_DIM)):
        nb = min(HEAD_DIM, nsel - w * HEAD_DIM)
        pieces = [jnp.zeros((NSA_TQ, HEAD_DIM), F32), selbias[:, w * HEAD_DIM:w * HEAD_DIM + nb]]
        if nb < HEAD_DIM:
            pieces.append(jnp.zeros((NSA_TQ, HEAD_DIM - nb), F32))
        bias_w = jnp.concatenate(pieces, axis=1)
        q3_sc[w] = (qr32 + jnp.concatenate([bias_w] * B_GROUP, axis=0)).astype(BF16)

    qpos_col = q0 + lax.broadcasted_iota(jnp.int32, (rows, 1), 0) % NSA_TQ
    o_sel = _two_pass_attention(q3_sc, ks_ref, vs_ref, kt_last, tiles_per_win, qpos_col,
                                m_sc, l_sc, acc_sc, s_sc)

    kband = jnp.concatenate([r[0] for r in kw_refs], axis=0)
    vband = jnp.concatenate([r[0] for r in vw_refs], axis=0)
    nband = WIN_BLOCKS * NSA_TQ
    s = _dot_t(qr[:, :HEAD_DIM], kband)
    qpos = q0 + lax.broadcasted_iota(jnp.int32, (rows, nband), 0) % NSA_TQ
    kpos = q0 - WINDOW + lax.broadcasted_iota(jnp.int32, (rows, nband), 1)
    p = _masked_softmax(s, (kpos <= qpos) & (kpos >= qpos - WINDOW) & (kpos >= 0))
    o_win = _dot(p.astype(BF16), vband)

    gts = g_ref[0]
    for j in range(B_GROUP):
        sl = slice(j * NSA_TQ, (j + 1) * NSA_TQ)
        o = (gts[:, 3 * j:3 * j + 1] * o_cmp[sl] + gts[:, 3 * j + 1:3 * j + 2] * o_sel[sl]
             + gts[:, 3 * j + 2:3 * j + 3] * o_win[sl])
        o_ref[:, j * HEAD_DIM:(j + 1) * HEAD_DIM] = o


def _nsa_prompt(qb, qbr, kvcmp, ksaug, vs, kw, vw, gates):
    t = qb.shape[1]
    nt = t // NSA_TQ
    nsel = t // SEL_BLOCK
    ncmp = kvcmp.shape[1]
    rows = B_GROUP * NSA_TQ
    impm = _importance_matrix(ncmp, nsel)
    res = lambda w: pl.BlockSpec((1, t, w), lambda g, i: (g, 0, 0))
    band = [pl.BlockSpec((1, NSA_TQ, HEAD_DIM),
                         functools.partial(lambda g, i, j: (g, jnp.maximum(i - (WIN_BLOCKS - 1) + j, 0), 0), j=j))
            for j in range(WIN_BLOCKS)]
    return pl.pallas_call(
        functools.partial(_nsa_prompt_kernel, nsel=nsel, ncmp=ncmp),
        out_shape=jax.ShapeDtypeStruct((t, B_HEADS * HEAD_DIM), F32),
        grid=(B_KV_HEADS, nt),
        in_specs=[pl.BlockSpec((B_GROUP, NSA_TQ, HEAD_DIM), lambda g, i: (g, i, 0)),
                  pl.BlockSpec((B_GROUP, NSA_TQ, LANES), lambda g, i: (g, i, 0)),
                  pl.BlockSpec((1, ncmp, HEAD_DIM), lambda g, i: (g, 0, 0)),
                  pl.BlockSpec((1, ncmp, HEAD_DIM), lambda g, i: (B_KV_HEADS + g, 0, 0)),
                  pl.BlockSpec(impm.shape, lambda g, i: (0, 0)),
                  res(LANES), res(HEAD_DIM)] + band + band +
                 [pl.BlockSpec((1, NSA_TQ, 3 * B_GROUP), lambda g, i: (g, i, 0))],
        out_specs=pl.BlockSpec((NSA_TQ, B_GROUP * HEAD_DIM), lambda g, i: (i, g)),
        scratch_shapes=[pltpu.VMEM((rows, LANES), F32), pltpu.VMEM((rows, LANES), F32),
                        pltpu.VMEM((rows, HEAD_DIM), F32),
                        pltpu.VMEM((-(-nsel // HEAD_DIM), rows, LANES), BF16),
                        pltpu.VMEM((2, rows, ATT_KT), F32)],
        compiler_params=_cparams("parallel", "parallel"),
        name="nsa_prompt",
    )(qb, qbr, kvcmp, kvcmp, impm, ksaug, vs, *([kw] * WIN_BLOCKS), *([vw] * WIN_BLOCKS), gates)


def _top_indices(v, lane, rounds):
    out = jnp.zeros((v.shape[0], LANES), jnp.int32)
    slot = lax.broadcasted_iota(jnp.int32, out.shape, 1)
    for r in range(rounds):
        mx = jnp.max(v, axis=1, keepdims=True)
        idx = jnp.min(jnp.where(v == mx, lane, 1 << 20), axis=1, keepdims=True)
        out = jnp.where(slot == r, idx, out)
        v = jnp.where(lane == idx, NEG_INF, v)
    return out


def _bf16_round(x):
    return x.astype(BF16).astype(F32)


def _moba_kmean_kernel(pt_ref, *refs, pg):
    pages, o_ref = refs[:pg], refs[pg]
    step = pl.program_id(1)
    per_blk = A_BLOCK // pages[0].shape[2]
    nb = pg // per_blk

    @pl.when(step == 0)
    def _():
        o_ref[...] = jnp.zeros(o_ref.shape, F32)

    lane = lax.broadcasted_iota(jnp.int32, o_ref.shape[1:], 1)
    out = o_ref[0]
    for j in range(nb):
        tot = pages[j * per_blk][0]
        for r in pages[j * per_blk + 1:(j + 1) * per_blk]:
            tot = tot + r[0]
        mean = jnp.sum(tot, axis=1, keepdims=True) * (1.0 / A_BLOCK)
        out = jnp.where(lane == step * nb + j, mean, out)
    o_ref[0] = out


def _moba_kmean(cache_t, page_table, *, pg):
    db, npg = page_table.shape
    page = cache_t.shape[2]
    kw = A_KV_HEADS * HEAD_DIM
    assert npg * page // A_BLOCK <= LANES
    specs = [pl.BlockSpec((1, kw, page), functools.partial(lambda b, s, pt, j: (pt[b, s * pg + j], 0, 0), j=j))
             for j in range(pg)]
    return pl.pallas_call(
        functools.partial(_moba_kmean_kernel, pg=pg),
        out_shape=jax.ShapeDtypeStruct((db, kw, LANES), F32),
        grid_spec=pltpu.PrefetchScalarGridSpec(
            num_scalar_prefetch=1, grid=(db, npg // pg), in_specs=specs,
            out_specs=pl.BlockSpec((1, kw, LANES), lambda b, s, pt: (b, 0, 0))),
        compiler_params=_cparams("parallel", "arbitrary"),
        name="moba_kmean",
    )(page_table, *([cache_t] * pg))


def _moba_gate_kernel(q_ref, km_ref, idx_ref, *, nblk):
    q = q_ref[0][:, :HEAD_DIM].astype(BF16)
    head = lax.broadcasted_iota(jnp.int32, (A_HEADS, LANES), 0)
    lane = lax.broadcasted_iota(jnp.int32, (A_HEADS, LANES), 1)
    gate = jnp.zeros((A_HEADS, LANES), F32)
    for g in range(A_KV_HEADS):
        kmg = km_ref[0][g * HEAD_DIM:(g + 1) * HEAD_DIM, :].astype(BF16)
        gate = jnp.where(head // (A_HEADS // A_KV_HEADS) == g, _dot(q, kmg), gate)
    idx_ref[0] = _top_indices(jnp.where(lane < nblk, gate, NEG_INF), lane, A_TOPK)


def _moba_gate(qa_rows, kmean_t, *, nblk):
    db, kw, _ = kmean_t.shape
    return pl.pallas_call(
        functools.partial(_moba_gate_kernel, nblk=nblk),
        out_shape=jax.ShapeDtypeStruct((db, A_HEADS, LANES), jnp.int32),
        grid=(db,),
        in_specs=[pl.BlockSpec((1, A_HEADS, LANES), lambda b: (b, 0, 0)),
                  pl.BlockSpec((1, kw, LANES), lambda b: (b, 0, 0))],
        out_specs=pl.BlockSpec((1, A_HEADS, LANES), lambda b: (b, 0, 0)),
        compiler_params=_cparams("parallel"),
        name="moba_gate",
    )(qa_rows, kmean_t)


def _gathered_attention(q, k_pages, v_pages, masks, k_own, v_own):
    qb = q.astype(BF16)
    ss = [_dot(qb, k.astype(BF16)) for k in k_pages]
    ss = [s if mk is None else jnp.where(mk, s, NEG_INF) for s, mk in zip(ss, masks)]
    s_own = jnp.sum(q * _bf16_round(k_own), axis=1, keepdims=True)
    m = s_own
    for s in ss:
        m = jnp.maximum(m, jnp.max(s, axis=1, keepdims=True))
    e_own = jnp.exp(s_own - m)
    d = e_own
    acc = _bf16_round(e_own) * _bf16_round(v_own)
    for s, v in zip(ss, v_pages):
        e = jnp.exp(s - m)
        d = d + jnp.sum(e, axis=1, keepdims=True)
        acc = acc + _dot_t(e.astype(BF16), v.astype(BF16))
    return acc / d


def _moba_sample_kernel(pt_ref, ix_ref, q_ref, *refs, npick):
    k_refs, v_refs = refs[:npick], refs[npick:2 * npick]
    kn_ref, vn_ref, o_ref, o_sc = refs[2 * npick:]
    h = pl.program_id(1)
    q = q_ref[0][:, :HEAD_DIM] * SCALE
    o_sc[...] = _gathered_attention(q, [r[0] for r in k_refs], [r[0] for r in v_refs], [None] * npick,
                                    kn_ref[0], vn_ref[0])
    o_ref[0, pl.ds(h, 1), :] = o_sc[pl.ds(h, 1), :]


def _moba_sample(cache_t, page_table, idx, qa_rows, k_new, v_new):
    db = page_table.shape[0]
    page = cache_t.shape[2]
    npage = A_BLOCK // page
    rep = A_HEADS // A_KV_HEADS

    def kv_spec(s, j, row_blk0):
        def imap(b, h, pt, ix):
            blk = ix[(b * A_HEADS + h) * A_TOPK + s]
            return (pt[b, blk * npage + j], row_blk0 + h // rep, 0)
        return pl.BlockSpec((1, HEAD_DIM, page), imap)

    picks = [(s, j) for s in range(A_TOPK) for j in range(npage)]
    per_seq = lambda w: pl.BlockSpec((1, A_HEADS, w), lambda b, h, pt, ix: (b, 0, 0))
    return pl.pallas_call(
        functools.partial(_moba_sample_kernel, npick=len(picks)),
        out_shape=jax.ShapeDtypeStruct((db, A_HEADS, HEAD_DIM), F32),
        grid_spec=pltpu.PrefetchScalarGridSpec(
            num_scalar_prefetch=2, grid=(db, A_HEADS),
            in_specs=[per_seq(LANES)]
                     + [kv_spec(s, j, 0) for s, j in picks] + [kv_spec(s, j, A_KV_HEADS) for s, j in picks]
                     + [per_seq(HEAD_DIM), per_seq(HEAD_DIM)],
            out_specs=per_seq(HEAD_DIM),
            scratch_shapes=[pltpu.VMEM((A_HEADS, HEAD_DIM), F32)]),
        compiler_params=_cparams("parallel", "arbitrary"),
        name="moba_sample",
    )(page_table, idx, qa_rows, *([cache_t] * (2 * len(picks))), k_new, v_new)


def _nsa_flatten_kernel(pt_ref, *refs, pg, nch):
    pages = refs[:pg]
    pelo_ref, pehi_ref, wlo_ref, whi_ref, b1_ref, w2_ref, o_ref, x_sc, hi_sc, rows_sc = refs[pg:]
    step = pl.program_id(1)
    page = pages[0].shape[2]
    nc = pg * page // CMP_STRIDE
    base = pl.multiple_of(step * nc, nc)
    lane = lax.broadcasted_iota(jnp.int32, (nc, LANES), 1)
    lo_half = lane < HEAD_DIM
    for j, r in enumerate(pages):
        for pr in range(2):
            rows_sc[pr, j * page:(j + 1) * page, :] = r[0, pr * LANES:(pr + 1) * LANES, :].T
    for u in range(CMP_STRIDE // 2):
        for pr in range(2):
            ap = rows_sc[pr, pl.ds(2 * u, nc, stride=CMP_STRIDE), :]
            bp = rows_sc[pr, pl.ds(2 * u + 1, nc, stride=CMP_STRIDE), :]
            x_sc[2 * pr, pl.ds(base, nc), u * LANES:(u + 1) * LANES] = jnp.where(
                lo_half, ap, pltpu.roll(bp, HEAD_DIM, 1))
            x_sc[2 * pr + 1, pl.ds(base, nc), u * LANES:(u + 1) * LANES] = jnp.where(
                lo_half, pltpu.roll(ap, HEAD_DIM, 1), bp)

    @pl.when(step == pl.num_programs(1) - 1)
    def _():
        hi_sc[nch:nch + 8] = jnp.zeros((8, CMP_HIDDEN), F32)
        for j in range(4):
            c = j // B_KV_HEADS
            x = x_sc[j]
            lo = _dot((x + pelo_ref[c]).astype(BF16), wlo_ref[c])
            hi_sc[0:nch] = _dot((x + pehi_ref[c]).astype(BF16), whi_ref[c])
            hid = _gelu(lo + hi_sc[pl.ds(1, nch), :] + b1_ref[c])
            o_ref[0, j] = _dot(hid.astype(BF16), w2_ref[c]).astype(o_ref.dtype)


def _nsa_sample_compress(cache_t, page_table, cw, *, pg):
    db, npg = page_table.shape
    page = cache_t.shape[2]
    assert page == LANES
    nch = npg * page // CMP_STRIDE
    flat = CMP_STRIDE * HEAD_DIM
    pelo, pehi, wlo, whi, b1, w2 = cw
    full = lambda a: pl.BlockSpec(a.shape, lambda b, s, pt: (0,) * a.ndim)
    specs = [pl.BlockSpec((1, 2 * LANES, page), functools.partial(lambda b, s, pt, j: (pt[b, s * pg + j], 0, 0), j=j))
             for j in range(pg)]
    return pl.pallas_call(
        functools.partial(_nsa_flatten_kernel, pg=pg, nch=nch),
        out_shape=jax.ShapeDtypeStruct((db, 4, nch, HEAD_DIM), BF16),
        grid_spec=pltpu.PrefetchScalarGridSpec(
            num_scalar_prefetch=1, grid=(db, npg // pg),
            in_specs=specs + [full(a) for a in (pelo, pehi, wlo, whi, b1, w2)],
            out_specs=pl.BlockSpec((1, 4, nch, HEAD_DIM), lambda b, s, pt: (b, 0, 0, 0)),
            scratch_shapes=[pltpu.VMEM((4, nch, flat), F32), pltpu.VMEM((nch + 8, CMP_HIDDEN), F32),
                            pltpu.VMEM((2, pg * page, LANES), F32)]),
        compiler_params=_cparams("parallel", "arbitrary"),
        name="nsa_sample_compress",
    )(page_table, *([cache_t] * pg), pelo, pehi, wlo, whi, b1, w2)


def _rows_by_group(per_group):
    head = lax.broadcasted_iota(jnp.int32, per_group[0].shape, 0)
    out = per_group[0]
    for g in range(1, B_KV_HEADS):
        out = jnp.where(head // B_GROUP == g, per_group[g], out)
    return out


def _nsa_sample_select_kernel(q_ref, qr_ref, kv_ref, imp_ref, win_ref, wnew_ref, g_ref, idx_ref, ocw_ref,
                              *, past, nsel_past, ncmp):
    q = q_ref[0][:, :HEAD_DIM].astype(BF16)
    qr = qr_ref[0][:, :HEAD_DIM]
    gts = g_ref[0]
    glane = lax.broadcasted_iota(jnp.int32, (B_HEADS, LANES), 1)
    ghead = lax.broadcasted_iota(jnp.int32, (B_HEADS, LANES), 0)
    gate = lambda c: jnp.sum(jnp.where(glane == 3 * ghead + c, gts, 0.0), axis=1, keepdims=True)

    s = _rows_by_group([_dot_t(q, kv_ref[0, g]) for g in range(B_KV_HEADS)])
    n_c = lax.broadcasted_iota(jnp.int32, (B_HEADS, ncmp), 1)
    p = _masked_softmax(s, n_c * CMP_STRIDE + (CMP_LEN - 1) <= past)
    o_cmp = _rows_by_group([_dot(p.astype(BF16), kv_ref[0, B_KV_HEADS + g]) for g in range(B_KV_HEADS)])

    head = lax.broadcasted_iota(jnp.int32, (B_HEADS, ncmp), 0)
    psum = jnp.zeros((B_HEADS, ncmp), F32)
    for g in range(B_KV_HEADS):
        tot = jnp.sum(jnp.where(head // B_GROUP == g, p, 0.0), axis=0, keepdims=True)
        psum = jnp.where(head == g, tot, psum)
    p_hi = psum.astype(BF16)
    p_lo = (psum - p_hi.astype(F32)).astype(BF16)
    imp = _dot(p_hi, imp_ref[...]) + _dot(p_lo, imp_ref[...])
    blk = lax.broadcasted_iota(jnp.int32, (B_HEADS, nsel_past), 1)
    cand = jnp.where((blk == 0) | (blk == nsel_past - 1), jnp.inf, imp)
    idx_ref[0] = _top_indices(cand, blk, min(SEL_TOPN, nsel_past + 1) - 1)

    wk = [win_ref[0][g * HEAD_DIM:(g + 1) * HEAD_DIM, :] for g in range(B_KV_HEADS)]
    wv = [win_ref[0][(B_KV_HEADS + g) * HEAD_DIM:(B_KV_HEADS + g + 1) * HEAD_DIM, :] for g in range(B_KV_HEADS)]
    nk = [wnew_ref[0][:, g * HEAD_DIM:(g + 1) * HEAD_DIM] for g in range(B_KV_HEADS)]
    nv = [wnew_ref[0][:, (B_KV_HEADS + g) * HEAD_DIM:(B_KV_HEADS + g + 1) * HEAD_DIM] for g in range(B_KV_HEADS)]
    s_w = _rows_by_group([_dot(qr.astype(BF16), k.astype(BF16)) for k in wk])
    s_n = _rows_by_group([jnp.sum(qr * _bf16_round(k), axis=1, keepdims=True) for k in nk])
    m = jnp.maximum(jnp.max(s_w, axis=1, keepdims=True), s_n)
    e_w, e_n = jnp.exp(s_w - m), jnp.exp(s_n - m)
    d = jnp.sum(e_w, axis=1, keepdims=True) + e_n
    o_win = _rows_by_group([_dot_t(e_w.astype(BF16), v.astype(BF16)) + _bf16_round(e_n) * _bf16_round(nvg)
                            for v, nvg in zip(wv, nv)]) / d

    ocw_ref[0] = jnp.concatenate([gate(0) * o_cmp, gate(2) * o_win], axis=1)


def _nsa_sample_select(qb_rows, qbr_rows, kvcmp, win_cache, win_new, gates, *, past):
    db, _, ncmp, _ = kvcmp.shape
    nsel_past = past // SEL_BLOCK
    impm = _importance_matrix(ncmp, nsel_past)
    row3 = lambda a: pl.BlockSpec((1,) + a.shape[1:], lambda b: (b,) + (0,) * (a.ndim - 1))
    kern = functools.partial(_nsa_sample_select_kernel, past=past, nsel_past=nsel_past, ncmp=ncmp)
    return pl.pallas_call(
        kern,
        out_shape=(jax.ShapeDtypeStruct((db, B_HEADS, LANES), jnp.int32),
                   jax.ShapeDtypeStruct((db, B_HEADS, LANES), F32)),
        grid=(db,),
        in_specs=[row3(qb_rows), row3(qbr_rows), row3(kvcmp), pl.BlockSpec(impm.shape, lambda b: (0, 0)),
                  row3(win_cache), row3(win_new), row3(gates)],
        out_specs=(pl.BlockSpec((1, B_HEADS, LANES), lambda b: (b, 0, 0)),
                   pl.BlockSpec((1, B_HEADS, LANES), lambda b: (b, 0, 0))),
        compiler_params=_cparams("parallel"),
        name="nsa_sample_select",
    )(qb_rows, qbr_rows, kvcmp, impm, win_cache, win_new, gates)


def _nsa_sample_attend_kernel(pt_ref, ix_ref, qr_ref, *refs, nslot, per_page):
    k_refs, v_refs = refs[:nslot], refs[nslot:2 * nslot]
    kn_ref, vn_ref, g_ref, ocw_ref, o_ref = refs[2 * nslot:]
    b, g = pl.program_id(0), pl.program_id(1)
    qs = qr_ref[0, pl.ds(g * B_GROUP, B_GROUP), :][:, :HEAD_DIM]

    page = k_refs[0].shape[2]
    lane = lax.broadcasted_iota(jnp.int32, (B_GROUP, page), 1)
    masks = [lane // SEL_BLOCK == ix_ref[(b * B_KV_HEADS + g) * nslot + s] % per_page for s in range(nslot)]
    o_sel = _gathered_attention(qs, [r[0] for r in k_refs], [r[0] for r in v_refs], masks,
                                kn_ref[0, 0], vn_ref[0, 0])
    gts = g_ref[0]
    hrow = lax.broadcasted_iota(jnp.int32, (B_GROUP, LANES), 0) + g * B_GROUP
    glane = lax.broadcasted_iota(jnp.int32, (B_GROUP, LANES), 1)
    g1 = jnp.sum(jnp.where(glane == 3 * hrow + 1, gts, 0.0), axis=1, keepdims=True)
    ocw = ocw_ref[0, pl.ds(g * B_GROUP, B_GROUP), :]
    o_ref[0, pl.ds(g * B_GROUP, B_GROUP), :] = g1 * o_sel + ocw[:, :HEAD_DIM] + ocw[:, HEAD_DIM:]


def _nsa_sample_attend(cache_t, page_table, idx, qbr_rows, ks_new, vs_new, gates, ocw, *, nslot):
    db = page_table.shape[0]
    page = cache_t.shape[2]
    per_page = page // SEL_BLOCK

    def kv_spec(s, row_blk0):
        def imap(b, g, pt, ix):
            blk = ix[(b * B_KV_HEADS + g) * nslot + s]
            return (pt[b, blk // per_page], row_blk0 + g, 0)
        return pl.BlockSpec((1, HEAD_DIM, page), imap)

    row = lambda a: pl.BlockSpec((1,) + a.shape[1:], lambda b, g, pt, ix: (b,) + (0,) * (a.ndim - 1))
    new = pl.BlockSpec((1, 1, 1, HEAD_DIM), lambda b, g, pt, ix: (b, g, 0, 0))
    return pl.pallas_call(
        functools.partial(_nsa_sample_attend_kernel, nslot=nslot, per_page=per_page),
        out_shape=jax.ShapeDtypeStruct((db, B_HEADS, HEAD_DIM), F32),
        grid_spec=pltpu.PrefetchScalarGridSpec(
            num_scalar_prefetch=2, grid=(db, B_KV_HEADS),
            in_specs=[row(qbr_rows)] + [kv_spec(s, 2 * B_KV_HEADS) for s in range(nslot)]
                     + [kv_spec(s, 3 * B_KV_HEADS) for s in range(nslot)] + [new, new, row(gates), row(ocw)],
            out_specs=pl.BlockSpec((1, B_HEADS, HEAD_DIM), lambda b, g, pt, ix: (b, 0, 0))),
        compiler_params=_cparams("parallel", "arbitrary"),
        name="nsa_sample_attend",
    )(page_table, idx, qbr_rows, *([cache_t] * (2 * nslot)), ks_new, vs_new, gates, ocw)


def _prep_weights(l, g_mix_pre, w_in, b_gate, cmp_pe, cmp_w1, cmp_b1, cmp_w2, w_out, g_mix_post,
                  g_ffn_pre, w_up, w_conv, b_conv, w_down, g_ffn_post):
    d_in = w_in.shape[-1]
    n_gate = b_gate.shape[-1]
    return dict(
        g_mix_pre=g_mix_pre[l][None],
        w_in=jnp.pad(w_in[l], ((0, 0), (0, D_IN_PAD - d_in))).astype(BF16),
        b_gate=jnp.pad(b_gate[l], (0, LANES - n_gate))[None],
        cmp=_compress_weights(cmp_pe[l], cmp_w1[l], cmp_b1[l], cmp_w2[l]),
        w_out=w_out[l].astype(BF16), g_mix_post=g_mix_post[l][None], g_ffn_pre=g_ffn_pre[l][None],
        w_up=w_up[l].astype(BF16), w_conv=w_conv[l], b_conv=b_conv[l][None],
        w_down=w_down[l].astype(BF16), g_ffn_post=g_ffn_post[l][None],
    )


def _group_gates(gates):
    t = gates.shape[0]
    return gates[:, :3 * B_HEADS].reshape(t, B_KV_HEADS, 3 * B_GROUP).transpose(1, 0, 2)


FFN_CK = 256
FFN_TM = 1024


def _prompt_layer(x2d, w):
    t = x2d.shape[0]
    (kva, nsa, win, gates, qa, kaug, va, qb, qbr, ksaug, vs, kw, vw, kcvc, kmean) = _inproj(
        x2d, jnp.arange(t), w["g_mix_pre"], w["w_in"], w["b_gate"], tm=512, with_kmean=True)
    o_a = _moba_prompt(qa, _gate_matrix(kmean.reshape(t // A_BLOCK, A_KV_HEADS * HEAD_DIM)), kaug, va)
    kvcmp = _compress(kcvc.reshape(4, t // CMP_STRIDE, CMP_STRIDE * HEAD_DIM), w["cmp"])
    o_b = _nsa_prompt(qb, qbr, kvcmp, ksaug, vs, kw, vw, _group_gates(gates))
    x1, h2 = _outproj(o_a, o_b, x2d, w["w_out"], w["g_mix_post"], w["g_ffn_pre"], tm=512)
    y, conv_state = _ffn_seq(h2, x1, w["w_up"], w["w_conv"], w["b_conv"], w["w_down"], w["g_ffn_post"],
                             tm=FFN_TM, ck=FFN_CK)
    keep = min(WINDOW, t)
    return (y, kva.reshape(t, 2, A_KV_HEADS, HEAD_DIM), nsa.reshape(t, 4, B_KV_HEADS, HEAD_DIM),
            win[t - keep:].reshape(keep, 2, B_KV_HEADS, HEAD_DIM), conv_state)


PAGES_PER_STEP = 16


def _sample_layer(x2d, cache_moba, cache_nsa, win_cache, conv_state, page_table, w):
    db = x2d.shape[0]
    n_pool, page = cache_moba.shape[:2]
    npg = page_table.shape[1]
    past = npg * page
    wb = win_cache.shape[1]
    assert wb == WINDOW and past % A_BLOCK == 0 and past // A_BLOCK >= A_TOPK and A_BLOCK % page == 0
    assert page % SEL_BLOCK == 0
    pg = min(PAGES_PER_STEP, npg)
    (kva, nsa, win, gates, qa, _, _, qb, qbr, _, _, _, _, _, _) = _inproj(
        x2d, jnp.full((db,), past, jnp.int32), w["g_mix_pre"], w["w_in"], w["b_gate"], tm=db, with_kmean=False)
    rows = lambda a: a.transpose(1, 0, 2).astype(F32)
    qa_rows, qb_rows, qbr_rows = rows(qa), rows(qb), rows(qbr)

    tpose = lambda c: c.transpose(0, 2, 3, 4, 1).reshape(c.shape[0], -1, c.shape[1])
    moba_t, nsa_t, win_t = tpose(cache_moba), tpose(cache_nsa), tpose(win_cache)

    kmean_t = _moba_kmean(moba_t, page_table, pg=pg)
    idx_a = _moba_gate(qa_rows, kmean_t, nblk=past // A_BLOCK)[:, :, :A_TOPK].reshape(-1)
    rep = A_HEADS // A_KV_HEADS
    per_head = lambda a: jnp.repeat(a.reshape(db, A_KV_HEADS, HEAD_DIM), rep, axis=1)
    o_a = _moba_sample(moba_t, page_table, idx_a, qa_rows, per_head(kva[:, :256]), per_head(kva[:, 256:]))

    kvcmp = _nsa_sample_compress(nsa_t, page_table, w["cmp"], pg=pg)
    gates3 = gates[:, None, :]
    win2d = win_cache.reshape(db, wb, 2 * B_KV_HEADS * HEAD_DIM)
    win_new = win[:, None, :]
    idx_b, ocw = _nsa_sample_select(qb_rows, qbr_rows, kvcmp, win_t, win_new, gates3, past=past)
    nslot = min(SEL_TOPN, past // SEL_BLOCK + 1) - 1
    per_group = lambda a: a.reshape(db, B_KV_HEADS, 1, HEAD_DIM)
    o_b = _nsa_sample_attend(nsa_t, page_table, idx_b[:, :B_KV_HEADS, :nslot].reshape(-1), qbr_rows,
                             per_group(nsa[:, 256:384]), per_group(nsa[:, 384:]), gates3, ocw, nslot=nslot)

    x1, h2 = _outproj(o_a.reshape(db, -1), o_b.reshape(db, -1), x2d, w["w_out"], w["g_mix_post"],
                      w["g_ffn_pre"], tm=db)
    y, conv_new = _ffn_step(h2, x1, conv_state, w["w_up"], w["w_conv"], w["b_conv"], w["w_down"],
                            w["g_ffn_post"], ck=FFN_CK)
    keep = min(WINDOW, wb + 1)
    win_all = jnp.concatenate([win2d, win_new], axis=1)[:, wb + 1 - keep:]
    return (y, kva.reshape(db, 2, A_KV_HEADS, HEAD_DIM), nsa.reshape(db, 4, B_KV_HEADS, HEAD_DIM),
            win_all.reshape(db, keep, 2, B_KV_HEADS, HEAD_DIM), conv_new)


def kernel(x_prompt, x_sample, cache_moba_kv, cache_nsa_kv, cache_nsa_win_kv, state_ffn_conv, page_table,
           g_mix_pre, w_in, b_gate, cmp_pe, cmp_w1, cmp_b1, cmp_w2, w_out, g_mix_post, g_ffn_pre, w_up,
           w_conv, b_conv, w_down, g_ffn_post):
    depth = w_in.shape[0]
    assert depth == 1 and x_prompt.shape[0] == 1
    w = _prep_weights(0, g_mix_pre, w_in, b_gate, cmp_pe, cmp_w1, cmp_b1, cmp_w2, w_out, g_mix_post,
                      g_ffn_pre, w_up, w_conv, b_conv, w_down, g_ffn_post)
    y_p, moba_p, nsa_p, win_p, conv_p = _prompt_layer(x_prompt[0], w)
    assert x_sample.shape[1] == 1
    y_s, moba_s, nsa_s, win_s, conv_s = _sample_layer(
        x_sample[:, 0], cache_moba_kv[0], cache_nsa_kv[0], cache_nsa_win_kv[0], state_ffn_conv[0],
        page_table, w)
    return (y_p[None], y_s[:, None], moba_p[None, None], moba_s[None, :, None],
            nsa_p[None, None], nsa_s[None, :, None], win_p[None, None], win_s[None],
            conv_p[None, None], conv_s[None])
```

```python
import functools
import math

import numpy as np
import jax
import jax.numpy as jnp
from jax import lax
from jax.experimental import pallas as pl
from jax.experimental.pallas import tpu as pltpu

HEAD_DIM = 64
ROPE_DIM = HEAD_DIM // 4
ROPE_THETA = 500000.0
A_HEADS = 8
A_KV_HEADS = 4
A_BLOCK = 256
A_TOPK = 3
B_HEADS = 8
B_KV_HEADS = 2
B_GROUP = B_HEADS // B_KV_HEADS
CMP_LEN = 32
CMP_STRIDE = 16
CMP_HIDDEN = 128
SEL_BLOCK = 64
SEL_TOPN = 16
WINDOW = 512
CONV_W = 3
RMS_EPS = 1e-6
SCALE = HEAD_DIM ** -0.5

LANES = 128
VMEM_LIMIT = 56 * 1024 * 1024
MASK_BIAS = -32768.0
NEG_INF = float("-inf")

BF16 = jnp.bfloat16
F32 = jnp.float32


def _cparams(*sem):
    return pltpu.CompilerParams(dimension_semantics=sem, vmem_limit_bytes=VMEM_LIMIT)


def _dot(a, b):
    return jnp.dot(a, b, preferred_element_type=F32)


def _dot_t(a, b):
    return lax.dot_general(a, b, (((1,), (1,)), ((), ())), preferred_element_type=F32)


def _rms(x, g):
    y = x * lax.rsqrt(jnp.mean(x * x, axis=-1, keepdims=True) + RMS_EPS)
    return y * g


def _gelu(x):
    c = math.sqrt(2.0 / math.pi)
    return 0.5 * x * (1.0 + jnp.tanh(c * (x + 0.044715 * (x * x * x))))


def _rope128(x, c, sa, sb):
    return x * c + pltpu.roll(x, LANES - ROPE_DIM // 2, 1) * sa + pltpu.roll(x, ROPE_DIM // 2, 1) * sb


C_QA, C_KA, C_VA, C_QB = 0, 512, 768, 1024
C_KC, C_VC, C_KS, C_VS, C_KW, C_VW, C_G = 1536, 1664, 1792, 1920, 2048, 2176, 2304
D_IN_PAD = 2432


def _inproj_kernel(x_ref, g_ref, w_ref, bg_ref, c_ref, sa_ref, sb_ref,
                   kva_ref, nsa_ref, win_ref, gates_ref,
                   qa_ref, kaug_ref, va_ref, qb_ref, qbr_ref, ksaug_ref, vs_ref, kw_ref, vw_ref,
                   kcvc_ref, kmean_ref, *, tm, with_kmean):
    i = pl.program_id(0)
    h = _rms(x_ref[...], g_ref[...]).astype(BF16)
    c, sa, sb = c_ref[...], sa_ref[...], sb_ref[...]

    def proj(c0, width):
        return _dot(h, w_ref[:, c0:c0 + width])

    lane = lax.broadcasted_iota(jnp.int32, (tm, LANES), 1)
    row = lax.broadcasted_iota(jnp.int32, (tm, LANES), 0) + i * tm
    zeros64 = jnp.zeros((tm, HEAD_DIM), BF16)

    for p in range(4):
        q = _rope128(proj(C_QA + p * LANES, LANES), c, sa, sb)
        qa_ref[2 * p] = jnp.concatenate([q[:, :HEAD_DIM].astype(BF16), zeros64], axis=1)
        qa_ref[2 * p + 1] = jnp.concatenate([q[:, HEAD_DIM:].astype(BF16), zeros64], axis=1)

    a_onehot = (lane - HEAD_DIM == row // A_BLOCK).astype(BF16)
    ksum = []
    for p in range(2):
        k = _rope128(proj(C_KA + p * LANES, LANES), c, sa, sb)
        v = proj(C_VA + p * LANES, LANES)
        kva_ref[:, p * LANES:(p + 1) * LANES] = k
        kva_ref[:, 256 + p * LANES:256 + (p + 1) * LANES] = v
        kb = k.astype(BF16)
        vb = v.astype(BF16)
        kaug_ref[2 * p] = jnp.concatenate([kb[:, :HEAD_DIM], a_onehot[:, HEAD_DIM:]], axis=1)
        kaug_ref[2 * p + 1] = jnp.concatenate([kb[:, HEAD_DIM:], a_onehot[:, HEAD_DIM:]], axis=1)
        va_ref[2 * p] = vb[:, :HEAD_DIM]
        va_ref[2 * p + 1] = vb[:, HEAD_DIM:]
        if with_kmean:
            ksum.append(jnp.sum(k.reshape(tm // A_BLOCK, A_BLOCK, LANES), axis=1))
    if with_kmean:
        kmean_ref[0] = jnp.concatenate(ksum, axis=1) * (1.0 / A_BLOCK)
    else:
        kmean_ref[...] = jnp.zeros(kmean_ref.shape, F32)

    for p in range(4):
        q = proj(C_QB + p * LANES, LANES) * SCALE
        qr = _rope128(q, c, sa, sb)
        qb_ref[2 * p] = q[:, :HEAD_DIM].astype(BF16)
        qb_ref[2 * p + 1] = q[:, HEAD_DIM:].astype(BF16)
        qbr_ref[2 * p] = jnp.concatenate([qr[:, :HEAD_DIM].astype(BF16), zeros64], axis=1)
        qbr_ref[2 * p + 1] = jnp.concatenate([qr[:, HEAD_DIM:].astype(BF16), zeros64], axis=1)

    kc = proj(C_KC, LANES)
    vc = proj(C_VC, LANES)
    ks = _rope128(proj(C_KS, LANES), c, sa, sb)
    vs = proj(C_VS, LANES)
    nsa_ref[:, 0:128] = kc
    nsa_ref[:, 128:256] = vc
    nsa_ref[:, 256:384] = ks
    nsa_ref[:, 384:512] = vs
    kcvc_ref[0] = kc[:, :HEAD_DIM]
    kcvc_ref[1] = kc[:, HEAD_DIM:]
    kcvc_ref[2] = vc[:, :HEAD_DIM]
    kcvc_ref[3] = vc[:, HEAD_DIM:]
    s_onehot = (lane - HEAD_DIM == (row // SEL_BLOCK) % HEAD_DIM).astype(BF16)
    ksb = ks.astype(BF16)
    vsb = vs.astype(BF16)
    ksaug_ref[0] = jnp.concatenate([ksb[:, :HEAD_DIM], s_onehot[:, HEAD_DIM:]], axis=1)
    ksaug_ref[1] = jnp.concatenate([ksb[:, HEAD_DIM:], s_onehot[:, HEAD_DIM:]], axis=1)
    vs_ref[0] = vsb[:, :HEAD_DIM]
    vs_ref[1] = vsb[:, HEAD_DIM:]

    kw = _rope128(proj(C_KW, LANES), c, sa, sb)
    vw = proj(C_VW, LANES)
    win_ref[:, 0:128] = kw
    win_ref[:, 128:256] = vw
    kwb = kw.astype(BF16)
    vwb = vw.astype(BF16)
    kw_ref[0] = kwb[:, :HEAD_DIM]
    kw_ref[1] = kwb[:, HEAD_DIM:]
    vw_ref[0] = vwb[:, :HEAD_DIM]
    vw_ref[1] = vwb[:, HEAD_DIM:]

    gates_ref[...] = jax.nn.sigmoid(proj(C_G, LANES) + bg_ref[...])


def _rope_tables(pos):
    half = ROPE_DIM // 2
    inv = ROPE_THETA ** (-2.0 * jnp.arange(half, dtype=F32) / ROPE_DIM)
    ang = pos.astype(F32)[:, None] * inv[None, :]
    cos, sin = jnp.cos(ang), jnp.sin(ang)
    t = pos.shape[0]
    ones = jnp.ones((t, HEAD_DIM - ROPE_DIM), F32)
    zeros = jnp.zeros((t, HEAD_DIM - ROPE_DIM), F32)
    zh = jnp.zeros((t, half), F32)
    c = jnp.concatenate([cos, cos, ones], axis=1)
    sa = jnp.concatenate([-sin, zh, zeros], axis=1)
    sb = jnp.concatenate([zh, sin, zeros], axis=1)
    return tuple(jnp.concatenate([a, a], axis=1) for a in (c, sa, sb))


def _inproj(x2d, pos, g, w_pad, bg_pad, *, tm, with_kmean):
    t, d = x2d.shape
    nt = t // tm
    c, sa, sb = _rope_tables(pos)
    row_spec = lambda w: pl.BlockSpec((tm, w), lambda i: (i, 0))
    head_spec = lambda n, w: pl.BlockSpec((n, tm, w), lambda i: (0, i, 0))
    full = lambda a: pl.BlockSpec(a.shape, lambda i: (0,) * a.ndim)
    nkm = max(tm // A_BLOCK, 1)
    out_shape = (
        jax.ShapeDtypeStruct((t, 512), F32),
        jax.ShapeDtypeStruct((t, 512), F32),
        jax.ShapeDtypeStruct((t, 256), F32),
        jax.ShapeDtypeStruct((t, LANES), F32),
        jax.ShapeDtypeStruct((A_HEADS, t, LANES), BF16),
        jax.ShapeDtypeStruct((A_KV_HEADS, t, LANES), BF16),
        jax.ShapeDtypeStruct((A_KV_HEADS, t, HEAD_DIM), BF16),
        jax.ShapeDtypeStruct((B_HEADS, t, HEAD_DIM), BF16),
        jax.ShapeDtypeStruct((B_HEADS, t, LANES), BF16),
        jax.ShapeDtypeStruct((B_KV_HEADS, t, LANES), BF16),
        jax.ShapeDtypeStruct((B_KV_HEADS, t, HEAD_DIM), BF16),
        jax.ShapeDtypeStruct((B_KV_HEADS, t, HEAD_DIM), BF16),
        jax.ShapeDtypeStruct((B_KV_HEADS, t, HEAD_DIM), BF16),
        jax.ShapeDtypeStruct((4, t, HEAD_DIM), F32),
        jax.ShapeDtypeStruct((nt, nkm, 256), F32),
    )
    out_specs = (
        row_spec(512), row_spec(512), row_spec(256), row_spec(LANES),
        head_spec(A_HEADS, LANES), head_spec(A_KV_HEADS, LANES), head_spec(A_KV_HEADS, HEAD_DIM),
        head_spec(B_HEADS, HEAD_DIM), head_spec(B_HEADS, LANES), head_spec(B_KV_HEADS, LANES),
        head_spec(B_KV_HEADS, HEAD_DIM), head_spec(B_KV_HEADS, HEAD_DIM), head_spec(B_KV_HEADS, HEAD_DIM),
        head_spec(4, HEAD_DIM),
        pl.BlockSpec((1, nkm, 256), lambda i: (i, 0, 0)),
    )
    return pl.pallas_call(
        functools.partial(_inproj_kernel, tm=tm, with_kmean=with_kmean),
        out_shape=out_shape,
        grid=(nt,),
        in_specs=[row_spec(d), full(g), full(w_pad), full(bg_pad),
                  row_spec(LANES), row_spec(LANES), row_spec(LANES)],
        out_specs=out_specs,
        compiler_params=_cparams("parallel"),
        name="inproj",
    )(x2d, g, w_pad, bg_pad, c, sa, sb)


def _outproj_kernel(oa_ref, ob_ref, x_ref, wa_ref, wb_ref, gpost_ref, gpre_ref, x1_ref, h2_ref):
    mix = _dot(oa_ref[...].astype(BF16), wa_ref[...]) + _dot(ob_ref[...].astype(BF16), wb_ref[...])
    x1 = x_ref[...] + _rms(mix, gpost_ref[...])
    x1_ref[...] = x1
    h2_ref[...] = _rms(x1, gpre_ref[...]).astype(BF16)


def _outproj(oa, ob, x2d, w_out_bf, g_post, g_pre, *, tm):
    t, d = x2d.shape
    half = oa.shape[1]
    row = lambda w: pl.BlockSpec((tm, w), lambda i: (i, 0))
    full = lambda a: pl.BlockSpec(a.shape, lambda i: (0,) * a.ndim)
    wa, wb = w_out_bf[:half], w_out_bf[half:]
    return pl.pallas_call(
        _outproj_kernel,
        out_shape=(jax.ShapeDtypeStruct((t, d), F32), jax.ShapeDtypeStruct((t, d), BF16)),
        grid=(t // tm,),
        in_specs=[row(half), row(half), row(d), full(wa), full(wb), full(g_post), full(g_pre)],
        out_specs=(row(d), row(d)),
        compiler_params=_cparams("parallel"),
        name="outproj",
    )(oa, ob, x2d, wa, wb, g_post, g_pre)


HALO = 8


def _ffn_seq_kernel(h_ref, halo_ref, x1_ref, wg_ref, wv_ref, cg_ref, cv_ref, bg_ref, bv_ref,
                    wd_ref, gpost_ref, y_ref, tailg_ref, tailv_ref, acc_ref, ug_ref, uv_ref, *, tm):
    i, c = pl.program_id(0), pl.program_id(1)

    @pl.when(c == 0)
    def _():
        acc_ref[...] = jnp.zeros(acc_ref.shape, F32)

    keep = (i > 0).astype(F32)

    def conv(w_ref, u_ref, cw_ref, cb_ref, tail_ref):
        u_ref[0:HALO] = _dot(halo_ref[...], w_ref[...]) * keep
        u_ref[HALO:HALO + tm] = _dot(h_ref[...], w_ref[...])
        tail_ref[0] = u_ref[tm:tm + HALO]
        cw = cw_ref[...]
        return (u_ref[pl.ds(HALO - 2, tm), :] * cw[0:1] + u_ref[pl.ds(HALO - 1, tm), :] * cw[1:2]
                + u_ref[pl.ds(HALO, tm), :] * cw[2:3] + cb_ref[...])

    gate = conv(wg_ref, ug_ref, cg_ref, bg_ref, tailg_ref)
    val = conv(wv_ref, uv_ref, cv_ref, bv_ref, tailv_ref)
    acc_ref[...] += _dot((_gelu(gate) * val).astype(BF16), wd_ref[...])

    @pl.when(c == pl.num_programs(1) - 1)
    def _():
        y_ref[...] = x1_ref[...] + _rms(acc_ref[...], gpost_ref[...])


def _ffn_step_kernel(h_ref, p0g_ref, p1g_ref, p0v_ref, p1v_ref, x1_ref, wg_ref, wv_ref, cg_ref, cv_ref,
                     bg_ref, bv_ref, wd_ref, gpost_ref, y_ref, upg_ref, upv_ref, acc_ref):
    c = pl.program_id(0)

    @pl.when(c == 0)
    def _():
        acc_ref[...] = jnp.zeros(acc_ref.shape, F32)

    def conv(w_ref, p0_ref, p1_ref, cw_ref, cb_ref, up_ref):
        u = _dot(h_ref[...], w_ref[...])
        up_ref[...] = u
        cw = cw_ref[...]
        return p0_ref[...] * cw[0:1] + p1_ref[...] * cw[1:2] + u * cw[2:3] + cb_ref[...]

    gate = conv(wg_ref, p0g_ref, p1g_ref, cg_ref, bg_ref, upg_ref)
    val = conv(wv_ref, p0v_ref, p1v_ref, cv_ref, bv_ref, upv_ref)
    acc_ref[...] += _dot((_gelu(gate) * val).astype(BF16), wd_ref[...])

    @pl.when(c == pl.num_programs(0) - 1)
    def _():
        y_ref[...] = x1_ref[...] + _rms(acc_ref[...], gpost_ref[...])


def _ffn_seq(h2, x1, w_up_bf, w_conv, b_conv2d, w_down_bf, g_post, *, tm, ck):
    t, d = x1.shape
    dff = w_down_bf.shape[0]
    nff = dff // ck
    nt = t // tm
    hb = tm // HALO
    gcol = lambda r: pl.BlockSpec((r, ck), lambda i, c: (0, c))
    vcol = lambda r: pl.BlockSpec((r, ck), lambda i, c: (0, nff + c))
    row = pl.BlockSpec((tm, d), lambda i, c: (i, 0))
    tail = pl.BlockSpec((1, HALO, ck), lambda i, c: (i, 0, c))
    y, tg, tv = pl.pallas_call(
        functools.partial(_ffn_seq_kernel, tm=tm),
        out_shape=(jax.ShapeDtypeStruct((t, d), F32),
                   jax.ShapeDtypeStruct((nt, HALO, dff), F32), jax.ShapeDtypeStruct((nt, HALO, dff), F32)),
        grid=(nt, nff),
        in_specs=[row, pl.BlockSpec((HALO, d), lambda i, c: (jnp.maximum(i * hb - 1, 0), 0)), row,
                  gcol(d), vcol(d), gcol(CONV_W), vcol(CONV_W), gcol(1), vcol(1),
                  pl.BlockSpec((ck, d), lambda i, c: (c, 0)),
                  pl.BlockSpec(g_post.shape, lambda i, c: (0, 0))],
        out_specs=(row, tail, tail),
        scratch_shapes=[pltpu.VMEM((tm, d), F32), pltpu.VMEM((tm + HALO, ck), F32),
                        pltpu.VMEM((tm + HALO, ck), F32)],
        compiler_params=_cparams("parallel", "arbitrary"),
        name="ffn_seq",
    )(h2, h2, x1, w_up_bf, w_up_bf, w_conv, w_conv, b_conv2d, b_conv2d, w_down_bf, g_post)
    state = jnp.concatenate([tg[-1, HALO - 2:], tv[-1, HALO - 2:]], axis=1)
    return y, state


def _ffn_step(h2, x1, prev, w_up_bf, w_conv, b_conv2d, w_down_bf, g_post, *, ck):
    t, d = x1.shape
    dff = w_down_bf.shape[0]
    nff = dff // ck
    p0, p1 = prev[:, 0], prev[:, 1]
    gcol = lambda r: pl.BlockSpec((r, ck), lambda c: (0, c))
    vcol = lambda r: pl.BlockSpec((r, ck), lambda c: (0, nff + c))
    row = pl.BlockSpec((t, d), lambda c: (0, 0))
    y, ug, uv = pl.pallas_call(
        _ffn_step_kernel,
        out_shape=(jax.ShapeDtypeStruct((t, d), F32),
                   jax.ShapeDtypeStruct((t, dff), F32), jax.ShapeDtypeStruct((t, dff), F32)),
        grid=(nff,),
        in_specs=[row, gcol(t), gcol(t), vcol(t), vcol(t), row,
                  gcol(d), vcol(d), gcol(CONV_W), vcol(CONV_W), gcol(1), vcol(1),
                  pl.BlockSpec((ck, d), lambda c: (c, 0)),
                  pl.BlockSpec(g_post.shape, lambda c: (0, 0))],
        out_specs=(row, pl.BlockSpec((t, ck), lambda c: (0, c)), pl.BlockSpec((t, ck), lambda c: (0, c))),
        scratch_shapes=[pltpu.VMEM((t, d), F32)],
        compiler_params=_cparams("arbitrary"),
        name="ffn_step",
    )(h2, p0, p1, p0, p1, x1, w_up_bf, w_up_bf, w_conv, w_conv, b_conv2d, b_conv2d, w_down_bf, g_post)
    state = jnp.stack([p1, jnp.concatenate([ug, uv], axis=1)], axis=1)
    return y, state


ATT_KT = 512
MOBA_TQ = ATT_KT


def _softmax_init(m_sc, l_sc, acc_sc):
    m_sc[...] = jnp.full(m_sc.shape, NEG_INF, F32)
    l_sc[...] = jnp.zeros(l_sc.shape, F32)
    acc_sc[...] = jnp.zeros(acc_sc.shape, F32)


def _two_pass_attention(q3_ref, k_ref, v_ref, n_full, tiles_per_win, qpos, m_sc, l_sc, acc_sc, s_sc):
    nwin, rows, _ = q3_ref.shape
    kt = s_sc.shape[2]
    groups = [slice(j * LANES, (j + 1) * LANES) for j in range(kt // LANES)]

    def scores(t):
        off = pl.multiple_of(t * kt, kt)
        w = t // tiles_per_win if nwin > 1 else 0
        return _dot_t(q3_ref[w], k_ref[0, pl.ds(off, kt), :])

    def causal(s):
        kpos = n_full * kt + lax.broadcasted_iota(jnp.int32, (rows, kt), 1)
        return jnp.where(kpos <= qpos, s, NEG_INF)

    def take_max(s):
        mx = s[:, groups[0]]
        for gs in groups[1:]:
            mx = jnp.maximum(mx, s[:, gs])
        m_sc[...] = jnp.maximum(m_sc[...], mx)

    def consume(t, s):
        m = m_sc[...]
        ps = [jnp.exp(s[:, gs] - m) for gs in groups]
        tot = ps[0]
        for p in ps[1:]:
            tot = tot + p
        l_sc[...] += tot
        off = pl.multiple_of(t * kt, kt)
        acc_sc[...] += _dot(jnp.concatenate(ps, axis=1).astype(BF16), v_ref[0, pl.ds(off, kt), :])

    _softmax_init(m_sc, l_sc, acc_sc)

    def max_body(t, carry):
        take_max(scores(t))
        return carry

    lax.fori_loop(0, n_full, max_body, 0)
    take_max(causal(scores(n_full)))
    m_sc[...] = jnp.broadcast_to(jnp.max(m_sc[...], axis=1, keepdims=True), m_sc.shape)

    s_sc[0] = scores(0)

    def sum_body(t, carry):
        s = s_sc[t % 2]
        s_sc[(t + 1) % 2] = scores(t + 1)
        consume(t, s)
        return carry

    lax.fori_loop(0, n_full, sum_body, 0)
    consume(n_full, causal(s_sc[n_full % 2]))
    return acc_sc[...] / jnp.sum(l_sc[...], axis=1, keepdims=True)


def _top_select(v, lane, forced, rounds):
    sel = forced
    lane = lane.astype(F32)
    for _ in range(rounds):
        mx = jnp.max(v, axis=1, keepdims=True)
        idx = jnp.min(jnp.where(v == mx, lane, 1e9), axis=1, keepdims=True)
        pick = (lane == idx) & (mx > NEG_INF)
        sel = sel | pick
        v = jnp.where(pick, NEG_INF, v)
    return sel


def _masked_softmax(s, mask):
    s = jnp.where(mask, s, NEG_INF)
    m = jnp.max(s, axis=1, keepdims=True)
    m = jnp.where(m > NEG_INF, m, 0.0)
    e = jnp.where(mask, jnp.exp(s - m), 0.0)
    d = jnp.sum(e, axis=1, keepdims=True)
    return e / jnp.where(d > 0, d, 1.0)


def _moba_prompt_kernel(q_ref, km_ref, k_ref, v_ref, o_ref, m_sc, l_sc, acc_sc, q3_sc, s_sc):
    i = pl.program_id(1)
    rows = 2 * MOBA_TQ
    q = q_ref[...].reshape(rows, LANES)
    gate = _dot(q, km_ref[0])
    lane = lax.broadcasted_iota(jnp.int32, (rows, LANES), 1)
    qpos = i * MOBA_TQ + lax.broadcasted_iota(jnp.int32, (rows, 1), 0) % MOBA_TQ
    cur = qpos // A_BLOCK
    past = (lane >= HEAD_DIM) & (lane < HEAD_DIM + cur)
    sel = _top_select(jnp.where(past, gate, NEG_INF), lane, lane == HEAD_DIM + cur, A_TOPK)
    bias = jnp.where(sel | (lane < HEAD_DIM), 0.0, MASK_BIAS)
    q3_sc[0] = (q.astype(F32) * SCALE + bias).astype(BF16)

    o = _two_pass_attention(q3_sc, k_ref, v_ref, i, None, qpos, m_sc, l_sc, acc_sc, s_sc)
    o_ref[:, 0:HEAD_DIM] = o[:MOBA_TQ]
    o_ref[:, HEAD_DIM:LANES] = o[MOBA_TQ:]


def _gate_matrix(kmean):
    nblk = kmean.shape[0]
    km = kmean.reshape(nblk, A_KV_HEADS, HEAD_DIM).transpose(1, 2, 0)
    km = jnp.pad(km, ((0, 0), (0, LANES - HEAD_DIM), (HEAD_DIM, LANES - HEAD_DIM - nblk)))
    return km.astype(BF16)


def _moba_prompt(qa, km, kaug, va):
    t = qa.shape[1]
    nt = t // MOBA_TQ
    assert t // A_BLOCK <= HEAD_DIM, "key-block one-hot occupies 64 lanes"
    assert t % MOBA_TQ == 0 and MOBA_TQ % A_BLOCK == 0
    rows = 2 * MOBA_TQ
    return pl.pallas_call(
        _moba_prompt_kernel,
        out_shape=jax.ShapeDtypeStruct((t, A_HEADS * HEAD_DIM), F32),
        grid=(A_KV_HEADS, nt),
        in_specs=[pl.BlockSpec((2, MOBA_TQ, LANES), lambda g, i: (g, i, 0)),
                  pl.BlockSpec((1, LANES, LANES), lambda g, i: (g, 0, 0)),
                  pl.BlockSpec((1, t, LANES), lambda g, i: (g, 0, 0)),
                  pl.BlockSpec((1, t, HEAD_DIM), lambda g, i: (g, 0, 0))],
        out_specs=pl.BlockSpec((MOBA_TQ, LANES), lambda g, i: (i, g)),
        scratch_shapes=[pltpu.VMEM((rows, LANES), F32), pltpu.VMEM((rows, LANES), F32),
                        pltpu.VMEM((rows, HEAD_DIM), F32), pltpu.VMEM((1, rows, LANES), BF16),
                        pltpu.VMEM((2, rows, ATT_KT), F32)],
        compiler_params=_cparams("parallel", "parallel"),
        name="moba_prompt",
    )(qa, km, kaug, va)


def _compress_kernel(x_ref, pelo_ref, pehi_ref, wlo_ref, whi_ref, b1_ref, w2_ref, o_ref, hi_sc, *, nch):
    x = x_ref[0]
    lo = _dot((x + pelo_ref[0]).astype(BF16), wlo_ref[0])
    hi_sc[0:nch] = _dot((x + pehi_ref[0]).astype(BF16), whi_ref[0])
    hi_sc[nch:nch + 8] = jnp.zeros((8, CMP_HIDDEN), F32)
    hid = _gelu(lo + hi_sc[pl.ds(1, nch), :] + b1_ref[0])
    o_ref[0] = _dot(hid.astype(BF16), w2_ref[0]).astype(o_ref.dtype)


def _compress_weights(cmp_pe, cmp_w1, cmp_b1, cmp_w2):
    flat = CMP_STRIDE * HEAD_DIM
    pelo = cmp_pe[:, :CMP_STRIDE].reshape(2, 1, flat)
    pehi = cmp_pe[:, CMP_STRIDE:].reshape(2, 1, flat)
    wlo = cmp_w1[:, :CMP_STRIDE].reshape(2, flat, CMP_HIDDEN).astype(BF16)
    whi = cmp_w1[:, CMP_STRIDE:].reshape(2, flat, CMP_HIDDEN).astype(BF16)
    return pelo, pehi, wlo, whi, cmp_b1.reshape(2, 1, CMP_HIDDEN), cmp_w2.astype(BF16)


def _compress(xch, cw):
    _, nch, flat = xch.shape
    pelo, pehi, wlo, whi, b1, w2 = cw
    kind = lambda shape: pl.BlockSpec((1,) + shape, lambda j: (j // B_KV_HEADS, 0, 0))
    return pl.pallas_call(
        functools.partial(_compress_kernel, nch=nch),
        out_shape=jax.ShapeDtypeStruct((4, nch, HEAD_DIM), BF16),
        grid=(4,),
        in_specs=[pl.BlockSpec((1, nch, flat), lambda j: (j, 0, 0)),
                  kind((1, flat)), kind((1, flat)), kind((flat, CMP_HIDDEN)), kind((flat, CMP_HIDDEN)),
                  kind((1, CMP_HIDDEN)), kind((CMP_HIDDEN, HEAD_DIM))],
        out_specs=pl.BlockSpec((1, nch, HEAD_DIM), lambda j: (j, 0, 0)),
        scratch_shapes=[pltpu.VMEM((nch + 8, CMP_HIDDEN), F32)],
        compiler_params=_cparams("parallel"),
        name="compress",
    )(xch, pelo, pehi, wlo, whi, b1, w2)


def _importance_matrix(ncmp, nsel):
    ratio = SEL_BLOCK // CMP_STRIDE
    n = np.arange(ncmp)[:, None]
    j = np.arange(nsel)[None, :]
    own = (n // ratio == j)
    last = (n % ratio == ratio - 1)
    m = np.where(own & ~last, 1.0, 0.0) + np.where(last & (own | (n // ratio == j - 1)), 0.5, 0.0)
    return jnp.asarray(m, BF16)


NSA_TQ = 256
WIN_BLOCKS = WINDOW // NSA_TQ + 1


def _nsa_prompt_kernel(*refs, nsel, ncmp):
    (qb_ref, qbr_ref, kc_ref, vc_ref, imp_ref, ks_ref, vs_ref) = refs[:7]
    kw_refs = refs[7:7 + WIN_BLOCKS]
    vw_refs = refs[7 + WIN_BLOCKS:7 + 2 * WIN_BLOCKS]
    g_ref, o_ref, m_sc, l_sc, acc_sc, q3_sc, s_sc = refs[7 + 2 * WIN_BLOCKS:]
    i = pl.program_id(1)
    rows = B_GROUP * NSA_TQ
    q0 = i * NSA_TQ
    q = qb_ref[...].reshape(rows, HEAD_DIM)
    qr = qbr_ref[...].reshape(rows, LANES)

    s = _dot_t(q, kc_ref[0])
    qpos_c = q0 + lax.broadcasted_iota(jnp.int32, (rows, ncmp), 0) % NSA_TQ
    n_c = lax.broadcasted_iota(jnp.int32, (rows, ncmp), 1)
    p = _masked_softmax(s, n_c * CMP_STRIDE + (CMP_LEN - 1) <= qpos_c)
    o_cmp = _dot(p.astype(BF16), vc_ref[0])

    psum = p[0:NSA_TQ]
    for j in range(1, B_GROUP):
        psum = psum + p[j * NSA_TQ:(j + 1) * NSA_TQ]
    p_hi = psum.astype(BF16)
    p_lo = (psum - p_hi.astype(F32)).astype(BF16)
    imp = _dot(p_hi, imp_ref[...]) + _dot(p_lo, imp_ref[...])
    blk = lax.broadcasted_iota(jnp.int32, (NSA_TQ, nsel), 1)
    cur = (q0 + lax.broadcasted_iota(jnp.int32, (NSA_TQ, nsel), 0)) // SEL_BLOCK
    forced = (blk == 0) | (blk == cur) | (blk == cur - 1)
    cand = jnp.where(blk > cur, NEG_INF, jnp.where(forced, jnp.inf, imp))
    sel = _top_select(cand, blk, jnp.zeros((NSA_TQ, nsel), jnp.bool_), min(SEL_TOPN, nsel))
    selbias = jnp.where(sel, 0.0, MASK_BIAS)

    kt_last = (q0 + NSA_TQ - 1) // ATT_KT
    tiles_per_win = HEAD_DIM * SEL_BLOCK // ATT_KT
    qr32 = qr.astype(F32)
    for w in range(-(-nsel // HEAD_DIM)):
        nb = min(HEAD_DIM, nsel - w * HEAD_DIM)
        pieces = [jnp.zeros((NSA_TQ, HEAD_DIM), F32), selbias[:, w * HEAD_DIM:w * HEAD_DIM + nb]]
        if nb < HEAD_DIM:
            pieces.append(jnp.zeros((NSA_TQ, HEAD_DIM - nb), F32))
        bias_w = jnp.concatenate(pieces, axis=1)
        q3_sc[w] = (qr32 + jnp.concatenate([bias_w] * B_GROUP, axis=0)).astype(BF16)

    qpos_col = q0 + lax.broadcasted_iota(jnp.int32, (rows, 1), 0) % NSA_TQ
    o_sel = _two_pass_attention(q3_sc, ks_ref, vs_ref, kt_last, tiles_per_win, qpos_col,
                                m_sc, l_sc, acc_sc, s_sc)

    kband = jnp.concatenate([r[0] for r in kw_refs], axis=0)
    vband = jnp.concatenate([r[0] for r in vw_refs], axis=0)
    nband = WIN_BLOCKS * NSA_TQ
    s = _dot_t(qr[:, :HEAD_DIM], kband)
    qpos = q0 + lax.broadcasted_iota(jnp.int32, (rows, nband), 0) % NSA_TQ
    kpos = q0 - WINDOW + lax.broadcasted_iota(jnp.int32, (rows, nband), 1)
    p = _masked_softmax(s, (kpos <= qpos) & (kpos >= qpos - WINDOW) & (kpos >= 0))
    o_win = _dot(p.astype(BF16), vband)

    gts = g_ref[0]
    for j in range(B_GROUP):
        sl = slice(j * NSA_TQ, (j + 1) * NSA_TQ)
        o = (gts[:, 3 * j:3 * j + 1] * o_cmp[sl] + gts[:, 3 * j + 1:3 * j + 2] * o_sel[sl]
             + gts[:, 3 * j + 2:3 * j + 3] * o_win[sl])
        o_ref[:, j * HEAD_DIM:(j + 1) * HEAD_DIM] = o


def _nsa_prompt(qb, qbr, kvcmp, ksaug, vs, kw, vw, gates):
    t = qb.shape[1]
    nt = t // NSA_TQ
    nsel = t // SEL_BLOCK
    ncmp = kvcmp.shape[1]
    rows = B_GROUP * NSA_TQ
    impm = _importance_matrix(ncmp, nsel)
    res = lambda w: pl.BlockSpec((1, t, w), lambda g, i: (g, 0, 0))
    band = [pl.BlockSpec((1, NSA_TQ, HEAD_DIM),
                         functools.partial(lambda g, i, j: (g, jnp.maximum(i - (WIN_BLOCKS - 1) + j, 0), 0), j=j))
            for j in range(WIN_BLOCKS)]
    return pl.pallas_call(
        functools.partial(_nsa_prompt_kernel, nsel=nsel, ncmp=ncmp),
        out_shape=jax.ShapeDtypeStruct((t, B_HEADS * HEAD_DIM), F32),
        grid=(B_KV_HEADS, nt),
        in_specs=[pl.BlockSpec((B_GROUP, NSA_TQ, HEAD_DIM), lambda g, i: (g, i, 0)),
                  pl.BlockSpec((B_GROUP, NSA_TQ, LANES), lambda g, i: (g, i, 0)),
                  pl.BlockSpec((1, ncmp, HEAD_DIM), lambda g, i: (g, 0, 0)),
                  pl.BlockSpec((1, ncmp, HEAD_DIM), lambda g, i: (B_KV_HEADS + g, 0, 0)),
                  pl.BlockSpec(impm.shape, lambda g, i: (0, 0)),
                  res(LANES), res(HEAD_DIM)] + band + band +
                 [pl.BlockSpec((1, NSA_TQ, 3 * B_GROUP), lambda g, i: (g, i, 0))],
        out_specs=pl.BlockSpec((NSA_TQ, B_GROUP * HEAD_DIM), lambda g, i: (i, g)),
        scratch_shapes=[pltpu.VMEM((rows, LANES), F32), pltpu.VMEM((rows, LANES), F32),
                        pltpu.VMEM((rows, HEAD_DIM), F32),
                        pltpu.VMEM((-(-nsel // HEAD_DIM), rows, LANES), BF16),
                        pltpu.VMEM((2, rows, ATT_KT), F32)],
        compiler_params=_cparams("parallel", "parallel"),
        name="nsa_prompt",
    )(qb, qbr, kvcmp, kvcmp, impm, ksaug, vs, *([kw] * WIN_BLOCKS), *([vw] * WIN_BLOCKS), gates)


def _top_indices(v, lane, rounds):
    out = jnp.zeros((v.shape[0], LANES), jnp.int32)
    slot = lax.broadcasted_iota(jnp.int32, out.shape, 1)
    for r in range(rounds):
        mx = jnp.max(v, axis=1, keepdims=True)
        idx = jnp.min(jnp.where(v == mx, lane, 1 << 20), axis=1, keepdims=True)
        out = jnp.where(slot == r, idx, out)
        v = jnp.where(lane == idx, NEG_INF, v)
    return out


def _bf16_round(x):
    return x.astype(BF16).astype(F32)


def _moba_kmean_kernel(pt_ref, *refs, pg):
    pages, o_ref = refs[:pg], refs[pg]
    step = pl.program_id(1)
    per_blk = A_BLOCK // pages[0].shape[2]
    nb = pg // per_blk

    @pl.when(step == 0)
    def _():
        o_ref[...] = jnp.zeros(o_ref.shape, F32)

    lane = lax.broadcasted_iota(jnp.int32, o_ref.shape[1:], 1)
    out = o_ref[0]
    for j in range(nb):
        tot = pages[j * per_blk][0]
        for r in pages[j * per_blk + 1:(j + 1) * per_blk]:
            tot = tot + r[0]
        mean = jnp.sum(tot, axis=1, keepdims=True) * (1.0 / A_BLOCK)
        out = jnp.where(lane == step * nb + j, mean, out)
    o_ref[0] = out


def _moba_kmean(cache_t, page_table, *, pg):
    db, npg = page_table.shape
    page = cache_t.shape[2]
    kw = A_KV_HEADS * HEAD_DIM
    assert npg * page // A_BLOCK <= LANES
    specs = [pl.BlockSpec((1, kw, page), functools.partial(lambda b, s, pt, j: (pt[b, s * pg + j], 0, 0), j=j))
             for j in range(pg)]
    return pl.pallas_call(
        functools.partial(_moba_kmean_kernel, pg=pg),
        out_shape=jax.ShapeDtypeStruct((db, kw, LANES), F32),
        grid_spec=pltpu.PrefetchScalarGridSpec(
            num_scalar_prefetch=1, grid=(db, npg // pg), in_specs=specs,
            out_specs=pl.BlockSpec((1, kw, LANES), lambda b, s, pt: (b, 0, 0))),
        compiler_params=_cparams("parallel", "arbitrary"),
        name="moba_kmean",
    )(page_table, *([cache_t] * pg))


def _moba_gate_kernel(q_ref, km_ref, idx_ref, *, nblk):
    q = q_ref[0][:, :HEAD_DIM].astype(BF16)
    head = lax.broadcasted_iota(jnp.int32, (A_HEADS, LANES), 0)
    lane = lax.broadcasted_iota(jnp.int32, (A_HEADS, LANES), 1)
    gate = jnp.zeros((A_HEADS, LANES), F32)
    for g in range(A_KV_HEADS):
        kmg = km_ref[0][g * HEAD_DIM:(g + 1) * HEAD_DIM, :].astype(BF16)
        gate = jnp.where(head // (A_HEADS // A_KV_HEADS) == g, _dot(q, kmg), gate)
    idx_ref[0] = _top_indices(jnp.where(lane < nblk, gate, NEG_INF), lane, A_TOPK)


def _moba_gate(qa_rows, kmean_t, *, nblk):
    db, kw, _ = kmean_t.shape
    return pl.pallas_call(
        functools.partial(_moba_gate_kernel, nblk=nblk),
        out_shape=jax.ShapeDtypeStruct((db, A_HEADS, LANES), jnp.int32),
        grid=(db,),
        in_specs=[pl.BlockSpec((1, A_HEADS, LANES), lambda b: (b, 0, 0)),
                  pl.BlockSpec((1, kw, LANES), lambda b: (b, 0, 0))],
        out_specs=pl.BlockSpec((1, A_HEADS, LANES), lambda b: (b, 0, 0)),
        compiler_params=_cparams("parallel"),
        name="moba_gate",
    )(qa_rows, kmean_t)


def _gathered_attention(q, k_pages, v_pages, masks, k_own, v_own):
    qb = q.astype(BF16)
    ss = [_dot(qb, k.astype(BF16)) for k in k_pages]
    ss = [s if mk is None else jnp.where(mk, s, NEG_INF) for s, mk in zip(ss, masks)]
    s_own = jnp.sum(q * _bf16_round(k_own), axis=1, keepdims=True)
    m = s_own
    for s in ss:
        m = jnp.maximum(m, jnp.max(s, axis=1, keepdims=True))
    e_own = jnp.exp(s_own - m)
    d = e_own
    acc = _bf16_round(e_own) * _bf16_round(v_own)
    for s, v in zip(ss, v_pages):
        e = jnp.exp(s - m)
        d = d + jnp.sum(e, axis=1, keepdims=True)
        acc = acc + _dot_t(e.astype(BF16), v.astype(BF16))
    return acc / d


def _moba_sample_kernel(pt_ref, ix_ref, q_ref, *refs, npick):
    k_refs, v_refs = refs[:npick], refs[npick:2 * npick]
    kn_ref, vn_ref, o_ref, o_sc = refs[2 * npick:]
    h = pl.program_id(1)
    q = q_ref[0][:, :HEAD_DIM] * SCALE
    o_sc[...] = _gathered_attention(q, [r[0] for r in k_refs], [r[0] for r in v_refs], [None] * npick,
                                    kn_ref[0], vn_ref[0])
    o_ref[0, pl.ds(h, 1), :] = o_sc[pl.ds(h, 1), :]


def _moba_sample(cache_t, page_table, idx, qa_rows, k_new, v_new):
    db = page_table.shape[0]
    page = cache_t.shape[2]
    npage = A_BLOCK // page
    rep = A_HEADS // A_KV_HEADS

    def kv_spec(s, j, row_blk0):
        def imap(b, h, pt, ix):
            blk = ix[(b * A_HEADS + h) * A_TOPK + s]
            return (pt[b, blk * npage + j], row_blk0 + h // rep, 0)
        return pl.BlockSpec((1, HEAD_DIM, page), imap)

    picks = [(s, j) for s in range(A_TOPK) for j in range(npage)]
    per_seq = lambda w: pl.BlockSpec((1, A_HEADS, w), lambda b, h, pt, ix: (b, 0, 0))
    return pl.pallas_call(
        functools.partial(_moba_sample_kernel, npick=len(picks)),
        out_shape=jax.ShapeDtypeStruct((db, A_HEADS, HEAD_DIM), F32),
        grid_spec=pltpu.PrefetchScalarGridSpec(
            num_scalar_prefetch=2, grid=(db, A_HEADS),
            in_specs=[per_seq(LANES)]
                     + [kv_spec(s, j, 0) for s, j in picks] + [kv_spec(s, j, A_KV_HEADS) for s, j in picks]
                     + [per_seq(HEAD_DIM), per_seq(HEAD_DIM)],
            out_specs=per_seq(HEAD_DIM),
            scratch_shapes=[pltpu.VMEM((A_HEADS, HEAD_DIM), F32)]),
        compiler_params=_cparams("parallel", "arbitrary"),
        name="moba_sample",
    )(page_table, idx, qa_rows, *([cache_t] * (2 * len(picks))), k_new, v_new)


def _nsa_flatten_kernel(pt_ref, *refs, pg, nch):
    pages = refs[:pg]
    pelo_ref, pehi_ref, wlo_ref, whi_ref, b1_ref, w2_ref, o_ref, x_sc, hi_sc, rows_sc = refs[pg:]
    step = pl.program_id(1)
    page = pages[0].shape[2]
    nc = pg * page // CMP_STRIDE
    base = pl.multiple_of(step * nc, nc)
    lane = lax.broadcasted_iota(jnp.int32, (nc, LANES), 1)
    lo_half = lane < HEAD_DIM
    for j, r in enumerate(pages):
        for pr in range(2):
            rows_sc[pr, j * page:(j + 1) * page, :] = r[0, pr * LANES:(pr + 1) * LANES, :].T
    for u in range(CMP_STRIDE // 2):
        for pr in range(2):
            ap = rows_sc[pr, pl.ds(2 * u, nc, stride=CMP_STRIDE), :]
            bp = rows_sc[pr, pl.ds(2 * u + 1, nc, stride=CMP_STRIDE), :]
            x_sc[2 * pr, pl.ds(base, nc), u * LANES:(u + 1) * LANES] = jnp.where(
                lo_half, ap, pltpu.roll(bp, HEAD_DIM, 1))
            x_sc[2 * pr + 1, pl.ds(base, nc), u * LANES:(u + 1) * LANES] = jnp.where(
                lo_half, pltpu.roll(ap, HEAD_DIM, 1), bp)

    @pl.when(step == pl.num_programs(1) - 1)
    def _():
        hi_sc[nch:nch + 8] = jnp.zeros((8, CMP_HIDDEN), F32)
        for j in range(4):
            c = j // B_KV_HEADS
            x = x_sc[j]
            lo = _dot((x + pelo_ref[c]).astype(BF16), wlo_ref[c])
            hi_sc[0:nch] = _dot((x + pehi_ref[c]).astype(BF16), whi_ref[c])
            hid = _gelu(lo + hi_sc[pl.ds(1, nch), :] + b1_ref[c])
            o_ref[0, j] = _dot(hid.astype(BF16), w2_ref[c]).astype(o_ref.dtype)


def _nsa_sample_compress(cache_t, page_table, cw, *, pg):
    db, npg = page_table.shape
    page = cache_t.shape[2]
    assert page == LANES
    nch = npg * page // CMP_STRIDE
    flat = CMP_STRIDE * HEAD_DIM
    pelo, pehi, wlo, whi, b1, w2 = cw
    full = lambda a: pl.BlockSpec(a.shape, lambda b, s, pt: (0,) * a.ndim)
    specs = [pl.BlockSpec((1, 2 * LANES, page), functools.partial(lambda b, s, pt, j: (pt[b, s * pg + j], 0, 0), j=j))
             for j in range(pg)]
    return pl.pallas_call(
        functools.partial(_nsa_flatten_kernel, pg=pg, nch=nch),
        out_shape=jax.ShapeDtypeStruct((db, 4, nch, HEAD_DIM), BF16),
        grid_spec=pltpu.PrefetchScalarGridSpec(
            num_scalar_prefetch=1, grid=(db, npg // pg),
            in_specs=specs + [full(a) for a in (pelo, pehi, wlo, whi, b1, w2)],
            out_specs=pl.BlockSpec((1, 4, nch, HEAD_DIM), lambda b, s, pt: (b, 0, 0, 0)),
            scratch_shapes=[pltpu.VMEM((4, nch, flat), F32), pltpu.VMEM((nch + 8, CMP_HIDDEN), F32),
                            pltpu.VMEM((2, pg * page, LANES), F32)]),
        compiler_params=_cparams("parallel", "arbitrary"),
        name="nsa_sample_compress",
    )(page_table, *([cache_t] * pg), pelo, pehi, wlo, whi, b1, w2)


def _rows_by_group(per_group):
    head = lax.broadcasted_iota(jnp.int32, per_group[0].shape, 0)
    out = per_group[0]
    for g in range(1, B_KV_HEADS):
        out = jnp.where(head // B_GROUP == g, per_group[g], out)
    return out


def _nsa_sample_select_kernel(q_ref, qr_ref, kv_ref, imp_ref, win_ref, wnew_ref, g_ref, idx_ref, ocw_ref,
                              *, past, nsel_past, ncmp):
    q = q_ref[0][:, :HEAD_DIM].astype(BF16)
    qr = qr_ref[0][:, :HEAD_DIM]
    gts = g_ref[0]
    glane = lax.broadcasted_iota(jnp.int32, (B_HEADS, LANES), 1)
    ghead = lax.broadcasted_iota(jnp.int32, (B_HEADS, LANES), 0)
    gate = lambda c: jnp.sum(jnp.where(glane == 3 * ghead + c, gts, 0.0), axis=1, keepdims=True)

    s = _rows_by_group([_dot_t(q, kv_ref[0, g]) for g in range(B_KV_HEADS)])
    n_c = lax.broadcasted_iota(jnp.int32, (B_HEADS, ncmp), 1)
    p = _masked_softmax(s, n_c * CMP_STRIDE + (CMP_LEN - 1) <= past)
    o_cmp = _rows_by_group([_dot(p.astype(BF16), kv_ref[0, B_KV_HEADS + g]) for g in range(B_KV_HEADS)])

    head = lax.broadcasted_iota(jnp.int32, (B_HEADS, ncmp), 0)
    psum = jnp.zeros((B_HEADS, ncmp), F32)
    for g in range(B_KV_HEADS):
        tot = jnp.sum(jnp.where(head // B_GROUP == g, p, 0.0), axis=0, keepdims=True)
        psum = jnp.where(head == g, tot, psum)
    p_hi = psum.astype(BF16)
    p_lo = (psum - p_hi.astype(F32)).astype(BF16)
    imp = _dot(p_hi, imp_ref[...]) + _dot(p_lo, imp_ref[...])
    blk = lax.broadcasted_iota(jnp.int32, (B_HEADS, nsel_past), 1)
    cand = jnp.where((blk == 0) | (blk == nsel_past - 1), jnp.inf, imp)
    idx_ref[0] = _top_indices(cand, blk, min(SEL_TOPN, nsel_past + 1) - 1)

    wk = [win_ref[0][g * HEAD_DIM:(g + 1) * HEAD_DIM, :] for g in range(B_KV_HEADS)]
    wv = [win_ref[0][(B_KV_HEADS + g) * HEAD_DIM:(B_KV_HEADS + g + 1) * HEAD_DIM, :] for g in range(B_KV_HEADS)]
    nk = [wnew_ref[0][:, g * HEAD_DIM:(g + 1) * HEAD_DIM] for g in range(B_KV_HEADS)]
    nv = [wnew_ref[0][:, (B_KV_HEADS + g) * HEAD_DIM:(B_KV_HEADS + g + 1) * HEAD_DIM] for g in range(B_KV_HEADS)]
    s_w = _rows_by_group([_dot(qr.astype(BF16), k.astype(BF16)) for k in wk])
    s_n = _rows_by_group([jnp.sum(qr * _bf16_round(k), axis=1, keepdims=True) for k in nk])
    m = jnp.maximum(jnp.max(s_w, axis=1, keepdims=True), s_n)
    e_w, e_n = jnp.exp(s_w - m), jnp.exp(s_n - m)
    d = jnp.sum(e_w, axis=1, keepdims=True) + e_n
    o_win = _rows_by_group([_dot_t(e_w.astype(BF16), v.astype(BF16)) + _bf16_round(e_n) * _bf16_round(nvg)
                            for v, nvg in zip(wv, nv)]) / d

    ocw_ref[0] = jnp.concatenate([gate(0) * o_cmp, gate(2) * o_win], axis=1)


def _nsa_sample_select(qb_rows, qbr_rows, kvcmp, win_cache, win_new, gates, *, past):
    db, _, ncmp, _ = kvcmp.shape
    nsel_past = past // SEL_BLOCK
    impm = _importance_matrix(ncmp, nsel_past)
    row3 = lambda a: pl.BlockSpec((1,) + a.shape[1:], lambda b: (b,) + (0,) * (a.ndim - 1))
    kern = functools.partial(_nsa_sample_select_kernel, past=past, nsel_past=nsel_past, ncmp=ncmp)
    return pl.pallas_call(
        kern,
        out_shape=(jax.ShapeDtypeStruct((db, B_HEADS, LANES), jnp.int32),
                   jax.ShapeDtypeStruct((db, B_HEADS, LANES), F32)),
        grid=(db,),
        in_specs=[row3(qb_rows), row3(qbr_rows), row3(kvcmp), pl.BlockSpec(impm.shape, lambda b: (0, 0)),
                  row3(win_cache), row3(win_new), row3(gates)],
        out_specs=(pl.BlockSpec((1, B_HEADS, LANES), lambda b: (b, 0, 0)),
                   pl.BlockSpec((1, B_HEADS, LANES), lambda b: (b, 0, 0))),
        compiler_params=_cparams("parallel"),
        name="nsa_sample_select",
    )(qb_rows, qbr_rows, kvcmp, impm, win_cache, win_new, gates)


def _nsa_sample_attend_kernel(pt_ref, ix_ref, qr_ref, *refs, nslot, per_page):
    k_refs, v_refs = refs[:nslot], refs[nslot:2 * nslot]
    kn_ref, vn_ref, g_ref, ocw_ref, o_ref = refs[2 * nslot:]
    b, g = pl.program_id(0), pl.program_id(1)
    qs = qr_ref[0, pl.ds(g * B_GROUP, B_GROUP), :][:, :HEAD_DIM]

    page = k_refs[0].shape[2]
    lane = lax.broadcasted_iota(jnp.int32, (B_GROUP, page), 1)
    masks = [lane // SEL_BLOCK == ix_ref[(b * B_KV_HEADS + g) * nslot + s] % per_page for s in range(nslot)]
    o_sel = _gathered_attention(qs, [r[0] for r in k_refs], [r[0] for r in v_refs], masks,
                                kn_ref[0, 0], vn_ref[0, 0])
    gts = g_ref[0]
    hrow = lax.broadcasted_iota(jnp.int32, (B_GROUP, LANES), 0) + g * B_GROUP
    glane = lax.broadcasted_iota(jnp.int32, (B_GROUP, LANES), 1)
    g1 = jnp.sum(jnp.where(glane == 3 * hrow + 1, gts, 0.0), axis=1, keepdims=True)
    ocw = ocw_ref[0, pl.ds(g * B_GROUP, B_GROUP), :]
    o_ref[0, pl.ds(g * B_GROUP, B_GROUP), :] = g1 * o_sel + ocw[:, :HEAD_DIM] + ocw[:, HEAD_DIM:]


def _nsa_sample_attend(cache_t, page_table, idx, qbr_rows, ks_new, vs_new, gates, ocw, *, nslot):
    db = page_table.shape[0]
    page = cache_t.shape[2]
    per_page = page // SEL_BLOCK

    def kv_spec(s, row_blk0):
        def imap(b, g, pt, ix):
            blk = ix[(b * B_KV_HEADS + g) * nslot + s]
            return (pt[b, blk // per_page], row_blk0 + g, 0)
        return pl.BlockSpec((1, HEAD_DIM, page), imap)

    row = lambda a: pl.BlockSpec((1,) + a.shape[1:], lambda b, g, pt, ix: (b,) + (0,) * (a.ndim - 1))
    new = pl.BlockSpec((1, 1, 1, HEAD_DIM), lambda b, g, pt, ix: (b, g, 0, 0))
    return pl.pallas_call(
        functools.partial(_nsa_sample_attend_kernel, nslot=nslot, per_page=per_page),
        out_shape=jax.ShapeDtypeStruct((db, B_HEADS, HEAD_DIM), F32),
        grid_spec=pltpu.PrefetchScalarGridSpec(
            num_scalar_prefetch=2, grid=(db, B_KV_HEADS),
            in_specs=[row(qbr_rows)] + [kv_spec(s, 2 * B_KV_HEADS) for s in range(nslot)]
                     + [kv_spec(s, 3 * B_KV_HEADS) for s in range(nslot)] + [new, new, row(gates), row(ocw)],
            out_specs=pl.BlockSpec((1, B_HEADS, HEAD_DIM), lambda b, g, pt, ix: (b, 0, 0))),
        compiler_params=_cparams("parallel", "arbitrary"),
        name="nsa_sample_attend",
    )(page_table, idx, qbr_rows, *([cache_t] * (2 * nslot)), ks_new, vs_new, gates, ocw)


def _prep_weights(l, g_mix_pre, w_in, b_gate, cmp_pe, cmp_w1, cmp_b1, cmp_w2, w_out, g_mix_post,
                  g_ffn_pre, w_up, w_conv, b_conv, w_down, g_ffn_post):
    d_in = w_in.shape[-1]
    n_gate = b_gate.shape[-1]
    return dict(
        g_mix_pre=g_mix_pre[l][None],
        w_in=jnp.pad(w_in[l], ((0, 0), (0, D_IN_PAD - d_in))).astype(BF16),
        b_gate=jnp.pad(b_gate[l], (0, LANES - n_gate))[None],
        cmp=_compress_weights(cmp_pe[l], cmp_w1[l], cmp_b1[l], cmp_w2[l]),
        w_out=w_out[l].astype(BF16), g_mix_post=g_mix_post[l][None], g_ffn_pre=g_ffn_pre[l][None],
        w_up=w_up[l].astype(BF16), w_conv=w_conv[l], b_conv=b_conv[l][None],
        w_down=w_down[l].astype(BF16), g_ffn_post=g_ffn_post[l][None],
    )


def _group_gates(gates):
    t = gates.shape[0]
    return gates[:, :3 * B_HEADS].reshape(t, B_KV_HEADS, 3 * B_GROUP).transpose(1, 0, 2)


FFN_CK = 256
FFN_TM = 1024


def _prompt_layer(x2d, w):
    t = x2d.shape[0]
    (kva, nsa, win, gates, qa, kaug, va, qb, qbr, ksaug, vs, kw, vw, kcvc, kmean) = _inproj(
        x2d, jnp.arange(t), w["g_mix_pre"], w["w_in"], w["b_gate"], tm=512, with_kmean=True)
    o_a = _moba_prompt(qa, _gate_matrix(kmean.reshape(t // A_BLOCK, A_KV_HEADS * HEAD_DIM)), kaug, va)
    kvcmp = _compress(kcvc.reshape(4, t // CMP_STRIDE, CMP_STRIDE * HEAD_DIM), w["cmp"])
    o_b = _nsa_prompt(qb, qbr, kvcmp, ksaug, vs, kw, vw, _group_gates(gates))
    x1, h2 = _outproj(o_a, o_b, x2d, w["w_out"], w["g_mix_post"], w["g_ffn_pre"], tm=512)
    y, conv_state = _ffn_seq(h2, x1, w["w_up"], w["w_conv"], w["b_conv"], w["w_down"], w["g_ffn_post"],
                             tm=FFN_TM, ck=FFN_CK)
    keep = min(WINDOW, t)
    return (y, kva.reshape(t, 2, A_KV_HEADS, HEAD_DIM), nsa.reshape(t, 4, B_KV_HEADS, HEAD_DIM),
            win[t - keep:].reshape(keep, 2, B_KV_HEADS, HEAD_DIM), conv_state)


PAGES_PER_STEP = 16


def _sample_layer(x2d, cache_moba, cache_nsa, win_cache, conv_state, page_table, w):
    db = x2d.shape[0]
    n_pool, page = cache_moba.shape[:2]
    npg = page_table.shape[1]
    past = npg * page
    wb = win_cache.shape[1]
    assert wb == WINDOW and past % A_BLOCK == 0 and past // A_BLOCK >= A_TOPK and A_BLOCK % page == 0
    assert page % SEL_BLOCK == 0
    pg = min(PAGES_PER_STEP, npg)
    (kva, nsa, win, gates, qa, _, _, qb, qbr, _, _, _, _, _, _) = _inproj(
        x2d, jnp.full((db,), past, jnp.int32), w["g_mix_pre"], w["w_in"], w["b_gate"], tm=db, with_kmean=False)
    rows = lambda a: a.transpose(1, 0, 2).astype(F32)
    qa_rows, qb_rows, qbr_rows = rows(qa), rows(qb), rows(qbr)

    tpose = lambda c: c.transpose(0, 2, 3, 4, 1).reshape(c.shape[0], -1, c.shape[1])
    moba_t, nsa_t, win_t = tpose(cache_moba), tpose(cache_nsa), tpose(win_cache)

    kmean_t = _moba_kmean(moba_t, page_table, pg=pg)
    idx_a = _moba_gate(qa_rows, kmean_t, nblk=past // A_BLOCK)[:, :, :A_TOPK].reshape(-1)
    rep = A_HEADS // A_KV_HEADS
    per_head = lambda a: jnp.repeat(a.reshape(db, A_KV_HEADS, HEAD_DIM), rep, axis=1)
    o_a = _moba_sample(moba_t, page_table, idx_a, qa_rows, per_head(kva[:, :256]), per_head(kva[:, 256:]))

    kvcmp = _nsa_sample_compress(nsa_t, page_table, w["cmp"], pg=pg)
    gates3 = gates[:, None, :]
    win2d = win_cache.reshape(db, wb, 2 * B_KV_HEADS * HEAD_DIM)
    win_new = win[:, None, :]
    idx_b, ocw = _nsa_sample_select(qb_rows, qbr_rows, kvcmp, win_t, win_new, gates3, past=past)
    nslot = min(SEL_TOPN, past // SEL_BLOCK + 1) - 1
    per_group = lambda a: a.reshape(db, B_KV_HEADS, 1, HEAD_DIM)
    o_b = _nsa_sample_attend(nsa_t, page_table, idx_b[:, :B_KV_HEADS, :nslot].reshape(-1), qbr_rows,
                             per_group(nsa[:, 256:384]), per_group(nsa[:, 384:]), gates3, ocw, nslot=nslot)

    x1, h2 = _outproj(o_a.reshape(db, -1), o_b.reshape(db, -1), x2d, w["w_out"], w["g_mix_post"],
                      w["g_ffn_pre"], tm=db)
    y, conv_new = _ffn_step(h2, x1, conv_state, w["w_up"], w["w_conv"], w["b_conv"], w["w_down"],
                            w["g_ffn_post"], ck=FFN_CK)
    keep = min(WINDOW, wb + 1)
    win_all = jnp.concatenate([win2d, win_new], axis=1)[:, wb + 1 - keep:]
    return (y, kva.reshape(db, 2, A_KV_HEADS, HEAD_DIM), nsa.reshape(db, 4, B_KV_HEADS, HEAD_DIM),
            win_all.reshape(db, keep, 2, B_KV_HEADS, HEAD_DIM), conv_new)


def kernel(x_prompt, x_sample, cache_moba_kv, cache_nsa_kv, cache_nsa_win_kv, state_ffn_conv, page_table,
           g_mix_pre, w_in, b_gate, cmp_pe, cmp_w1, cmp_b1, cmp_w2, w_out, g_mix_post, g_ffn_pre, w_up,
           w_conv, b_conv, w_down, g_ffn_post):
    depth = w_in.shape[0]
    assert depth == 1 and x_prompt.shape[0] == 1
    w = _prep_weights(0, g_mix_pre, w_in, b_gate, cmp_pe, cmp_w1, cmp_b1, cmp_w2, w_out, g_mix_post,
                      g_ffn_pre, w_up, w_conv, b_conv, w_down, g_ffn_post)
    y_p, moba_p, nsa_p, win_p, conv_p = _prompt_layer(x_prompt[0], w)
    assert x_sample.shape[1] == 1
    y_s, moba_s, nsa_s, win_s, conv_s = _sample_layer(
        x_sample[:, 0], cache_moba_kv[0], cache_nsa_kv[0], cache_nsa_win_kv[0], state_ffn_conv[0],
        page_table, w)
    return (y_p[None], y_s[:, None], moba_p[None, None], moba_s[None, :, None],
            nsa_p[None, None], nsa_s[None, :, None], win_p[None, None], win_s[None],
            conv_p[None, None], conv_s[None])
```

```python
import functools
import math

import numpy as np
import jax
import jax.numpy as jnp
from jax import lax
from jax.experimental import pallas as pl
from jax.experimental.pallas import tpu as pltpu

HEAD_DIM = 64
ROPE_DIM = HEAD_DIM // 4
ROPE_THETA = 500000.0
A_HEADS = 8
A_KV_HEADS = 4
A_BLOCK = 256
A_TOPK = 3
B_HEADS = 8
B_KV_HEADS = 2
B_GROUP = B_HEADS // B_KV_HEADS
CMP_LEN = 32
CMP_STRIDE = 16
CMP_HIDDEN = 128
SEL_BLOCK = 64
SEL_TOPN = 16
WINDOW = 512
CONV_W = 3
RMS_EPS = 1e-6
SCALE = HEAD_DIM ** -0.5

LANES = 128
VMEM_LIMIT = 56 * 1024 * 1024
MASK_BIAS = -32768.0
NEG_INF = float("-inf")

BF16 = jnp.bfloat16
F32 = jnp.float32


def _cparams(*sem):
    return pltpu.CompilerParams(dimension_semantics=sem, vmem_limit_bytes=VMEM_LIMIT)


def _dot(a, b):
    return jnp.dot(a, b, preferred_element_type=F32)


def _dot_t(a, b):
    return lax.dot_general(a, b, (((1,), (1,)), ((), ())), preferred_element_type=F32)


def _rms(x, g):
    y = x * lax.rsqrt(jnp.mean(x * x, axis=-1, keepdims=True) + RMS_EPS)
    return y * g


def _gelu(x):
    c = math.sqrt(2.0 / math.pi)
    return 0.5 * x * (1.0 + jnp.tanh(c * (x + 0.044715 * (x * x * x))))


def _rope128(x, c, sa, sb):
    return x * c + pltpu.roll(x, LANES - ROPE_DIM // 2, 1) * sa + pltpu.roll(x, ROPE_DIM // 2, 1) * sb


C_QA, C_KA, C_VA, C_QB = 0, 512, 768, 1024
C_KC, C_VC, C_KS, C_VS, C_KW, C_VW, C_G = 1536, 1664, 1792, 1920, 2048, 2176, 2304
D_IN_PAD = 2432


def _inproj_kernel(x_ref, g_ref, w_ref, bg_ref, c_ref, sa_ref, sb_ref,
                   kva_ref, nsa_ref, win_ref, gates_ref,
                   qa_ref, kaug_ref, va_ref, qb_ref, qbr_ref, ksaug_ref, vs_ref, kw_ref, vw_ref,
                   kcvc_ref, kmean_ref, *, tm, with_kmean):
    i = pl.program_id(0)
    h = _rms(x_ref[...], g_ref[...]).astype(BF16)
    c, sa, sb = c_ref[...], sa_ref[...], sb_ref[...]

    def proj(c0, width):
        return _dot(h, w_ref[:, c0:c0 + width])

    lane = lax.broadcasted_iota(jnp.int32, (tm, LANES), 1)
    row = lax.broadcasted_iota(jnp.int32, (tm, LANES), 0) + i * tm
    zeros64 = jnp.zeros((tm, HEAD_DIM), BF16)

    for p in range(4):
        q = _rope128(proj(C_QA + p * LANES, LANES), c, sa, sb)
        qa_ref[2 * p] = jnp.concatenate([q[:, :HEAD_DIM].astype(BF16), zeros64], axis=1)
        qa_ref[2 * p + 1] = jnp.concatenate([q[:, HEAD_DIM:].astype(BF16), zeros64], axis=1)

    a_onehot = (lane - HEAD_DIM == row // A_BLOCK).astype(BF16)
    ksum = []
    for p in range(2):
        k = _rope128(proj(C_KA + p * LANES, LANES), c, sa, sb)
        v = proj(C_VA + p * LANES, LANES)
        kva_ref[:, p * LANES:(p + 1) * LANES] = k
        kva_ref[:, 256 + p * LANES:256 + (p + 1) * LANES] = v
        kb = k.astype(BF16)
        vb = v.astype(BF16)
        kaug_ref[2 * p] = jnp.concatenate([kb[:, :HEAD_DIM], a_onehot[:, HEAD_DIM:]], axis=1)
        kaug_ref[2 * p + 1] = jnp.concatenate([kb[:, HEAD_DIM:], a_onehot[:, HEAD_DIM:]], axis=1)
        va_ref[2 * p] = vb[:, :HEAD_DIM]
        va_ref[2 * p + 1] = vb[:, HEAD_DIM:]
        if with_kmean:
            ksum.append(jnp.sum(k.reshape(tm // A_BLOCK, A_BLOCK, LANES), axis=1))
    if with_kmean:
        kmean_ref[0] = jnp.concatenate(ksum, axis=1) * (1.0 / A_BLOCK)
    else:
        kmean_ref[...] = jnp.zeros(kmean_ref.shape, F32)

    for p in range(4):
        q = proj(C_QB + p * LANES, LANES) * SCALE
        qr = _rope128(q, c, sa, sb)
        qb_ref[2 * p] = q[:, :HEAD_DIM].astype(BF16)
        qb_ref[2 * p + 1] = q[:, HEAD_DIM:].astype(BF16)
        qbr_ref[2 * p] = jnp.concatenate([qr[:, :HEAD_DIM].astype(BF16), zeros64], axis=1)
        qbr_ref[2 * p + 1] = jnp.concatenate([qr[:, HEAD_DIM:].astype(BF16), zeros64], axis=1)

    kc = proj(C_KC, LANES)
    vc = proj(C_VC, LANES)
    ks = _rope128(proj(C_KS, LANES), c, sa, sb)
    vs = proj(C_VS, LANES)
    nsa_ref[:, 0:128] = kc
    nsa_ref[:, 128:256] = vc
    nsa_ref[:, 256:384] = ks
    nsa_ref[:, 384:512] = vs
    kcvc_ref[0] = kc[:, :HEAD_DIM]
    kcvc_ref[1] = kc[:, HEAD_DIM:]
    kcvc_ref[2] = vc[:, :HEAD_DIM]
    kcvc_ref[3] = vc[:, HEAD_DIM:]
    s_onehot = (lane - HEAD_DIM == (row // SEL_BLOCK) % HEAD_DIM).astype(BF16)
    ksb = ks.astype(BF16)
    vsb = vs.astype(BF16)
    ksaug_ref[0] = jnp.concatenate([ksb[:, :HEAD_DIM], s_onehot[:, HEAD_DIM:]], axis=1)
    ksaug_ref[1] = jnp.concatenate([ksb[:, HEAD_DIM:], s_onehot[:, HEAD_DIM:]], axis=1)
    vs_ref[0] = vsb[:, :HEAD_DIM]
    vs_ref[1] = vsb[:, HEAD_DIM:]

    kw = _rope128(proj(C_KW, LANES), c, sa, sb)
    vw = proj(C_VW, LANES)
    win_ref[:, 0:128] = kw
    win_ref[:, 128:256] = vw
    kwb = kw.astype(BF16)
    vwb = vw.astype(BF16)
    kw_ref[0] = kwb[:, :HEAD_DIM]
    kw_ref[1] = kwb[:, HEAD_DIM:]
    vw_ref[0] = vwb[:, :HEAD_DIM]
    vw_ref[1] = vwb[:, HEAD_DIM:]

    gates_ref[...] = jax.nn.sigmoid(proj(C_G, LANES) + bg_ref[...])


def _rope_tables(pos):
    half = ROPE_DIM // 2
    inv = ROPE_THETA ** (-2.0 * jnp.arange(half, dtype=F32) / ROPE_DIM)
    ang = pos.astype(F32)[:, None] * inv[None, :]
    cos, sin = jnp.cos(ang), jnp.sin(ang)
    t = pos.shape[0]
    ones = jnp.ones((t, HEAD_DIM - ROPE_DIM), F32)
    zeros = jnp.zeros((t, HEAD_DIM - ROPE_DIM), F32)
    zh = jnp.zeros((t, half), F32)
    c = jnp.concatenate([cos, cos, ones], axis=1)
    sa = jnp.concatenate([-sin, zh, zeros], axis=1)
    sb = jnp.concatenate([zh, sin, zeros], axis=1)
    return tuple(jnp.concatenate([a, a], axis=1) for a in (c, sa, sb))


def _inproj(x2d, pos, g, w_pad, bg_pad, *, tm, with_kmean):
    t, d = x2d.shape
    nt = t // tm
    c, sa, sb = _rope_tables(pos)
    row_spec = lambda w: pl.BlockSpec((tm, w), lambda i: (i, 0))
    head_spec = lambda n, w: pl.BlockSpec((n, tm, w), lambda i: (0, i, 0))
    full = lambda a: pl.BlockSpec(a.shape, lambda i: (0,) * a.ndim)
    nkm = max(tm // A_BLOCK, 1)
    out_shape = (
        jax.ShapeDtypeStruct((t, 512), F32),
        jax.ShapeDtypeStruct((t, 512), F32),
        jax.ShapeDtypeStruct((t, 256), F32),
        jax.ShapeDtypeStruct((t, LANES), F32),
        jax.ShapeDtypeStruct((A_HEADS, t, LANES), BF16),
        jax.ShapeDtypeStruct((A_KV_HEADS, t, LANES), BF16),
        jax.ShapeDtypeStruct((A_KV_HEADS, t, HEAD_DIM), BF16),
        jax.ShapeDtypeStruct((B_HEADS, t, HEAD_DIM), BF16),
        jax.ShapeDtypeStruct((B_HEADS, t, LANES), BF16),
        jax.ShapeDtypeStruct((B_KV_HEADS, t, LANES), BF16),
        jax.ShapeDtypeStruct((B_KV_HEADS, t, HEAD_DIM), BF16),
        jax.ShapeDtypeStruct((B_KV_HEADS, t, HEAD_DIM), BF16),
        jax.ShapeDtypeStruct((B_KV_HEADS, t, HEAD_DIM), BF16),
        jax.ShapeDtypeStruct((4, t, HEAD_DIM), F32),
        jax.ShapeDtypeStruct((nt, nkm, 256), F32),
    )
    out_specs = (
        row_spec(512), row_spec(512), row_spec(256), row_spec(LANES),
        head_spec(A_HEADS, LANES), head_spec(A_KV_HEADS, LANES), head_spec(A_KV_HEADS, HEAD_DIM),
        head_spec(B_HEADS, HEAD_DIM), head_spec(B_HEADS, LANES), head_spec(B_KV_HEADS, LANES),
        head_spec(B_KV_HEADS, HEAD_DIM), head_spec(B_KV_HEADS, HEAD_DIM), head_spec(B_KV_HEADS, HEAD_DIM),
        head_spec(4, HEAD_DIM),
        pl.BlockSpec((1, nkm, 256), lambda i: (i, 0, 0)),
    )
    return pl.pallas_call(
        functools.partial(_inproj_kernel, tm=tm, with_kmean=with_kmean),
        out_shape=out_shape,
        grid=(nt,),
        in_specs=[row_spec(d), full(g), full(w_pad), full(bg_pad),
                  row_spec(LANES), row_spec(LANES), row_spec(LANES)],
        out_specs=out_specs,
        compiler_params=_cparams("parallel"),
        name="inproj",
    )(x2d, g, w_pad, bg_pad, c, sa, sb)


def _outproj_kernel(oa_ref, ob_ref, x_ref, wa_ref, wb_ref, gpost_ref, gpre_ref, x1_ref, h2_ref):
    mix = _dot(oa_ref[...].astype(BF16), wa_ref[...]) + _dot(ob_ref[...].astype(BF16), wb_ref[...])
    x1 = x_ref[...] + _rms(mix, gpost_ref[...])
    x1_ref[...] = x1
    h2_ref[...] = _rms(x1, gpre_ref[...]).astype(BF16)


def _outproj(oa, ob, x2d, w_out_bf, g_post, g_pre, *, tm):
    t, d = x2d.shape
    half = oa.shape[1]
    row = lambda w: pl.BlockSpec((tm, w), lambda i: (i, 0))
    full = lambda a: pl.BlockSpec(a.shape, lambda i: (0,) * a.ndim)
    wa, wb = w_out_bf[:half], w_out_bf[half:]
    return pl.pallas_call(
        _outproj_kernel,
        out_shape=(jax.ShapeDtypeStruct((t, d), F32), jax.ShapeDtypeStruct((t, d), BF16)),
        grid=(t // tm,),
        in_specs=[row(half), row(half), row(d), full(wa), full(wb), full(g_post), full(g_pre)],
        out_specs=(row(d), row(d)),
        compiler_params=_cparams("parallel"),
        name="outproj",
    )(oa, ob, x2d, wa, wb, g_post, g_pre)


HALO = 8


def _ffn_seq_kernel(h_ref, halo_ref, x1_ref, wg_ref, wv_ref, cg_ref, cv_ref, bg_ref, bv_ref,
                    wd_ref, gpost_ref, y_ref, tailg_ref, tailv_ref, acc_ref, ug_ref, uv_ref, *, tm):
    i, c = pl.program_id(0), pl.program_id(1)

    @pl.when(c == 0)
    def _():
        acc_ref[...] = jnp.zeros(acc_ref.shape, F32)

    keep = (i > 0).astype(F32)

    def conv(w_ref, u_ref, cw_ref, cb_ref, tail_ref):
        u_ref[0:HALO] = _dot(halo_ref[...], w_ref[...]) * keep
        u_ref[HALO:HALO + tm] = _dot(h_ref[...], w_ref[...])
        tail_ref[0] = u_ref[tm:tm + HALO]
        cw = cw_ref[...]
        return (u_ref[pl.ds(HALO - 2, tm), :] * cw[0:1] + u_ref[pl.ds(HALO - 1, tm), :] * cw[1:2]
                + u_ref[pl.ds(HALO, tm), :] * cw[2:3] + cb_ref[...])

    gate = conv(wg_ref, ug_ref, cg_ref, bg_ref, tailg_ref)
    val = conv(wv_ref, uv_ref, cv_ref, bv_ref, tailv_ref)
    acc_ref[...] += _dot((_gelu(gate) * val).astype(BF16), wd_ref[...])

    @pl.when(c == pl.num_programs(1) - 1)
    def _():
        y_ref[...] = x1_ref[...] + _rms(acc_ref[...], gpost_ref[...])


def _ffn_step_kernel(h_ref, p0g_ref, p1g_ref, p0v_ref, p1v_ref, x1_ref, wg_ref, wv_ref, cg_ref, cv_ref,
                     bg_ref, bv_ref, wd_ref, gpost_ref, y_ref, upg_ref, upv_ref, acc_ref):
    c = pl.program_id(0)

    @pl.when(c == 0)
    def _():
        acc_ref[...] = jnp.zeros(acc_ref.shape, F32)

    def conv(w_ref, p0_ref, p1_ref, cw_ref, cb_ref, up_ref):
        u = _dot(h_ref[...], w_ref[...])
        up_ref[...] = u
        cw = cw_ref[...]
        return p0_ref[...] * cw[0:1] + p1_ref[...] * cw[1:2] + u * cw[2:3] + cb_ref[...]

    gate = conv(wg_ref, p0g_ref, p1g_ref, cg_ref, bg_ref, upg_ref)
    val = conv(wv_ref, p0v_ref, p1v_ref, cv_ref, bv_ref, upv_ref)
    acc_ref[...] += _dot((_gelu(gate) * val).astype(BF16), wd_ref[...])

    @pl.when(c == pl.num_programs(0) - 1)
    def _():
        y_ref[...] = x1_ref[...] + _rms(acc_ref[...], gpost_ref[...])


def _ffn_seq(h2, x1, w_up_bf, w_conv, b_conv2d, w_down_bf, g_post, *, tm, ck):
    t, d = x1.shape
    dff = w_down_bf.shape[0]
    nff = dff // ck
    nt = t // tm
    hb = tm // HALO
    gcol = lambda r: pl.BlockSpec((r, ck), lambda i, c: (0, c))
    vcol = lambda r: pl.BlockSpec((r, ck), lambda i, c: (0, nff + c))
    row = pl.BlockSpec((tm, d), lambda i, c: (i, 0))
    tail = pl.BlockSpec((1, HALO, ck), lambda i, c: (i, 0, c))
    y, tg, tv = pl.pallas_call(
        functools.partial(_ffn_seq_kernel, tm=tm),
        out_shape=(jax.ShapeDtypeStruct((t, d), F32),
                   jax.ShapeDtypeStruct((nt, HALO, dff), F32), jax.ShapeDtypeStruct((nt, HALO, dff), F32)),
        grid=(nt, nff),
        in_specs=[row, pl.BlockSpec((HALO, d), lambda i, c: (jnp.maximum(i * hb - 1, 0), 0)), row,
                  gcol(d), vcol(d), gcol(CONV_W), vcol(CONV_W), gcol(1), vcol(1),
                  pl.BlockSpec((ck, d), lambda i, c: (c, 0)),
                  pl.BlockSpec(g_post.shape, lambda i, c: (0, 0))],
        out_specs=(row, tail, tail),
        scratch_shapes=[pltpu.VMEM((tm, d), F32), pltpu.VMEM((tm + HALO, ck), F32),
                        pltpu.VMEM((tm + HALO, ck), F32)],
        compiler_params=_cparams("parallel", "arbitrary"),
        name="ffn_seq",
    )(h2, h2, x1, w_up_bf, w_up_bf, w_conv, w_conv, b_conv2d, b_conv2d, w_down_bf, g_post)
    state = jnp.concatenate([tg[-1, HALO - 2:], tv[-1, HALO - 2:]], axis=1)
    return y, state


def _ffn_step(h2, x1, prev, w_up_bf, w_conv, b_conv2d, w_down_bf, g_post, *, ck):
    t, d = x1.shape
    dff = w_down_bf.shape[0]
    nff = dff // ck
    p0, p1 = prev[:, 0], prev[:, 1]
    gcol = lambda r: pl.BlockSpec((r, ck), lambda c: (0, c))
    vcol = lambda r: pl.BlockSpec((r, ck), lambda c: (0, nff + c))
    row = pl.BlockSpec((t, d), lambda c: (0, 0))
    y, ug, uv = pl.pallas_call(
        _ffn_step_kernel,
        out_shape=(jax.ShapeDtypeStruct((t, d), F32),
                   jax.ShapeDtypeStruct((t, dff), F32), jax.ShapeDtypeStruct((t, dff), F32)),
        grid=(nff,),
        in_specs=[row, gcol(t), gcol(t), vcol(t), vcol(t), row,
                  gcol(d), vcol(d), gcol(CONV_W), vcol(CONV_W), gcol(1), vcol(1),
                  pl.BlockSpec((ck, d), lambda c: (c, 0)),
                  pl.BlockSpec(g_post.shape, lambda c: (0, 0))],
        out_specs=(row, pl.BlockSpec((t, ck), lambda c: (0, c)), pl.BlockSpec((t, ck), lambda c: (0, c))),
        scratch_shapes=[pltpu.VMEM((t, d), F32)],
        compiler_params=_cparams("arbitrary"),
        name="ffn_step",
    )(h2, p0, p1, p0, p1, x1, w_up_bf, w_up_bf, w_conv, w_conv, b_conv2d, b_conv2d, w_down_bf, g_post)
    state = jnp.stack([p1, jnp.concatenate([ug, uv], axis=1)], axis=1)
    return y, state


ATT_KT = 512
MOBA_TQ = ATT_KT


def _softmax_init(m_sc, l_sc, acc_sc):
    m_sc[...] = jnp.full(m_sc.shape, NEG_INF, F32)
    l_sc[...] = jnp.zeros(l_sc.shape, F32)
    acc_sc[...] = jnp.zeros(acc_sc.shape, F32)


def _two_pass_attention(q3_ref, k_ref, v_ref, n_full, tiles_per_win, qpos, m_sc, l_sc, acc_sc, p_sc):
    nwin, rows, _ = q3_ref.shape
    kt = p_sc.shape[2]
    groups = [slice(j * LANES, (j + 1) * LANES) for j in range(kt // LANES)]

    def scores(t):
        off = pl.multiple_of(t * kt, kt)
        w = t // tiles_per_win if nwin > 1 else 0
        return _dot_t(q3_ref[w], k_ref[0, pl.ds(off, kt), :])

    def causal(s):
        kpos = n_full * kt + lax.broadcasted_iota(jnp.int32, (rows, kt), 1)
        return jnp.where(kpos <= qpos, s, NEG_INF)

    def take_max(s):
        mx = s[:, groups[0]]
        for gs in groups[1:]:
            mx = jnp.maximum(mx, s[:, gs])
        m_sc[...] = jnp.maximum(m_sc[...], mx)

    def probs(s):
        m = m_sc[...]
        ps = [jnp.exp(s[:, gs] - m) for gs in groups]
        tot = ps[0]
        for p in ps[1:]:
            tot = tot + p
        l_sc[...] += tot
        return jnp.concatenate(ps, axis=1).astype(BF16)

    def add_pv(p, t):
        off = pl.multiple_of(t * kt, kt)
        acc_sc[...] += _dot(p, v_ref[0, pl.ds(off, kt), :])

    _softmax_init(m_sc, l_sc, acc_sc)

    def max_body(t, carry):
        take_max(scores(t))
        return carry

    lax.fori_loop(0, n_full, max_body, 0)
    take_max(causal(scores(n_full)))
    m_sc[...] = jnp.broadcast_to(jnp.max(m_sc[...], axis=1, keepdims=True), m_sc.shape)

    p_sc[1] = jnp.zeros(p_sc.shape[1:], BF16)

    def sum_body(t, carry):
        add_pv(p_sc[(t + 1) % 2], jnp.maximum(t - 1, 0))
        p_sc[t % 2] = probs(scores(t))
        return carry

    lax.fori_loop(0, n_full, sum_body, 0)
    add_pv(p_sc[(n_full + 1) % 2], jnp.maximum(n_full - 1, 0))
    add_pv(probs(causal(scores(n_full))), n_full)
    return acc_sc[...] / jnp.sum(l_sc[...], axis=1, keepdims=True)


def _top_select(v, lane, forced, rounds):
    sel = forced
    lane = lane.astype(F32)
    for _ in range(rounds):
        mx = jnp.max(v, axis=1, keepdims=True)
        idx = jnp.min(jnp.where(v == mx, lane, 1e9), axis=1, keepdims=True)
        pick = (lane == idx) & (mx > NEG_INF)
        sel = sel | pick
        v = jnp.where(pick, NEG_INF, v)
    return sel


def _masked_softmax(s, mask):
    s = jnp.where(mask, s, NEG_INF)
    m = jnp.max(s, axis=1, keepdims=True)
    m = jnp.where(m > NEG_INF, m, 0.0)
    e = jnp.where(mask, jnp.exp(s - m), 0.0)
    d = jnp.sum(e, axis=1, keepdims=True)
    return e / jnp.where(d > 0, d, 1.0)


def _moba_prompt_kernel(q_ref, km_ref, k_ref, v_ref, o_ref, m_sc, l_sc, acc_sc, q3_sc, p_sc):
    i = pl.program_id(1)
    rows = 2 * MOBA_TQ
    q = q_ref[...].reshape(rows, LANES)
    gate = _dot(q, km_ref[0])
    lane = lax.broadcasted_iota(jnp.int32, (rows, LANES), 1)
    qpos = i * MOBA_TQ + lax.broadcasted_iota(jnp.int32, (rows, 1), 0) % MOBA_TQ
    cur = qpos // A_BLOCK
    past = (lane >= HEAD_DIM) & (lane < HEAD_DIM + cur)
    sel = _top_select(jnp.where(past, gate, NEG_INF), lane, lane == HEAD_DIM + cur, A_TOPK)
    bias = jnp.where(sel | (lane < HEAD_DIM), 0.0, MASK_BIAS)
    q3_sc[0] = (q.astype(F32) * SCALE + bias).astype(BF16)

    o = _two_pass_attention(q3_sc, k_ref, v_ref, i, None, qpos, m_sc, l_sc, acc_sc, p_sc)
    o_ref[:, 0:HEAD_DIM] = o[:MOBA_TQ]
    o_ref[:, HEAD_DIM:LANES] = o[MOBA_TQ:]


def _gate_matrix(kmean):
    nblk = kmean.shape[0]
    km = kmean.reshape(nblk, A_KV_HEADS, HEAD_DIM).transpose(1, 2, 0)
    km = jnp.pad(km, ((0, 0), (0, LANES - HEAD_DIM), (HEAD_DIM, LANES - HEAD_DIM - nblk)))
    return km.astype(BF16)


def _moba_prompt(qa, km, kaug, va):
    t = qa.shape[1]
    nt = t // MOBA_TQ
    assert t // A_BLOCK <= HEAD_DIM, "key-block one-hot occupies 64 lanes"
    assert t % MOBA_TQ == 0 and MOBA_TQ % A_BLOCK == 0
    rows = 2 * MOBA_TQ
    return pl.pallas_call(
        _moba_prompt_kernel,
        out_shape=jax.ShapeDtypeStruct((t, A_HEADS * HEAD_DIM), F32),
        grid=(A_KV_HEADS, nt),
        in_specs=[pl.BlockSpec((2, MOBA_TQ, LANES), lambda g, i: (g, i, 0)),
                  pl.BlockSpec((1, LANES, LANES), lambda g, i: (g, 0, 0)),
                  pl.BlockSpec((1, t, LANES), lambda g, i: (g, 0, 0)),
                  pl.BlockSpec((1, t, HEAD_DIM), lambda g, i: (g, 0, 0))],
        out_specs=pl.BlockSpec((MOBA_TQ, LANES), lambda g, i: (i, g)),
        scratch_shapes=[pltpu.VMEM((rows, LANES), F32), pltpu.VMEM((rows, LANES), F32),
                        pltpu.VMEM((rows, HEAD_DIM), F32), pltpu.VMEM((1, rows, LANES), BF16),
                        pltpu.VMEM((2, rows, ATT_KT), BF16)],
        compiler_params=_cparams("parallel", "parallel"),
        name="moba_prompt",
    )(qa, km, kaug, va)


def _compress_kernel(x_ref, pelo_ref, pehi_ref, wlo_ref, whi_ref, b1_ref, w2_ref, o_ref, hi_sc, *, nch):
    x = x_ref[0]
    lo = _dot((x + pelo_ref[0]).astype(BF16), wlo_ref[0])
    hi_sc[0:nch] = _dot((x + pehi_ref[0]).astype(BF16), whi_ref[0])
    hi_sc[nch:nch + 8] = jnp.zeros((8, CMP_HIDDEN), F32)
    hid = _gelu(lo + hi_sc[pl.ds(1, nch), :] + b1_ref[0])
    o_ref[0] = _dot(hid.astype(BF16), w2_ref[0]).astype(o_ref.dtype)


def _compress_weights(cmp_pe, cmp_w1, cmp_b1, cmp_w2):
    flat = CMP_STRIDE * HEAD_DIM
    pelo = cmp_pe[:, :CMP_STRIDE].reshape(2, 1, flat)
    pehi = cmp_pe[:, CMP_STRIDE:].reshape(2, 1, flat)
    wlo = cmp_w1[:, :CMP_STRIDE].reshape(2, flat, CMP_HIDDEN).astype(BF16)
    whi = cmp_w1[:, CMP_STRIDE:].reshape(2, flat, CMP_HIDDEN).astype(BF16)
    return pelo, pehi, wlo, whi, cmp_b1.reshape(2, 1, CMP_HIDDEN), cmp_w2.astype(BF16)


def _compress(xch, cw):
    _, nch, flat = xch.shape
    pelo, pehi, wlo, whi, b1, w2 = cw
    kind = lambda shape: pl.BlockSpec((1,) + shape, lambda j: (j // B_KV_HEADS, 0, 0))
    return pl.pallas_call(
        functools.partial(_compress_kernel, nch=nch),
        out_shape=jax.ShapeDtypeStruct((4, nch, HEAD_DIM), BF16),
        grid=(4,),
        in_specs=[pl.BlockSpec((1, nch, flat), lambda j: (j, 0, 0)),
                  kind((1, flat)), kind((1, flat)), kind((flat, CMP_HIDDEN)), kind((flat, CMP_HIDDEN)),
                  kind((1, CMP_HIDDEN)), kind((CMP_HIDDEN, HEAD_DIM))],
        out_specs=pl.BlockSpec((1, nch, HEAD_DIM), lambda j: (j, 0, 0)),
        scratch_shapes=[pltpu.VMEM((nch + 8, CMP_HIDDEN), F32)],
        compiler_params=_cparams("parallel"),
        name="compress",
    )(xch, pelo, pehi, wlo, whi, b1, w2)


def _importance_matrix(ncmp, nsel):
    ratio = SEL_BLOCK // CMP_STRIDE
    n = np.arange(ncmp)[:, None]
    j = np.arange(nsel)[None, :]
    own = (n // ratio == j)
    last = (n % ratio == ratio - 1)
    m = np.where(own & ~last, 1.0, 0.0) + np.where(last & (own | (n // ratio == j - 1)), 0.5, 0.0)
    return jnp.asarray(m, BF16)


NSA_TQ = 256
WIN_BLOCKS = WINDOW // NSA_TQ + 1


def _nsa_prompt_kernel(*refs, nsel, ncmp):
    (qb_ref, qbr_ref, kc_ref, vc_ref, imp_ref, ks_ref, vs_ref) = refs[:7]
    kw_refs = refs[7:7 + WIN_BLOCKS]
    vw_refs = refs[7 + WIN_BLOCKS:7 + 2 * WIN_BLOCKS]
    g_ref, o_ref, m_sc, l_sc, acc_sc, q3_sc, p_sc = refs[7 + 2 * WIN_BLOCKS:]
    i = pl.program_id(1)
    rows = B_GROUP * NSA_TQ
    q0 = i * NSA_TQ
    q = qb_ref[...].reshape(rows, HEAD_DIM)
    qr = qbr_ref[...].reshape(rows, LANES)

    s = _dot_t(q, kc_ref[0])
    qpos_c = q0 + lax.broadcasted_iota(jnp.int32, (rows, ncmp), 0) % NSA_TQ
    n_c = lax.broadcasted_iota(jnp.int32, (rows, ncmp), 1)
    p = _masked_softmax(s, n_c * CMP_STRIDE + (CMP_LEN - 1) <= qpos_c)
    o_cmp = _dot(p.astype(BF16), vc_ref[0])

    psum = p[0:NSA_TQ]
    for j in range(1, B_GROUP):
        psum = psum + p[j * NSA_TQ:(j + 1) * NSA_TQ]
    p_hi = psum.astype(BF16)
    p_lo = (psum - p_hi.astype(F32)).astype(BF16)
    imp = _dot(p_hi, imp_ref[...]) + _dot(p_lo, imp_ref[...])
    blk = lax.broadcasted_iota(jnp.int32, (NSA_TQ, nsel), 1)
    cur = (q0 + lax.broadcasted_iota(jnp.int32, (NSA_TQ, nsel), 0)) // SEL_BLOCK
    forced = (blk == 0) | (blk == cur) | (blk == cur - 1)
    cand = jnp.where(blk > cur, NEG_INF, jnp.where(forced, jnp.inf, imp))
    sel = _top_select(cand, blk, jnp.zeros((NSA_TQ, nsel), jnp.bool_), min(SEL_TOPN, nsel))
    selbias = jnp.where(sel, 0.0, MASK_BIAS)

    kt_last = (q0 + NSA_TQ - 1) // ATT_KT
    tiles_per_win = HEAD_DIM * SEL_BLOCK // ATT_KT
    qr32 = qr.astype(F32)
    for w in range(-(-nsel // HEAD_DIM)):
        nb = min(HEAD_DIM, nsel - w * HEAD_DIM)
        pieces = [jnp.zeros((NSA_TQ, HEAD_DIM), F32), selbias[:, w * HEAD_DIM:w * HEAD_DIM + nb]]
        if nb < HEAD_DIM:
            pieces.append(jnp.zeros((NSA_TQ, HEAD_DIM - nb), F32))
        bias_w = jnp.concatenate(pieces, axis=1)
        q3_sc[w] = (qr32 + jnp.concatenate([bias_w] * B_GROUP, axis=0)).astype(BF16)

    qpos_col = q0 + lax.broadcasted_iota(jnp.int32, (rows, 1), 0) % NSA_TQ
    o_sel = _two_pass_attention(q3_sc, ks_ref, vs_ref, kt_last, tiles_per_win, qpos_col,
                                m_sc, l_sc, acc_sc, p_sc)

    kband = jnp.concatenate([r[0] for r in kw_refs], axis=0)
    vband = jnp.concatenate([r[0] for r in vw_refs], axis=0)
    nband = WIN_BLOCKS * NSA_TQ
    s = _dot_t(qr[:, :HEAD_DIM], kband)
    qpos = q0 + lax.broadcasted_iota(jnp.int32, (rows, nband), 0) % NSA_TQ
    kpos = q0 - WINDOW + lax.broadcasted_iota(jnp.int32, (rows, nband), 1)
    p = _masked_softmax(s, (kpos <= qpos) & (kpos >= qpos - WINDOW) & (kpos >= 0))
    o_win = _dot(p.astype(BF16), vband)

    gts = g_ref[0]
    for j in range(B_GROUP):
        sl = slice(j * NSA_TQ, (j + 1) * NSA_TQ)
        o = (gts[:, 3 * j:3 * j + 1] * o_cmp[sl] + gts[:, 3 * j + 1:3 * j + 2] * o_sel[sl]
             + gts[:, 3 * j + 2:3 * j + 3] * o_win[sl])
        o_ref[:, j * HEAD_DIM:(j + 1) * HEAD_DIM] = o


def _nsa_prompt(qb, qbr, kvcmp, ksaug, vs, kw, vw, gates):
    t = qb.shape[1]
    nt = t // NSA_TQ
    nsel = t // SEL_BLOCK
    ncmp = kvcmp.shape[1]
    rows = B_GROUP * NSA_TQ
    impm = _importance_matrix(ncmp, nsel)
    res = lambda w: pl.BlockSpec((1, t, w), lambda g, i: (g, 0, 0))
    band = [pl.BlockSpec((1, NSA_TQ, HEAD_DIM),
                         functools.partial(lambda g, i, j: (g, jnp.maximum(i - (WIN_BLOCKS - 1) + j, 0), 0), j=j))
            for j in range(WIN_BLOCKS)]
    return pl.pallas_call(
        functools.partial(_nsa_prompt_kernel, nsel=nsel, ncmp=ncmp),
        out_shape=jax.ShapeDtypeStruct((t, B_HEADS * HEAD_DIM), F32),
        grid=(B_KV_HEADS, nt),
        in_specs=[pl.BlockSpec((B_GROUP, NSA_TQ, HEAD_DIM), lambda g, i: (g, i, 0)),
                  pl.BlockSpec((B_GROUP, NSA_TQ, LANES), lambda g, i: (g, i, 0)),
                  pl.BlockSpec((1, ncmp, HEAD_DIM), lambda g, i: (g, 0, 0)),
                  pl.BlockSpec((1, ncmp, HEAD_DIM), lambda g, i: (B_KV_HEADS + g, 0, 0)),
                  pl.BlockSpec(impm.shape, lambda g, i: (0, 0)),
                  res(LANES), res(HEAD_DIM)] + band + band +
                 [pl.BlockSpec((1, NSA_TQ, 3 * B_GROUP), lambda g, i: (g, i, 0))],
        out_specs=pl.BlockSpec((NSA_TQ, B_GROUP * HEAD_DIM), lambda g, i: (i, g)),
        scratch_shapes=[pltpu.VMEM((rows, LANES), F32), pltpu.VMEM((rows, LANES), F32),
                        pltpu.VMEM((rows, HEAD_DIM), F32),
                        pltpu.VMEM((-(-nsel // HEAD_DIM), rows, LANES), BF16),
                        pltpu.VMEM((2, rows, ATT_KT), BF16)],
        compiler_params=_cparams("parallel", "parallel"),
        name="nsa_prompt",
    )(qb, qbr, kvcmp, kvcmp, impm, ksaug, vs, *([kw] * WIN_BLOCKS), *([vw] * WIN_BLOCKS), gates)


def _top_indices(v, lane, rounds):
    out = jnp.zeros((v.shape[0], LANES), jnp.int32)
    slot = lax.broadcasted_iota(jnp.int32, out.shape, 1)
    for r in range(rounds):
        mx = jnp.max(v, axis=1, keepdims=True)
        idx = jnp.min(jnp.where(v == mx, lane, 1 << 20), axis=1, keepdims=True)
        out = jnp.where(slot == r, idx, out)
        v = jnp.where(lane == idx, NEG_INF, v)
    return out


def _bf16_round(x):
    return x.astype(BF16).astype(F32)


def _moba_kmean_kernel(pt_ref, *refs, pg):
    pages, o_ref = refs[:pg], refs[pg]
    step = pl.program_id(1)
    per_blk = A_BLOCK // pages[0].shape[2]
    nb = pg // per_blk

    @pl.when(step == 0)
    def _():
        o_ref[...] = jnp.zeros(o_ref.shape, F32)

    lane = lax.broadcasted_iota(jnp.int32, o_ref.shape[1:], 1)
    out = o_ref[0]
    for j in range(nb):
        tot = pages[j * per_blk][0]
        for r in pages[j * per_blk + 1:(j + 1) * per_blk]:
            tot = tot + r[0]
        mean = jnp.sum(tot, axis=1, keepdims=True) * (1.0 / A_BLOCK)
        out = jnp.where(lane == step * nb + j, mean, out)
    o_ref[0] = out


def _moba_kmean(cache_t, page_table, *, pg):
    db, npg = page_table.shape
    page = cache_t.shape[2]
    kw = A_KV_HEADS * HEAD_DIM
    assert npg * page // A_BLOCK <= LANES
    specs = [pl.BlockSpec((1, kw, page), functools.partial(lambda b, s, pt, j: (pt[b, s * pg + j], 0, 0), j=j))
             for j in range(pg)]
    return pl.pallas_call(
        functools.partial(_moba_kmean_kernel, pg=pg),
        out_shape=jax.ShapeDtypeStruct((db, kw, LANES), F32),
        grid_spec=pltpu.PrefetchScalarGridSpec(
            num_scalar_prefetch=1, grid=(db, npg // pg), in_specs=specs,
            out_specs=pl.BlockSpec((1, kw, LANES), lambda b, s, pt: (b, 0, 0))),
        compiler_params=_cparams("parallel", "arbitrary"),
        name="moba_kmean",
    )(page_table, *([cache_t] * pg))


def _moba_gate_kernel(q_ref, km_ref, idx_ref, *, nblk):
    q = q_ref[0][:, :HEAD_DIM].astype(BF16)
    head = lax.broadcasted_iota(jnp.int32, (A_HEADS, LANES), 0)
    lane = lax.broadcasted_iota(jnp.int32, (A_HEADS, LANES), 1)
    gate = jnp.zeros((A_HEADS, LANES), F32)
    for g in range(A_KV_HEADS):
        kmg = km_ref[0][g * HEAD_DIM:(g + 1) * HEAD_DIM, :].astype(BF16)
        gate = jnp.where(head // (A_HEADS // A_KV_HEADS) == g, _dot(q, kmg), gate)
    idx_ref[0] = _top_indices(jnp.where(lane < nblk, gate, NEG_INF), lane, A_TOPK)


def _moba_gate(qa_rows, kmean_t, *, nblk):
    db, kw, _ = kmean_t.shape
    return pl.pallas_call(
        functools.partial(_moba_gate_kernel, nblk=nblk),
        out_shape=jax.ShapeDtypeStruct((db, A_HEADS, LANES), jnp.int32),
        grid=(db,),
        in_specs=[pl.BlockSpec((1, A_HEADS, LANES), lambda b: (b, 0, 0)),
                  pl.BlockSpec((1, kw, LANES), lambda b: (b, 0, 0))],
        out_specs=pl.BlockSpec((1, A_HEADS, LANES), lambda b: (b, 0, 0)),
        compiler_params=_cparams("parallel"),
        name="moba_gate",
    )(qa_rows, kmean_t)


def _gathered_attention(q, k_pages, v_pages, masks, k_own, v_own):
    qb = q.astype(BF16)
    ss = [_dot(qb, k.astype(BF16)) for k in k_pages]
    ss = [s if mk is None else jnp.where(mk, s, NEG_INF) for s, mk in zip(ss, masks)]
    s_own = jnp.sum(q * _bf16_round(k_own), axis=1, keepdims=True)
    m = s_own
    for s in ss:
        m = jnp.maximum(m, jnp.max(s, axis=1, keepdims=True))
    e_own = jnp.exp(s_own - m)
    d = e_own
    acc = _bf16_round(e_own) * _bf16_round(v_own)
    for s, v in zip(ss, v_pages):
        e = jnp.exp(s - m)
        d = d + jnp.sum(e, axis=1, keepdims=True)
        acc = acc + _dot_t(e.astype(BF16), v.astype(BF16))
    return acc / d


def _moba_sample_kernel(pt_ref, ix_ref, q_ref, *refs, npick):
    k_refs, v_refs = refs[:npick], refs[npick:2 * npick]
    kn_ref, vn_ref, o_ref, o_sc = refs[2 * npick:]
    h = pl.program_id(1)
    q = q_ref[0][:, :HEAD_DIM] * SCALE
    o_sc[...] = _gathered_attention(q, [r[0] for r in k_refs], [r[0] for r in v_refs], [None] * npick,
                                    kn_ref[0], vn_ref[0])
    o_ref[0, pl.ds(h, 1), :] = o_sc[pl.ds(h, 1), :]


def _moba_sample(cache_t, page_table, idx, qa_rows, k_new, v_new):
    db = page_table.shape[0]
    page = cache_t.shape[2]
    npage = A_BLOCK // page
    rep = A_HEADS // A_KV_HEADS

    def kv_spec(s, j, row_blk0):
        def imap(b, h, pt, ix):
            blk = ix[(b * A_HEADS + h) * A_TOPK + s]
            return (pt[b, blk * npage + j], row_blk0 + h // rep, 0)
        return pl.BlockSpec((1, HEAD_DIM, page), imap)

    picks = [(s, j) for s in range(A_TOPK) for j in range(npage)]
    per_seq = lambda w: pl.BlockSpec((1, A_HEADS, w), lambda b, h, pt, ix: (b, 0, 0))
    return pl.pallas_call(
        functools.partial(_moba_sample_kernel, npick=len(picks)),
        out_shape=jax.ShapeDtypeStruct((db, A_HEADS, HEAD_DIM), F32),
        grid_spec=pltpu.PrefetchScalarGridSpec(
            num_scalar_prefetch=2, grid=(db, A_HEADS),
            in_specs=[per_seq(LANES)]
                     + [kv_spec(s, j, 0) for s, j in picks] + [kv_spec(s, j, A_KV_HEADS) for s, j in picks]
                     + [per_seq(HEAD_DIM), per_seq(HEAD_DIM)],
            out_specs=per_seq(HEAD_DIM),
            scratch_shapes=[pltpu.VMEM((A_HEADS, HEAD_DIM), F32)]),
        compiler_params=_cparams("parallel", "arbitrary"),
        name="moba_sample",
    )(page_table, idx, qa_rows, *([cache_t] * (2 * len(picks))), k_new, v_new)


def _nsa_flatten_kernel(pt_ref, *refs, pg, nch):
    pages = refs[:pg]
    pelo_ref, pehi_ref, wlo_ref, whi_ref, b1_ref, w2_ref, o_ref, x_sc, hi_sc, rows_sc = refs[pg:]
    step = pl.program_id(1)
    page = pages[0].shape[2]
    nc = pg * page // CMP_STRIDE
    base = pl.multiple_of(step * nc, nc)
    lane = lax.broadcasted_iota(jnp.int32, (nc, LANES), 1)
    lo_half = lane < HEAD_DIM
    for j, r in enumerate(pages):
        for pr in range(2):
            rows_sc[pr, j * page:(j + 1) * page, :] = r[0, pr * LANES:(pr + 1) * LANES, :].T
    for u in range(CMP_STRIDE // 2):
        for pr in range(2):
            ap = rows_sc[pr, pl.ds(2 * u, nc, stride=CMP_STRIDE), :]
            bp = rows_sc[pr, pl.ds(2 * u + 1, nc, stride=CMP_STRIDE), :]
            x_sc[2 * pr, pl.ds(base, nc), u * LANES:(u + 1) * LANES] = jnp.where(
                lo_half, ap, pltpu.roll(bp, HEAD_DIM, 1))
            x_sc[2 * pr + 1, pl.ds(base, nc), u * LANES:(u + 1) * LANES] = jnp.where(
                lo_half, pltpu.roll(ap, HEAD_DIM, 1), bp)

    @pl.when(step == pl.num_programs(1) - 1)
    def _():
        hi_sc[nch:nch + 8] = jnp.zeros((8, CMP_HIDDEN), F32)
        for j in range(4):
            c = j // B_KV_HEADS
            x = x_sc[j]
            lo = _dot((x + pelo_ref[c]).astype(BF16), wlo_ref[c])
            hi_sc[0:nch] = _dot((x + pehi_ref[c]).astype(BF16), whi_ref[c])
            hid = _gelu(lo + hi_sc[pl.ds(1, nch), :] + b1_ref[c])
            o_ref[0, j] = _dot(hid.astype(BF16), w2_ref[c]).astype(o_ref.dtype)


def _nsa_sample_compress(cache_t, page_table, cw, *, pg):
    db, npg = page_table.shape
    page = cache_t.shape[2]
    assert page == LANES
    nch = npg * page // CMP_STRIDE
    flat = CMP_STRIDE * HEAD_DIM
    pelo, pehi, wlo, whi, b1, w2 = cw
    full = lambda a: pl.BlockSpec(a.shape, lambda b, s, pt: (0,) * a.ndim)
    specs = [pl.BlockSpec((1, 2 * LANES, page), functools.partial(lambda b, s, pt, j: (pt[b, s * pg + j], 0, 0), j=j))
             for j in range(pg)]
    return pl.pallas_call(
        functools.partial(_nsa_flatten_kernel, pg=pg, nch=nch),
        out_shape=jax.ShapeDtypeStruct((db, 4, nch, HEAD_DIM), BF16),
        grid_spec=pltpu.PrefetchScalarGridSpec(
            num_scalar_prefetch=1, grid=(db, npg // pg),
            in_specs=specs + [full(a) for a in (pelo, pehi, wlo, whi, b1, w2)],
            out_specs=pl.BlockSpec((1, 4, nch, HEAD_DIM), lambda b, s, pt: (b, 0, 0, 0)),
            scratch_shapes=[pltpu.VMEM((4, nch, flat), F32), pltpu.VMEM((nch + 8, CMP_HIDDEN), F32),
                            pltpu.VMEM((2, pg * page, LANES), F32)]),
        compiler_params=_cparams("parallel", "arbitrary"),
        name="nsa_sample_compress",
    )(page_table, *([cache_t] * pg), pelo, pehi, wlo, whi, b1, w2)


def _rows_by_group(per_group):
    head = lax.broadcasted_iota(jnp.int32, per_group[0].shape, 0)
    out = per_group[0]
    for g in range(1, B_KV_HEADS):
        out = jnp.where(head // B_GROUP == g, per_group[g], out)
    return out


def _nsa_sample_select_kernel(q_ref, qr_ref, kv_ref, imp_ref, win_ref, wnew_ref, g_ref, idx_ref, ocw_ref,
                              *, past, nsel_past, ncmp):
    q = q_ref[0][:, :HEAD_DIM].astype(BF16)
    qr = qr_ref[0][:, :HEAD_DIM]
    gts = g_ref[0]
    glane = lax.broadcasted_iota(jnp.int32, (B_HEADS, LANES), 1)
    ghead = lax.broadcasted_iota(jnp.int32, (B_HEADS, LANES), 0)
    gate = lambda c: jnp.sum(jnp.where(glane == 3 * ghead + c, gts, 0.0), axis=1, keepdims=True)

    s = _rows_by_group([_dot_t(q, kv_ref[0, g]) for g in range(B_KV_HEADS)])
    n_c = lax.broadcasted_iota(jnp.int32, (B_HEADS, ncmp), 1)
    p = _masked_softmax(s, n_c * CMP_STRIDE + (CMP_LEN - 1) <= past)
    o_cmp = _rows_by_group([_dot(p.astype(BF16), kv_ref[0, B_KV_HEADS + g]) for g in range(B_KV_HEADS)])

    head = lax.broadcasted_iota(jnp.int32, (B_HEADS, ncmp), 0)
    psum = jnp.zeros((B_HEADS, ncmp), F32)
    for g in range(B_KV_HEADS):
        tot = jnp.sum(jnp.where(head // B_GROUP == g, p, 0.0), axis=0, keepdims=True)
        psum = jnp.where(head == g, tot, psum)
    p_hi = psum.astype(BF16)
    p_lo = (psum - p_hi.astype(F32)).astype(BF16)
    imp = _dot(p_hi, imp_ref[...]) + _dot(p_lo, imp_ref[...])
    blk = lax.broadcasted_iota(jnp.int32, (B_HEADS, nsel_past), 1)
    cand = jnp.where((blk == 0) | (blk == nsel_past - 1), jnp.inf, imp)
    idx_ref[0] = _top_indices(cand, blk, min(SEL_TOPN, nsel_past + 1) - 1)

    wk = [win_ref[0][g * HEAD_DIM:(g + 1) * HEAD_DIM, :] for g in range(B_KV_HEADS)]
    wv = [win_ref[0][(B_KV_HEADS + g) * HEAD_DIM:(B_KV_HEADS + g + 1) * HEAD_DIM, :] for g in range(B_KV_HEADS)]
    nk = [wnew_ref[0][:, g * HEAD_DIM:(g + 1) * HEAD_DIM] for g in range(B_KV_HEADS)]
    nv = [wnew_ref[0][:, (B_KV_HEADS + g) * HEAD_DIM:(B_KV_HEADS + g + 1) * HEAD_DIM] for g in range(B_KV_HEADS)]
    s_w = _rows_by_group([_dot(qr.astype(BF16), k.astype(BF16)) for k in wk])
    s_n = _rows_by_group([jnp.sum(qr * _bf16_round(k), axis=1, keepdims=True) for k in nk])
    m = jnp.maximum(jnp.max(s_w, axis=1, keepdims=True), s_n)
    e_w, e_n = jnp.exp(s_w - m), jnp.exp(s_n - m)
    d = jnp.sum(e_w, axis=1, keepdims=True) + e_n
    o_win = _rows_by_group([_dot_t(e_w.astype(BF16), v.astype(BF16)) + _bf16_round(e_n) * _bf16_round(nvg)
                            for v, nvg in zip(wv, nv)]) / d

    ocw_ref[0] = jnp.concatenate([gate(0) * o_cmp, gate(2) * o_win], axis=1)


def _nsa_sample_select(qb_rows, qbr_rows, kvcmp, win_cache, win_new, gates, *, past):
    db, _, ncmp, _ = kvcmp.shape
    nsel_past = past // SEL_BLOCK
    impm = _importance_matrix(ncmp, nsel_past)
    row3 = lambda a: pl.BlockSpec((1,) + a.shape[1:], lambda b: (b,) + (0,) * (a.ndim - 1))
    kern = functools.partial(_nsa_sample_select_kernel, past=past, nsel_past=nsel_past, ncmp=ncmp)
    return pl.pallas_call(
        kern,
        out_shape=(jax.ShapeDtypeStruct((db, B_HEADS, LANES), jnp.int32),
                   jax.ShapeDtypeStruct((db, B_HEADS, LANES), F32)),
        grid=(db,),
        in_specs=[row3(qb_rows), row3(qbr_rows), row3(kvcmp), pl.BlockSpec(impm.shape, lambda b: (0, 0)),
                  row3(win_cache), row3(win_new), row3(gates)],
        out_specs=(pl.BlockSpec((1, B_HEADS, LANES), lambda b: (b, 0, 0)),
                   pl.BlockSpec((1, B_HEADS, LANES), lambda b: (b, 0, 0))),
        compiler_params=_cparams("parallel"),
        name="nsa_sample_select",
    )(qb_rows, qbr_rows, kvcmp, impm, win_cache, win_new, gates)


def _nsa_sample_attend_kernel(pt_ref, ix_ref, qr_ref, *refs, nslot, per_page):
    k_refs, v_refs = refs[:nslot], refs[nslot:2 * nslot]
    kn_ref, vn_ref, g_ref, ocw_ref, o_ref = refs[2 * nslot:]
    b, g = pl.program_id(0), pl.program_id(1)
    qs = qr_ref[0, pl.ds(g * B_GROUP, B_GROUP), :][:, :HEAD_DIM]

    page = k_refs[0].shape[2]
    lane = lax.broadcasted_iota(jnp.int32, (B_GROUP, page), 1)
    masks = [lane // SEL_BLOCK == ix_ref[(b * B_KV_HEADS + g) * nslot + s] % per_page for s in range(nslot)]
    o_sel = _gathered_attention(qs, [r[0] for r in k_refs], [r[0] for r in v_refs], masks,
                                kn_ref[0, 0], vn_ref[0, 0])
    gts = g_ref[0]
    hrow = lax.broadcasted_iota(jnp.int32, (B_GROUP, LANES), 0) + g * B_GROUP
    glane = lax.broadcasted_iota(jnp.int32, (B_GROUP, LANES), 1)
    g1 = jnp.sum(jnp.where(glane == 3 * hrow + 1, gts, 0.0), axis=1, keepdims=True)
    ocw = ocw_ref[0, pl.ds(g * B_GROUP, B_GROUP), :]
    o_ref[0, pl.ds(g * B_GROUP, B_GROUP), :] = g1 * o_sel + ocw[:, :HEAD_DIM] + ocw[:, HEAD_DIM:]


def _nsa_sample_attend(cache_t, page_table, idx, qbr_rows, ks_new, vs_new, gates, ocw, *, nslot):
    db = page_table.shape[0]
    page = cache_t.shape[2]
    per_page = page // SEL_BLOCK

    def kv_spec(s, row_blk0):
        def imap(b, g, pt, ix):
            blk = ix[(b * B_KV_HEADS + g) * nslot + s]
            return (pt[b, blk // per_page], row_blk0 + g, 0)
        return pl.BlockSpec((1, HEAD_DIM, page), imap)

    row = lambda a: pl.BlockSpec((1,) + a.shape[1:], lambda b, g, pt, ix: (b,) + (0,) * (a.ndim - 1))
    new = pl.BlockSpec((1, 1, 1, HEAD_DIM), lambda b, g, pt, ix: (b, g, 0, 0))
    return pl.pallas_call(
        functools.partial(_nsa_sample_attend_kernel, nslot=nslot, per_page=per_page),
        out_shape=jax.ShapeDtypeStruct((db, B_HEADS, HEAD_DIM), F32),
        grid_spec=pltpu.PrefetchScalarGridSpec(
            num_scalar_prefetch=2, grid=(db, B_KV_HEADS),
            in_specs=[row(qbr_rows)] + [kv_spec(s, 2 * B_KV_HEADS) for s in range(nslot)]
                     + [kv_spec(s, 3 * B_KV_HEADS) for s in range(nslot)] + [new, new, row(gates), row(ocw)],
            out_specs=pl.BlockSpec((1, B_HEADS, HEAD_DIM), lambda b, g, pt, ix: (b, 0, 0))),
        compiler_params=_cparams("parallel", "arbitrary"),
        name="nsa_sample_attend",
    )(page_table, idx, qbr_rows, *([cache_t] * (2 * nslot)), ks_new, vs_new, gates, ocw)


def _prep_weights(l, g_mix_pre, w_in, b_gate, cmp_pe, cmp_w1, cmp_b1, cmp_w2, w_out, g_mix_post,
                  g_ffn_pre, w_up, w_conv, b_conv, w_down, g_ffn_post):
    d_in = w_in.shape[-1]
    n_gate = b_gate.shape[-1]
    return dict(
        g_mix_pre=g_mix_pre[l][None],
        w_in=jnp.pad(w_in[l], ((0, 0), (0, D_IN_PAD - d_in))).astype(BF16),
        b_gate=jnp.pad(b_gate[l], (0, LANES - n_gate))[None],
        cmp=_compress_weights(cmp_pe[l], cmp_w1[l], cmp_b1[l], cmp_w2[l]),
        w_out=w_out[l].astype(BF16), g_mix_post=g_mix_post[l][None], g_ffn_pre=g_ffn_pre[l][None],
        w_up=w_up[l].astype(BF16), w_conv=w_conv[l], b_conv=b_conv[l][None],
        w_down=w_down[l].astype(BF16), g_ffn_post=g_ffn_post[l][None],
    )


def _group_gates(gates):
    t = gates.shape[0]
    return gates[:, :3 * B_HEADS].reshape(t, B_KV_HEADS, 3 * B_GROUP).transpose(1, 0, 2)


FFN_CK = 256
FFN_TM = 1024


def _prompt_layer(x2d, w):
    t = x2d.shape[0]
    (kva, nsa, win, gates, qa, kaug, va, qb, qbr, ksaug, vs, kw, vw, kcvc, kmean) = _inproj(
        x2d, jnp.arange(t), w["g_mix_pre"], w["w_in"], w["b_gate"], tm=512, with_kmean=True)
    o_a = _moba_prompt(qa, _gate_matrix(kmean.reshape(t // A_BLOCK, A_KV_HEADS * HEAD_DIM)), kaug, va)
    kvcmp = _compress(kcvc.reshape(4, t // CMP_STRIDE, CMP_STRIDE * HEAD_DIM), w["cmp"])
    o_b = _nsa_prompt(qb, qbr, kvcmp, ksaug, vs, kw, vw, _group_gates(gates))
    x1, h2 = _outproj(o_a, o_b, x2d, w["w_out"], w["g_mix_post"], w["g_ffn_pre"], tm=512)
    y, conv_state = _ffn_seq(h2, x1, w["w_up"], w["w_conv"], w["b_conv"], w["w_down"], w["g_ffn_post"],
                             tm=FFN_TM, ck=FFN_CK)
    keep = min(WINDOW, t)
    return (y, kva.reshape(t, 2, A_KV_HEADS, HEAD_DIM), nsa.reshape(t, 4, B_KV_HEADS, HEAD_DIM),
            win[t - keep:].reshape(keep, 2, B_KV_HEADS, HEAD_DIM), conv_state)


PAGES_PER_STEP = 16


def _sample_layer(x2d, cache_moba, cache_nsa, win_cache, conv_state, page_table, w):
    db = x2d.shape[0]
    n_pool, page = cache_moba.shape[:2]
    npg = page_table.shape[1]
    past = npg * page
    wb = win_cache.shape[1]
    assert wb == WINDOW and past % A_BLOCK == 0 and past // A_BLOCK >= A_TOPK and A_BLOCK % page == 0
    assert page % SEL_BLOCK == 0
    pg = min(PAGES_PER_STEP, npg)
    (kva, nsa, win, gates, qa, _, _, qb, qbr, _, _, _, _, _, _) = _inproj(
        x2d, jnp.full((db,), past, jnp.int32), w["g_mix_pre"], w["w_in"], w["b_gate"], tm=db, with_kmean=False)
    rows = lambda a: a.transpose(1, 0, 2).astype(F32)
    qa_rows, qb_rows, qbr_rows = rows(qa), rows(qb), rows(qbr)

    tpose = lambda c: c.transpose(0, 2, 3, 4, 1).reshape(c.shape[0], -1, c.shape[1])
    moba_t, nsa_t, win_t = tpose(cache_moba), tpose(cache_nsa), tpose(win_cache)

    kmean_t = _moba_kmean(moba_t, page_table, pg=pg)
    idx_a = _moba_gate(qa_rows, kmean_t, nblk=past // A_BLOCK)[:, :, :A_TOPK].reshape(-1)
    rep = A_HEADS // A_KV_HEADS
    per_head = lambda a: jnp.repeat(a.reshape(db, A_KV_HEADS, HEAD_DIM), rep, axis=1)
    o_a = _moba_sample(moba_t, page_table, idx_a, qa_rows, per_head(kva[:, :256]), per_head(kva[:, 256:]))

    kvcmp = _nsa_sample_compress(nsa_t, page_table, w["cmp"], pg=pg)
    gates3 = gates[:, None, :]
    win2d = win_cache.reshape(db, wb, 2 * B_KV_HEADS * HEAD_DIM)
    win_new = win[:, None, :]
    idx_b, ocw = _nsa_sample_select(qb_rows, qbr_rows, kvcmp, win_t, win_new, gates3, past=past)
    nslot = min(SEL_TOPN, past // SEL_BLOCK + 1) - 1
    per_group = lambda a: a.reshape(db, B_KV_HEADS, 1, HEAD_DIM)
    o_b = _nsa_sample_attend(nsa_t, page_table, idx_b[:, :B_KV_HEADS, :nslot].reshape(-1), qbr_rows,
                             per_group(nsa[:, 256:384]), per_group(nsa[:, 384:]), gates3, ocw, nslot=nslot)

    x1, h2 = _outproj(o_a.reshape(db, -1), o_b.reshape(db, -1), x2d, w["w_out"], w["g_mix_post"],
                      w["g_ffn_pre"], tm=db)
    y, conv_new = _ffn_step(h2, x1, conv_state, w["w_up"], w["w_conv"], w["b_conv"], w["w_down"],
                            w["g_ffn_post"], ck=FFN_CK)
    keep = min(WINDOW, wb + 1)
    win_all = jnp.concatenate([win2d, win_new], axis=1)[:, wb + 1 - keep:]
    return (y, kva.reshape(db, 2, A_KV_HEADS, HEAD_DIM), nsa.reshape(db, 4, B_KV_HEADS, HEAD_DIM),
            win_all.reshape(db, keep, 2, B_KV_HEADS, HEAD_DIM), conv_new)


def kernel(x_prompt, x_sample, cache_moba_kv, cache_nsa_kv, cache_nsa_win_kv, state_ffn_conv, page_table,
           g_mix_pre, w_in, b_gate, cmp_pe, cmp_w1, cmp_b1, cmp_w2, w_out, g_mix_post, g_ffn_pre, w_up,
           w_conv, b_conv, w_down, g_ffn_post):
    depth = w_in.shape[0]
    assert depth == 1 and x_prompt.shape[0] == 1
    w = _prep_weights(0, g_mix_pre, w_in, b_gate, cmp_pe, cmp_w1, cmp_b1, cmp_w2, w_out, g_mix_post,
                      g_ffn_pre, w_up, w_conv, b_conv, w_down, g_ffn_post)
    y_p, moba_p, nsa_p, win_p, conv_p = _prompt_layer(x_prompt[0], w)
    assert x_sample.shape[1] == 1
    y_s, moba_s, nsa_s, win_s, conv_s = _sample_layer(
        x_sample[:, 0], cache_moba_kv[0], cache_nsa_kv[0], cache_nsa_win_kv[0], state_ffn_conv[0],
        page_table, w)
    return (y_p[None], y_s[:, None], moba_p[None, None], moba_s[None, :, None],
            nsa_p[None, None], nsa_s[None, :, None], win_p[None, None], win_s[None],
            conv_p[None, None], conv_s[None])
```

```python
import functools
import math

import numpy as np
import jax
import jax.numpy as jnp
from jax import lax
from jax.experimental import pallas as pl
from jax.experimental.pallas import tpu as pltpu

HEAD_DIM = 64
ROPE_DIM = HEAD_DIM // 4
ROPE_THETA = 500000.0
A_HEADS = 8
A_KV_HEADS = 4
A_BLOCK = 256
A_TOPK = 3
B_HEADS = 8
B_KV_HEADS = 2
B_GROUP = B_HEADS // B_KV_HEADS
CMP_LEN = 32
CMP_STRIDE = 16
CMP_HIDDEN = 128
SEL_BLOCK = 64
SEL_TOPN = 16
WINDOW = 512
CONV_W = 3
RMS_EPS = 1e-6
SCALE = HEAD_DIM ** -0.5

LANES = 128
VMEM_LIMIT = 56 * 1024 * 1024
MASK_BIAS = -32768.0
NEG_INF = float("-inf")

BF16 = jnp.bfloat16
F32 = jnp.float32


def _cparams(*sem):
    return pltpu.CompilerParams(dimension_semantics=sem, vmem_limit_bytes=VMEM_LIMIT)


def _dot(a, b):
    return jnp.dot(a, b, preferred_element_type=F32)


def _dot_t(a, b):
    return lax.dot_general(a, b, (((1,), (1,)), ((), ())), preferred_element_type=F32)


def _rms(x, g):
    y = x * lax.rsqrt(jnp.mean(x * x, axis=-1, keepdims=True) + RMS_EPS)
    return y * g


def _gelu(x):
    c = math.sqrt(2.0 / math.pi)
    return 0.5 * x * (1.0 + jnp.tanh(c * (x + 0.044715 * (x * x * x))))


def _rope128(x, c, sa, sb):
    return x * c + pltpu.roll(x, LANES - ROPE_DIM // 2, 1) * sa + pltpu.roll(x, ROPE_DIM // 2, 1) * sb


C_QA, C_KA, C_VA, C_QB = 0, 512, 768, 1024
C_KC, C_VC, C_KS, C_VS, C_KW, C_VW, C_G = 1536, 1664, 1792, 1920, 2048, 2176, 2304
D_IN_PAD = 2432


def _inproj_kernel(x_ref, g_ref, w_ref, bg_ref, c_ref, sa_ref, sb_ref,
                   kva_ref, nsa_ref, win_ref, gates_ref,
                   qa_ref, kaug_ref, va_ref, qb_ref, qbr_ref, ksaug_ref, vs_ref, kw_ref, vw_ref,
                   kcvc_ref, kmean_ref, *, tm, with_kmean):
    i = pl.program_id(0)
    h = _rms(x_ref[...], g_ref[...]).astype(BF16)
    c, sa, sb = c_ref[...], sa_ref[...], sb_ref[...]

    def proj(c0, width):
        return _dot(h, w_ref[:, c0:c0 + width])

    lane = lax.broadcasted_iota(jnp.int32, (tm, LANES), 1)
    row = lax.broadcasted_iota(jnp.int32, (tm, LANES), 0) + i * tm
    zeros64 = jnp.zeros((tm, HEAD_DIM), BF16)

    for p in range(4):
        q = _rope128(proj(C_QA + p * LANES, LANES), c, sa, sb)
        qa_ref[2 * p] = jnp.concatenate([q[:, :HEAD_DIM].astype(BF16), zeros64], axis=1)
        qa_ref[2 * p + 1] = jnp.concatenate([q[:, HEAD_DIM:].astype(BF16), zeros64], axis=1)

    a_onehot = (lane - HEAD_DIM == row // A_BLOCK).astype(BF16)
    ksum = []
    for p in range(2):
        k = _rope128(proj(C_KA + p * LANES, LANES), c, sa, sb)
        v = proj(C_VA + p * LANES, LANES)
        kva_ref[:, p * LANES:(p + 1) * LANES] = k
        kva_ref[:, 256 + p * LANES:256 + (p + 1) * LANES] = v
        kb = k.astype(BF16)
        vb = v.astype(BF16)
        kaug_ref[2 * p] = jnp.concatenate([kb[:, :HEAD_DIM], a_onehot[:, HEAD_DIM:]], axis=1)
        kaug_ref[2 * p + 1] = jnp.concatenate([kb[:, HEAD_DIM:], a_onehot[:, HEAD_DIM:]], axis=1)
        va_ref[2 * p] = vb[:, :HEAD_DIM]
        va_ref[2 * p + 1] = vb[:, HEAD_DIM:]
        if with_kmean:
            ksum.append(jnp.sum(k.reshape(tm // A_BLOCK, A_BLOCK, LANES), axis=1))
    if with_kmean:
        kmean_ref[0] = jnp.concatenate(ksum, axis=1) * (1.0 / A_BLOCK)
    else:
        kmean_ref[...] = jnp.zeros(kmean_ref.shape, F32)

    for p in range(4):
        q = proj(C_QB + p * LANES, LANES) * SCALE
        qr = _rope128(q, c, sa, sb)
        qb_ref[2 * p] = q[:, :HEAD_DIM].astype(BF16)
        qb_ref[2 * p + 1] = q[:, HEAD_DIM:].astype(BF16)
        qbr_ref[2 * p] = jnp.concatenate([qr[:, :HEAD_DIM].astype(BF16), zeros64], axis=1)
        qbr_ref[2 * p + 1] = jnp.concatenate([qr[:, HEAD_DIM:].astype(BF16), zeros64], axis=1)

    kc = proj(C_KC, LANES)
    vc = proj(C_VC, LANES)
    ks = _rope128(proj(C_KS, LANES), c, sa, sb)
    vs = proj(C_VS, LANES)
    nsa_ref[:, 0:128] = kc
    nsa_ref[:, 128:256] = vc
    nsa_ref[:, 256:384] = ks
    nsa_ref[:, 384:512] = vs
    kcvc_ref[0] = kc[:, :HEAD_DIM]
    kcvc_ref[1] = kc[:, HEAD_DIM:]
    kcvc_ref[2] = vc[:, :HEAD_DIM]
    kcvc_ref[3] = vc[:, HEAD_DIM:]
    s_onehot = (lane - HEAD_DIM == (row // SEL_BLOCK) % HEAD_DIM).astype(BF16)
    ksb = ks.astype(BF16)
    vsb = vs.astype(BF16)
    ksaug_ref[0] = jnp.concatenate([ksb[:, :HEAD_DIM], s_onehot[:, HEAD_DIM:]], axis=1)
    ksaug_ref[1] = jnp.concatenate([ksb[:, HEAD_DIM:], s_onehot[:, HEAD_DIM:]], axis=1)
    vs_ref[0] = vsb[:, :HEAD_DIM]
    vs_ref[1] = vsb[:, HEAD_DIM:]

    kw = _rope128(proj(C_KW, LANES), c, sa, sb)
    vw = proj(C_VW, LANES)
    win_ref[:, 0:128] = kw
    win_ref[:, 128:256] = vw
    kwb = kw.astype(BF16)
    vwb = vw.astype(BF16)
    kw_ref[0] = kwb[:, :HEAD_DIM]
    kw_ref[1] = kwb[:, HEAD_DIM:]
    vw_ref[0] = vwb[:, :HEAD_DIM]
    vw_ref[1] = vwb[:, HEAD_DIM:]

    gates_ref[...] = jax.nn.sigmoid(proj(C_G, LANES) + bg_ref[...])


def _rope_tables(pos):
    half = ROPE_DIM // 2
    inv = ROPE_THETA ** (-2.0 * jnp.arange(half, dtype=F32) / ROPE_DIM)
    ang = pos.astype(F32)[:, None] * inv[None, :]
    cos, sin = jnp.cos(ang), jnp.sin(ang)
    t = pos.shape[0]
    ones = jnp.ones((t, HEAD_DIM - ROPE_DIM), F32)
    zeros = jnp.zeros((t, HEAD_DIM - ROPE_DIM), F32)
    zh = jnp.zeros((t, half), F32)
    c = jnp.concatenate([cos, cos, ones], axis=1)
    sa = jnp.concatenate([-sin, zh, zeros], axis=1)
    sb = jnp.concatenate([zh, sin, zeros], axis=1)
    return tuple(jnp.concatenate([a, a], axis=1) for a in (c, sa, sb))


def _inproj(x2d, pos, g, w_pad, bg_pad, *, tm, with_kmean):
    t, d = x2d.shape
    nt = t // tm
    c, sa, sb = _rope_tables(pos)
    row_spec = lambda w: pl.BlockSpec((tm, w), lambda i: (i, 0))
    head_spec = lambda n, w: pl.BlockSpec((n, tm, w), lambda i: (0, i, 0))
    full = lambda a: pl.BlockSpec(a.shape, lambda i: (0,) * a.ndim)
    nkm = max(tm // A_BLOCK, 1)
    out_shape = (
        jax.ShapeDtypeStruct((t, 512), F32),
        jax.ShapeDtypeStruct((t, 512), F32),
        jax.ShapeDtypeStruct((t, 256), F32),
        jax.ShapeDtypeStruct((t, LANES), F32),
        jax.ShapeDtypeStruct((A_HEADS, t, LANES), BF16),
        jax.ShapeDtypeStruct((A_KV_HEADS, t, LANES), BF16),
        jax.ShapeDtypeStruct((A_KV_HEADS, t, HEAD_DIM), BF16),
        jax.ShapeDtypeStruct((B_HEADS, t, HEAD_DIM), BF16),
        jax.ShapeDtypeStruct((B_HEADS, t, LANES), BF16),
        jax.ShapeDtypeStruct((B_KV_HEADS, t, LANES), BF16),
        jax.ShapeDtypeStruct((B_KV_HEADS, t, HEAD_DIM), BF16),
        jax.ShapeDtypeStruct((B_KV_HEADS, t, HEAD_DIM), BF16),
        jax.ShapeDtypeStruct((B_KV_HEADS, t, HEAD_DIM), BF16),
        jax.ShapeDtypeStruct((4, t, HEAD_DIM), F32),
        jax.ShapeDtypeStruct((nt, nkm, 256), F32),
    )
    out_specs = (
        row_spec(512), row_spec(512), row_spec(256), row_spec(LANES),
        head_spec(A_HEADS, LANES), head_spec(A_KV_HEADS, LANES), head_spec(A_KV_HEADS, HEAD_DIM),
        head_spec(B_HEADS, HEAD_DIM), head_spec(B_HEADS, LANES), head_spec(B_KV_HEADS, LANES),
        head_spec(B_KV_HEADS, HEAD_DIM), head_spec(B_KV_HEADS, HEAD_DIM), head_spec(B_KV_HEADS, HEAD_DIM),
        head_spec(4, HEAD_DIM),
        pl.BlockSpec((1, nkm, 256), lambda i: (i, 0, 0)),
    )
    return pl.pallas_call(
        functools.partial(_inproj_kernel, tm=tm, with_kmean=with_kmean),
        out_shape=out_shape,
        grid=(nt,),
        in_specs=[row_spec(d), full(g), full(w_pad), full(bg_pad),
                  row_spec(LANES), row_spec(LANES), row_spec(LANES)],
        out_specs=out_specs,
        compiler_params=_cparams("parallel"),
        name="inproj",
    )(x2d, g, w_pad, bg_pad, c, sa, sb)


def _outproj_kernel(oa_ref, ob_ref, x_ref, wa_ref, wb_ref, gpost_ref, gpre_ref, x1_ref, h2_ref):
    mix = _dot(oa_ref[...].astype(BF16), wa_ref[...]) + _dot(ob_ref[...].astype(BF16), wb_ref[...])
    x1 = x_ref[...] + _rms(mix, gpost_ref[...])
    x1_ref[...] = x1
    h2_ref[...] = _rms(x1, gpre_ref[...]).astype(BF16)


def _outproj(oa, ob, x2d, w_out_bf, g_post, g_pre, *, tm):
    t, d = x2d.shape
    half = oa.shape[1]
    row = lambda w: pl.BlockSpec((tm, w), lambda i: (i, 0))
    full = lambda a: pl.BlockSpec(a.shape, lambda i: (0,) * a.ndim)
    wa, wb = w_out_bf[:half], w_out_bf[half:]
    return pl.pallas_call(
        _outproj_kernel,
        out_shape=(jax.ShapeDtypeStruct((t, d), F32), jax.ShapeDtypeStruct((t, d), BF16)),
        grid=(t // tm,),
        in_specs=[row(half), row(half), row(d), full(wa), full(wb), full(g_post), full(g_pre)],
        out_specs=(row(d), row(d)),
        compiler_params=_cparams("parallel"),
        name="outproj",
    )(oa, ob, x2d, wa, wb, g_post, g_pre)


HALO = 8


def _ffn_seq_kernel(h_ref, halo_ref, x1_ref, wg_ref, wv_ref, cg_ref, cv_ref, bg_ref, bv_ref,
                    wd_ref, gpost_ref, y_ref, tailg_ref, tailv_ref, acc_ref, ug_ref, uv_ref, *, tm):
    i, c = pl.program_id(0), pl.program_id(1)

    @pl.when(c == 0)
    def _():
        acc_ref[...] = jnp.zeros(acc_ref.shape, F32)

    keep = (i > 0).astype(F32)

    def conv(w_ref, u_ref, cw_ref, cb_ref, tail_ref):
        u_ref[0:HALO] = _dot(halo_ref[...], w_ref[...]) * keep
        u_ref[HALO:HALO + tm] = _dot(h_ref[...], w_ref[...])
        tail_ref[0] = u_ref[tm:tm + HALO]
        cw = cw_ref[...]
        return (u_ref[pl.ds(HALO - 2, tm), :] * cw[0:1] + u_ref[pl.ds(HALO - 1, tm), :] * cw[1:2]
                + u_ref[pl.ds(HALO, tm), :] * cw[2:3] + cb_ref[...])

    gate = conv(wg_ref, ug_ref, cg_ref, bg_ref, tailg_ref)
    val = conv(wv_ref, uv_ref, cv_ref, bv_ref, tailv_ref)
    acc_ref[...] += _dot((_gelu(gate) * val).astype(BF16), wd_ref[...])

    @pl.when(c == pl.num_programs(1) - 1)
    def _():
        y_ref[...] = x1_ref[...] + _rms(acc_ref[...], gpost_ref[...])


def _ffn_step_kernel(h_ref, p0g_ref, p1g_ref, p0v_ref, p1v_ref, x1_ref, wg_ref, wv_ref, cg_ref, cv_ref,
                     bg_ref, bv_ref, wd_ref, gpost_ref, y_ref, upg_ref, upv_ref, acc_ref):
    c = pl.program_id(0)

    @pl.when(c == 0)
    def _():
        acc_ref[...] = jnp.zeros(acc_ref.shape, F32)

    def conv(w_ref, p0_ref, p1_ref, cw_ref, cb_ref, up_ref):
        u = _dot(h_ref[...], w_ref[...])
        up_ref[...] = u
        cw = cw_ref[...]
        return p0_ref[...] * cw[0:1] + p1_ref[...] * cw[1:2] + u * cw[2:3] + cb_ref[...]

    gate = conv(wg_ref, p0g_ref, p1g_ref, cg_ref, bg_ref, upg_ref)
    val = conv(wv_ref, p0v_ref, p1v_ref, cv_ref, bv_ref, upv_ref)
    acc_ref[...] += _dot((_gelu(gate) * val).astype(BF16), wd_ref[...])

    @pl.when(c == pl.num_programs(0) - 1)
    def _():
        y_ref[...] = x1_ref[...] + _rms(acc_ref[...], gpost_ref[...])


def _ffn_seq(h2, x1, w_up_bf, w_conv, b_conv2d, w_down_bf, g_post, *, tm, ck):
    t, d = x1.shape
    dff = w_down_bf.shape[0]
    nff = dff // ck
    nt = t // tm
    hb = tm // HALO
    gcol = lambda r: pl.BlockSpec((r, ck), lambda i, c: (0, c))
    vcol = lambda r: pl.BlockSpec((r, ck), lambda i, c: (0, nff + c))
    row = pl.BlockSpec((tm, d), lambda i, c: (i, 0))
    tail = pl.BlockSpec((1, HALO, ck), lambda i, c: (i, 0, c))
    y, tg, tv = pl.pallas_call(
        functools.partial(_ffn_seq_kernel, tm=tm),
        out_shape=(jax.ShapeDtypeStruct((t, d), F32),
                   jax.ShapeDtypeStruct((nt, HALO, dff), F32), jax.ShapeDtypeStruct((nt, HALO, dff), F32)),
        grid=(nt, nff),
        in_specs=[row, pl.BlockSpec((HALO, d), lambda i, c: (jnp.maximum(i * hb - 1, 0), 0)), row,
                  gcol(d), vcol(d), gcol(CONV_W), vcol(CONV_W), gcol(1), vcol(1),
                  pl.BlockSpec((ck, d), lambda i, c: (c, 0)),
                  pl.BlockSpec(g_post.shape, lambda i, c: (0, 0))],
        out_specs=(row, tail, tail),
        scratch_shapes=[pltpu.VMEM((tm, d), F32), pltpu.VMEM((tm + HALO, ck), F32),
                        pltpu.VMEM((tm + HALO, ck), F32)],
        compiler_params=_cparams("parallel", "arbitrary"),
        name="ffn_seq",
    )(h2, h2, x1, w_up_bf, w_up_bf, w_conv, w_conv, b_conv2d, b_conv2d, w_down_bf, g_post)
    state = jnp.concatenate([tg[-1, HALO - 2:], tv[-1, HALO - 2:]], axis=1)
    return y, state


def _ffn_step(h2, x1, prev, w_up_bf, w_conv, b_conv2d, w_down_bf, g_post, *, ck):
    t, d = x1.shape
    dff = w_down_bf.shape[0]
    nff = dff // ck
    p0, p1 = prev[:, 0], prev[:, 1]
    gcol = lambda r: pl.BlockSpec((r, ck), lambda c: (0, c))
    vcol = lambda r: pl.BlockSpec((r, ck), lambda c: (0, nff + c))
    row = pl.BlockSpec((t, d), lambda c: (0, 0))
    y, ug, uv = pl.pallas_call(
        _ffn_step_kernel,
        out_shape=(jax.ShapeDtypeStruct((t, d), F32),
                   jax.ShapeDtypeStruct((t, dff), F32), jax.ShapeDtypeStruct((t, dff), F32)),
        grid=(nff,),
        in_specs=[row, gcol(t), gcol(t), vcol(t), vcol(t), row,
                  gcol(d), vcol(d), gcol(CONV_W), vcol(CONV_W), gcol(1), vcol(1),
                  pl.BlockSpec((ck, d), lambda c: (c, 0)),
                  pl.BlockSpec(g_post.shape, lambda c: (0, 0))],
        out_specs=(row, pl.BlockSpec((t, ck), lambda c: (0, c)), pl.BlockSpec((t, ck), lambda c: (0, c))),
        scratch_shapes=[pltpu.VMEM((t, d), F32)],
        compiler_params=_cparams("arbitrary"),
        name="ffn_step",
    )(h2, p0, p1, p0, p1, x1, w_up_bf, w_up_bf, w_conv, w_conv, b_conv2d, b_conv2d, w_down_bf, g_post)
    state = jnp.stack([p1, jnp.concatenate([ug, uv], axis=1)], axis=1)
    return y, state


ATT_KT = 512
MOBA_TQ = ATT_KT


def _softmax_init(m_sc, l_sc, acc_sc):
    m_sc[...] = jnp.full(m_sc.shape, NEG_INF, F32)
    l_sc[...] = jnp.zeros(l_sc.shape, F32)
    acc_sc[...] = jnp.zeros(acc_sc.shape, F32)


def _two_pass_attention(q3_ref, k_ref, v_ref, n_full, tiles_per_win, qpos, m_sc, l_sc, acc_sc, p_sc):
    nwin, rows, _ = q3_ref.shape
    kt = p_sc.shape[2]
    groups = [slice(j * LANES, (j + 1) * LANES) for j in range(kt // LANES)]

    def scores(t, ntile=1):
        off = pl.multiple_of(t * kt, kt)
        w = t // tiles_per_win if nwin > 1 else 0
        return _dot_t(q3_ref[w], k_ref[0, pl.ds(off, ntile * kt), :])

    def causal(s, t):
        kpos = t * kt + lax.broadcasted_iota(jnp.int32, (rows, kt), 1)
        return jnp.where(kpos <= qpos, s, NEG_INF)

    def take_max(s):
        mx = s[:, :LANES]
        for j in range(1, s.shape[1] // LANES):
            mx = jnp.maximum(mx, s[:, j * LANES:(j + 1) * LANES])
        m_sc[...] = jnp.maximum(m_sc[...], mx)

    def probs(s):
        m = m_sc[...]
        ps = [jnp.exp(s[:, gs] - m) for gs in groups]
        tot = ps[0]
        for p in ps[1:]:
            tot = tot + p
        l_sc[...] += tot
        return jnp.concatenate(ps, axis=1).astype(BF16)

    def add_pv(p, t):
        off = pl.multiple_of(t * kt, kt)
        acc_sc[...] += _dot(p, v_ref[0, pl.ds(off, kt), :])

    _softmax_init(m_sc, l_sc, acc_sc)

    assert nwin == 1 or tiles_per_win % 2 == 0

    def max_body(u, carry):
        take_max(scores(2 * u, 2))
        return carry

    lax.fori_loop(0, n_full // 2, max_body, 0)
    t_odd = jnp.maximum(n_full - 1, 0)
    take_max(causal(scores(t_odd), t_odd))
    take_max(causal(scores(n_full), n_full))
    m_sc[...] = jnp.broadcast_to(jnp.max(m_sc[...], axis=1, keepdims=True), m_sc.shape)

    p_sc[1] = jnp.zeros(p_sc.shape[1:], BF16)

    def sum_body(t, carry):
        add_pv(p_sc[(t + 1) % 2], jnp.maximum(t - 1, 0))
        p_sc[t % 2] = probs(scores(t))
        return carry

    lax.fori_loop(0, n_full, sum_body, 0)
    add_pv(p_sc[(n_full + 1) % 2], jnp.maximum(n_full - 1, 0))
    add_pv(probs(causal(scores(n_full), n_full)), n_full)
    return acc_sc[...] / jnp.sum(l_sc[...], axis=1, keepdims=True)


def _top_select(v, lane, forced, rounds):
    sel = forced
    lane = lane.astype(F32)
    for _ in range(rounds):
        mx = jnp.max(v, axis=1, keepdims=True)
        idx = jnp.min(jnp.where(v == mx, lane, 1e9), axis=1, keepdims=True)
        pick = (lane == idx) & (mx > NEG_INF)
        sel = sel | pick
        v = jnp.where(pick, NEG_INF, v)
    return sel


def _masked_softmax(s, mask):
    s = jnp.where(mask, s, NEG_INF)
    m = jnp.max(s, axis=1, keepdims=True)
    m = jnp.where(m > NEG_INF, m, 0.0)
    e = jnp.where(mask, jnp.exp(s - m), 0.0)
    d = jnp.sum(e, axis=1, keepdims=True)
    return e / jnp.where(d > 0, d, 1.0)


def _moba_prompt_kernel(q_ref, km_ref, k_ref, v_ref, o_ref, m_sc, l_sc, acc_sc, q3_sc, p_sc):
    i = pl.program_id(1)
    rows = 2 * MOBA_TQ
    q = q_ref[...].reshape(rows, LANES)
    gate = _dot(q, km_ref[0])
    lane = lax.broadcasted_iota(jnp.int32, (rows, LANES), 1)
    qpos = i * MOBA_TQ + lax.broadcasted_iota(jnp.int32, (rows, 1), 0) % MOBA_TQ
    cur = qpos // A_BLOCK
    past = (lane >= HEAD_DIM) & (lane < HEAD_DIM + cur)
    sel = _top_select(jnp.where(past, gate, NEG_INF), lane, lane == HEAD_DIM + cur, A_TOPK)
    bias = jnp.where(sel | (lane < HEAD_DIM), 0.0, MASK_BIAS)
    q3_sc[0] = (q.astype(F32) * SCALE + bias).astype(BF16)

    o = _two_pass_attention(q3_sc, k_ref, v_ref, i, None, qpos, m_sc, l_sc, acc_sc, p_sc)
    o_ref[:, 0:HEAD_DIM] = o[:MOBA_TQ]
    o_ref[:, HEAD_DIM:LANES] = o[MOBA_TQ:]


def _gate_matrix(kmean):
    nblk = kmean.shape[0]
    km = kmean.reshape(nblk, A_KV_HEADS, HEAD_DIM).transpose(1, 2, 0)
    km = jnp.pad(km, ((0, 0), (0, LANES - HEAD_DIM), (HEAD_DIM, LANES - HEAD_DIM - nblk)))
    return km.astype(BF16)


def _moba_prompt(qa, km, kaug, va):
    t = qa.shape[1]
    nt = t // MOBA_TQ
    assert t // A_BLOCK <= HEAD_DIM, "key-block one-hot occupies 64 lanes"
    assert t % MOBA_TQ == 0 and MOBA_TQ % A_BLOCK == 0
    rows = 2 * MOBA_TQ
    return pl.pallas_call(
        _moba_prompt_kernel,
        out_shape=jax.ShapeDtypeStruct((t, A_HEADS * HEAD_DIM), F32),
        grid=(A_KV_HEADS, nt),
        in_specs=[pl.BlockSpec((2, MOBA_TQ, LANES), lambda g, i: (g, i, 0)),
                  pl.BlockSpec((1, LANES, LANES), lambda g, i: (g, 0, 0)),
                  pl.BlockSpec((1, t, LANES), lambda g, i: (g, 0, 0)),
                  pl.BlockSpec((1, t, HEAD_DIM), lambda g, i: (g, 0, 0))],
        out_specs=pl.BlockSpec((MOBA_TQ, LANES), lambda g, i: (i, g)),
        scratch_shapes=[pltpu.VMEM((rows, LANES), F32), pltpu.VMEM((rows, LANES), F32),
                        pltpu.VMEM((rows, HEAD_DIM), F32), pltpu.VMEM((1, rows, LANES), BF16),
                        pltpu.VMEM((2, rows, ATT_KT), BF16)],
        compiler_params=_cparams("parallel", "parallel"),
        name="moba_prompt",
    )(qa, km, kaug, va)


def _compress_kernel(x_ref, pelo_ref, pehi_ref, wlo_ref, whi_ref, b1_ref, w2_ref, o_ref, hi_sc, *, nch):
    x = x_ref[0]
    lo = _dot((x + pelo_ref[0]).astype(BF16), wlo_ref[0])
    hi_sc[0:nch] = _dot((x + pehi_ref[0]).astype(BF16), whi_ref[0])
    hi_sc[nch:nch + 8] = jnp.zeros((8, CMP_HIDDEN), F32)
    hid = _gelu(lo + hi_sc[pl.ds(1, nch), :] + b1_ref[0])
    o_ref[0] = _dot(hid.astype(BF16), w2_ref[0]).astype(o_ref.dtype)


def _compress_weights(cmp_pe, cmp_w1, cmp_b1, cmp_w2):
    flat = CMP_STRIDE * HEAD_DIM
    pelo = cmp_pe[:, :CMP_STRIDE].reshape(2, 1, flat)
    pehi = cmp_pe[:, CMP_STRIDE:].reshape(2, 1, flat)
    wlo = cmp_w1[:, :CMP_STRIDE].reshape(2, flat, CMP_HIDDEN).astype(BF16)
    whi = cmp_w1[:, CMP_STRIDE:].reshape(2, flat, CMP_HIDDEN).astype(BF16)
    return pelo, pehi, wlo, whi, cmp_b1.reshape(2, 1, CMP_HIDDEN), cmp_w2.astype(BF16)


def _compress(xch, cw):
    _, nch, flat = xch.shape
    pelo, pehi, wlo, whi, b1, w2 = cw
    kind = lambda shape: pl.BlockSpec((1,) + shape, lambda j: (j // B_KV_HEADS, 0, 0))
    return pl.pallas_call(
        functools.partial(_compress_kernel, nch=nch),
        out_shape=jax.ShapeDtypeStruct((4, nch, HEAD_DIM), BF16),
        grid=(4,),
        in_specs=[pl.BlockSpec((1, nch, flat), lambda j: (j, 0, 0)),
                  kind((1, flat)), kind((1, flat)), kind((flat, CMP_HIDDEN)), kind((flat, CMP_HIDDEN)),
                  kind((1, CMP_HIDDEN)), kind((CMP_HIDDEN, HEAD_DIM))],
        out_specs=pl.BlockSpec((1, nch, HEAD_DIM), lambda j: (j, 0, 0)),
        scratch_shapes=[pltpu.VMEM((nch + 8, CMP_HIDDEN), F32)],
        compiler_params=_cparams("parallel"),
        name="compress",
    )(xch, pelo, pehi, wlo, whi, b1, w2)


def _importance_matrix(ncmp, nsel):
    ratio = SEL_BLOCK // CMP_STRIDE
    n = np.arange(ncmp)[:, None]
    j = np.arange(nsel)[None, :]
    own = (n // ratio == j)
    last = (n % ratio == ratio - 1)
    m = np.where(own & ~last, 1.0, 0.0) + np.where(last & (own | (n // ratio == j - 1)), 0.5, 0.0)
    return jnp.asarray(m, BF16)


NSA_TQ = 256
WIN_BLOCKS = WINDOW // NSA_TQ + 1


def _nsa_prompt_kernel(*refs, nsel, ncmp):
    (qb_ref, qbr_ref, kc_ref, vc_ref, imp_ref, ks_ref, vs_ref) = refs[:7]
    kw_refs = refs[7:7 + WIN_BLOCKS]
    vw_refs = refs[7 + WIN_BLOCKS:7 + 2 * WIN_BLOCKS]
    g_ref, o_ref, m_sc, l_sc, acc_sc, q3_sc, p_sc = refs[7 + 2 * WIN_BLOCKS:]
    i = pl.program_id(1)
    rows = B_GROUP * NSA_TQ
    q0 = i * NSA_TQ
    q = qb_ref[...].reshape(rows, HEAD_DIM)
    qr = qbr_ref[...].reshape(rows, LANES)

    s = _dot_t(q, kc_ref[0])
    qpos_c = q0 + lax.broadcasted_iota(jnp.int32, (rows, ncmp), 0) % NSA_TQ
    n_c = lax.broadcasted_iota(jnp.int32, (rows, ncmp), 1)
    p = _masked_softmax(s, n_c * CMP_STRIDE + (CMP_LEN - 1) <= qpos_c)
    o_cmp = _dot(p.astype(BF16), vc_ref[0])

    psum = p[0:NSA_TQ]
    for j in range(1, B_GROUP):
        psum = psum + p[j * NSA_TQ:(j + 1) * NSA_TQ]
    p_hi = psum.astype(BF16)
    p_lo = (psum - p_hi.astype(F32)).astype(BF16)
    imp = _dot(p_hi, imp_ref[...]) + _dot(p_lo, imp_ref[...])
    blk = lax.broadcasted_iota(jnp.int32, (NSA_TQ, nsel), 1)
    cur = (q0 + lax.broadcasted_iota(jnp.int32, (NSA_TQ, nsel), 0)) // SEL_BLOCK
    forced = (blk == 0) | (blk == cur) | (blk == cur - 1)
    cand = jnp.where(blk > cur, NEG_INF, jnp.where(forced, jnp.inf, imp))
    sel = _top_select(cand, blk, jnp.zeros((NSA_TQ, nsel), jnp.bool_), min(SEL_TOPN, nsel))
    selbias = jnp.where(sel, 0.0, MASK_BIAS)

    kt_last = (q0 + NSA_TQ - 1) // ATT_KT
    tiles_per_win = HEAD_DIM * SEL_BLOCK // ATT_KT
    qr32 = qr.astype(F32)
    for w in range(-(-nsel // HEAD_DIM)):
        nb = min(HEAD_DIM, nsel - w * HEAD_DIM)
        pieces = [jnp.zeros((NSA_TQ, HEAD_DIM), F32), selbias[:, w * HEAD_DIM:w * HEAD_DIM + nb]]
        if nb < HEAD_DIM:
            pieces.append(jnp.zeros((NSA_TQ, HEAD_DIM - nb), F32))
        bias_w = jnp.concatenate(pieces, axis=1)
        q3_sc[w] = (qr32 + jnp.concatenate([bias_w] * B_GROUP, axis=0)).astype(BF16)

    qpos_col = q0 + lax.broadcasted_iota(jnp.int32, (rows, 1), 0) % NSA_TQ
    o_sel = _two_pass_attention(q3_sc, ks_ref, vs_ref, kt_last, tiles_per_win, qpos_col,
                                m_sc, l_sc, acc_sc, p_sc)

    kband = jnp.concatenate([r[0] for r in kw_refs], axis=0)
    vband = jnp.concatenate([r[0] for r in vw_refs], axis=0)
    nband = WIN_BLOCKS * NSA_TQ
    s = _dot_t(qr[:, :HEAD_DIM], kband)
    qpos = q0 + lax.broadcasted_iota(jnp.int32, (rows, nband), 0) % NSA_TQ
    kpos = q0 - WINDOW + lax.broadcasted_iota(jnp.int32, (rows, nband), 1)
    p = _masked_softmax(s, (kpos <= qpos) & (kpos >= qpos - WINDOW) & (kpos >= 0))
    o_win = _dot(p.astype(BF16), vband)

    gts = g_ref[0]
    for j in range(B_GROUP):
        sl = slice(j * NSA_TQ, (j + 1) * NSA_TQ)
        o = (gts[:, 3 * j:3 * j + 1] * o_cmp[sl] + gts[:, 3 * j + 1:3 * j + 2] * o_sel[sl]
             + gts[:, 3 * j + 2:3 * j + 3] * o_win[sl])
        o_ref[:, j * HEAD_DIM:(j + 1) * HEAD_DIM] = o


def _nsa_prompt(qb, qbr, kvcmp, ksaug, vs, kw, vw, gates):
    t = qb.shape[1]
    nt = t // NSA_TQ
    nsel = t // SEL_BLOCK
    ncmp = kvcmp.shape[1]
    rows = B_GROUP * NSA_TQ
    impm = _importance_matrix(ncmp, nsel)
    res = lambda w: pl.BlockSpec((1, t, w), lambda g, i: (g, 0, 0))
    band = [pl.BlockSpec((1, NSA_TQ, HEAD_DIM),
                         functools.partial(lambda g, i, j: (g, jnp.maximum(i - (WIN_BLOCKS - 1) + j, 0), 0), j=j))
            for j in range(WIN_BLOCKS)]
    return pl.pallas_call(
        functools.partial(_nsa_prompt_kernel, nsel=nsel, ncmp=ncmp),
        out_shape=jax.ShapeDtypeStruct((t, B_HEADS * HEAD_DIM), F32),
        grid=(B_KV_HEADS, nt),
        in_specs=[pl.BlockSpec((B_GROUP, NSA_TQ, HEAD_DIM), lambda g, i: (g, i, 0)),
                  pl.BlockSpec((B_GROUP, NSA_TQ, LANES), lambda g, i: (g, i, 0)),
                  pl.BlockSpec((1, ncmp, HEAD_DIM), lambda g, i: (g, 0, 0)),
                  pl.BlockSpec((1, ncmp, HEAD_DIM), lambda g, i: (B_KV_HEADS + g, 0, 0)),
                  pl.BlockSpec(impm.shape, lambda g, i: (0, 0)),
                  res(LANES), res(HEAD_DIM)] + band + band +
                 [pl.BlockSpec((1, NSA_TQ, 3 * B_GROUP), lambda g, i: (g, i, 0))],
        out_specs=pl.BlockSpec((NSA_TQ, B_GROUP * HEAD_DIM), lambda g, i: (i, g)),
        scratch_shapes=[pltpu.VMEM((rows, LANES), F32), pltpu.VMEM((rows, LANES), F32),
                        pltpu.VMEM((rows, HEAD_DIM), F32),
                        pltpu.VMEM((-(-nsel // HEAD_DIM), rows, LANES), BF16),
                        pltpu.VMEM((2, rows, ATT_KT), BF16)],
        compiler_params=_cparams("parallel", "parallel"),
        name="nsa_prompt",
    )(qb, qbr, kvcmp, kvcmp, impm, ksaug, vs, *([kw] * WIN_BLOCKS), *([vw] * WIN_BLOCKS), gates)


def _top_indices(v, lane, rounds):
    out = jnp.zeros((v.shape[0], LANES), jnp.int32)
    slot = lax.broadcasted_iota(jnp.int32, out.shape, 1)
    for r in range(rounds):
        mx = jnp.max(v, axis=1, keepdims=True)
        idx = jnp.min(jnp.where(v == mx, lane, 1 << 20), axis=1, keepdims=True)
        out = jnp.where(slot == r, idx, out)
        v = jnp.where(lane == idx, NEG_INF, v)
    return out


def _bf16_round(x):
    return x.astype(BF16).astype(F32)


def _moba_kmean_kernel(pt_ref, *refs, pg):
    pages, o_ref = refs[:pg], refs[pg]
    step = pl.program_id(1)
    per_blk = A_BLOCK // pages[0].shape[2]
    nb = pg // per_blk

    @pl.when(step == 0)
    def _():
        o_ref[...] = jnp.zeros(o_ref.shape, F32)

    lane = lax.broadcasted_iota(jnp.int32, o_ref.shape[1:], 1)
    out = o_ref[0]
    for j in range(nb):
        tot = pages[j * per_blk][0]
        for r in pages[j * per_blk + 1:(j + 1) * per_blk]:
            tot = tot + r[0]
        mean = jnp.sum(tot, axis=1, keepdims=True) * (1.0 / A_BLOCK)
        out = jnp.where(lane == step * nb + j, mean, out)
    o_ref[0] = out


def _moba_kmean(cache_t, page_table, *, pg):
    db, npg = page_table.shape
    page = cache_t.shape[2]
    kw = A_KV_HEADS * HEAD_DIM
    assert npg * page // A_BLOCK <= LANES
    specs = [pl.BlockSpec((1, kw, page), functools.partial(lambda b, s, pt, j: (pt[b, s * pg + j], 0, 0), j=j))
             for j in range(pg)]
    return pl.pallas_call(
        functools.partial(_moba_kmean_kernel, pg=pg),
        out_shape=jax.ShapeDtypeStruct((db, kw, LANES), F32),
        grid_spec=pltpu.PrefetchScalarGridSpec(
            num_scalar_prefetch=1, grid=(db, npg // pg), in_specs=specs,
            out_specs=pl.BlockSpec((1, kw, LANES), lambda b, s, pt: (b, 0, 0))),
        compiler_params=_cparams("parallel", "arbitrary"),
        name="moba_kmean",
    )(page_table, *([cache_t] * pg))


def _moba_gate_kernel(q_ref, km_ref, idx_ref, *, nblk):
    q = q_ref[0][:, :HEAD_DIM].astype(BF16)
    head = lax.broadcasted_iota(jnp.int32, (A_HEADS, LANES), 0)
    lane = lax.broadcasted_iota(jnp.int32, (A_HEADS, LANES), 1)
    gate = jnp.zeros((A_HEADS, LANES), F32)
    for g in range(A_KV_HEADS):
        kmg = km_ref[0][g * HEAD_DIM:(g + 1) * HEAD_DIM, :].astype(BF16)
        gate = jnp.where(head // (A_HEADS // A_KV_HEADS) == g, _dot(q, kmg), gate)
    idx_ref[0] = _top_indices(jnp.where(lane < nblk, gate, NEG_INF), lane, A_TOPK)


def _moba_gate(qa_rows, kmean_t, *, nblk):
    db, kw, _ = kmean_t.shape
    return pl.pallas_call(
        functools.partial(_moba_gate_kernel, nblk=nblk),
        out_shape=jax.ShapeDtypeStruct((db, A_HEADS, LANES), jnp.int32),
        grid=(db,),
        in_specs=[pl.BlockSpec((1, A_HEADS, LANES), lambda b: (b, 0, 0)),
                  pl.BlockSpec((1, kw, LANES), lambda b: (b, 0, 0))],
        out_specs=pl.BlockSpec((1, A_HEADS, LANES), lambda b: (b, 0, 0)),
        compiler_params=_cparams("parallel"),
        name="moba_gate",
    )(qa_rows, kmean_t)


def _gathered_attention(q, k_pages, v_pages, masks, k_own, v_own):
    qb = q.astype(BF16)
    ss = [_dot(qb, k.astype(BF16)) for k in k_pages]
    ss = [s if mk is None else jnp.where(mk, s, NEG_INF) for s, mk in zip(ss, masks)]
    s_own = jnp.sum(q * _bf16_round(k_own), axis=1, keepdims=True)
    m = s_own
    for s in ss:
        m = jnp.maximum(m, jnp.max(s, axis=1, keepdims=True))
    e_own = jnp.exp(s_own - m)
    d = e_own
    acc = _bf16_round(e_own) * _bf16_round(v_own)
    for s, v in zip(ss, v_pages):
        e = jnp.exp(s - m)
        d = d + jnp.sum(e, axis=1, keepdims=True)
        acc = acc + _dot_t(e.astype(BF16), v.astype(BF16))
    return acc / d


def _moba_sample_kernel(pt_ref, ix_ref, q_ref, *refs, npick):
    k_refs, v_refs = refs[:npick], refs[npick:2 * npick]
    kn_ref, vn_ref, o_ref, o_sc = refs[2 * npick:]
    h = pl.program_id(1)
    q = q_ref[0][:, :HEAD_DIM] * SCALE
    o_sc[...] = _gathered_attention(q, [r[0] for r in k_refs], [r[0] for r in v_refs], [None] * npick,
                                    kn_ref[0], vn_ref[0])
    o_ref[0, pl.ds(h, 1), :] = o_sc[pl.ds(h, 1), :]


def _moba_sample(cache_t, page_table, idx, qa_rows, k_new, v_new):
    db = page_table.shape[0]
    page = cache_t.shape[2]
    npage = A_BLOCK // page
    rep = A_HEADS // A_KV_HEADS

    def kv_spec(s, j, row_blk0):
        def imap(b, h, pt, ix):
            blk = ix[(b * A_HEADS + h) * A_TOPK + s]
            return (pt[b, blk * npage + j], row_blk0 + h // rep, 0)
        return pl.BlockSpec((1, HEAD_DIM, page), imap)

    picks = [(s, j) for s in range(A_TOPK) for j in range(npage)]
    per_seq = lambda w: pl.BlockSpec((1, A_HEADS, w), lambda b, h, pt, ix: (b, 0, 0))
    return pl.pallas_call(
        functools.partial(_moba_sample_kernel, npick=len(picks)),
        out_shape=jax.ShapeDtypeStruct((db, A_HEADS, HEAD_DIM), F32),
        grid_spec=pltpu.PrefetchScalarGridSpec(
            num_scalar_prefetch=2, grid=(db, A_HEADS),
            in_specs=[per_seq(LANES)]
                     + [kv_spec(s, j, 0) for s, j in picks] + [kv_spec(s, j, A_KV_HEADS) for s, j in picks]
                     + [per_seq(HEAD_DIM), per_seq(HEAD_DIM)],
            out_specs=per_seq(HEAD_DIM),
            scratch_shapes=[pltpu.VMEM((A_HEADS, HEAD_DIM), F32)]),
        compiler_params=_cparams("parallel", "arbitrary"),
        name="moba_sample",
    )(page_table, idx, qa_rows, *([cache_t] * (2 * len(picks))), k_new, v_new)


def _nsa_flatten_kernel(pt_ref, *refs, pg, nch):
    pages = refs[:pg]
    pelo_ref, pehi_ref, wlo_ref, whi_ref, b1_ref, w2_ref, o_ref, x_sc, hi_sc, rows_sc = refs[pg:]
    step = pl.program_id(1)
    page = pages[0].shape[2]
    nc = pg * page // CMP_STRIDE
    base = pl.multiple_of(step * nc, nc)
    lane = lax.broadcasted_iota(jnp.int32, (nc, LANES), 1)
    lo_half = lane < HEAD_DIM
    for j, r in enumerate(pages):
        for pr in range(2):
            rows_sc[pr, j * page:(j + 1) * page, :] = r[0, pr * LANES:(pr + 1) * LANES, :].T
    for u in range(CMP_STRIDE // 2):
        for pr in range(2):
            ap = rows_sc[pr, pl.ds(2 * u, nc, stride=CMP_STRIDE), :]
            bp = rows_sc[pr, pl.ds(2 * u + 1, nc, stride=CMP_STRIDE), :]
            x_sc[2 * pr, pl.ds(base, nc), u * LANES:(u + 1) * LANES] = jnp.where(
                lo_half, ap, pltpu.roll(bp, HEAD_DIM, 1))
            x_sc[2 * pr + 1, pl.ds(base, nc), u * LANES:(u + 1) * LANES] = jnp.where(
                lo_half, pltpu.roll(ap, HEAD_DIM, 1), bp)

    @pl.when(step == pl.num_programs(1) - 1)
    def _():
        hi_sc[nch:nch + 8] = jnp.zeros((8, CMP_HIDDEN), F32)
        for j in range(4):
            c = j // B_KV_HEADS
            x = x_sc[j]
            lo = _dot((x + pelo_ref[c]).astype(BF16), wlo_ref[c])
            hi_sc[0:nch] = _dot((x + pehi_ref[c]).astype(BF16), whi_ref[c])
            hid = _gelu(lo + hi_sc[pl.ds(1, nch), :] + b1_ref[c])
            o_ref[0, j] = _dot(hid.astype(BF16), w2_ref[c]).astype(o_ref.dtype)


def _nsa_sample_compress(cache_t, page_table, cw, *, pg):
    db, npg = page_table.shape
    page = cache_t.shape[2]
    assert page == LANES
    nch = npg * page // CMP_STRIDE
    flat = CMP_STRIDE * HEAD_DIM
    pelo, pehi, wlo, whi, b1, w2 = cw
    full = lambda a: pl.BlockSpec(a.shape, lambda b, s, pt: (0,) * a.ndim)
    specs = [pl.BlockSpec((1, 2 * LANES, page), functools.partial(lambda b, s, pt, j: (pt[b, s * pg + j], 0, 0), j=j))
             for j in range(pg)]
    return pl.pallas_call(
        functools.partial(_nsa_flatten_kernel, pg=pg, nch=nch),
        out_shape=jax.ShapeDtypeStruct((db, 4, nch, HEAD_DIM), BF16),
        grid_spec=pltpu.PrefetchScalarGridSpec(
            num_scalar_prefetch=1, grid=(db, npg // pg),
            in_specs=specs + [full(a) for a in (pelo, pehi, wlo, whi, b1, w2)],
            out_specs=pl.BlockSpec((1, 4, nch, HEAD_DIM), lambda b, s, pt: (b, 0, 0, 0)),
            scratch_shapes=[pltpu.VMEM((4, nch, flat), F32), pltpu.VMEM((nch + 8, CMP_HIDDEN), F32),
                            pltpu.VMEM((2, pg * page, LANES), F32)]),
        compiler_params=_cparams("parallel", "arbitrary"),
        name="nsa_sample_compress",
    )(page_table, *([cache_t] * pg), pelo, pehi, wlo, whi, b1, w2)


def _rows_by_group(per_group):
    head = lax.broadcasted_iota(jnp.int32, per_group[0].shape, 0)
    out = per_group[0]
    for g in range(1, B_KV_HEADS):
        out = jnp.where(head // B_GROUP == g, per_group[g], out)
    return out


def _nsa_sample_select_kernel(q_ref, qr_ref, kv_ref, imp_ref, win_ref, wnew_ref, g_ref, idx_ref, ocw_ref,
                              *, past, nsel_past, ncmp):
    q = q_ref[0][:, :HEAD_DIM].astype(BF16)
    qr = qr_ref[0][:, :HEAD_DIM]
    gts = g_ref[0]
    glane = lax.broadcasted_iota(jnp.int32, (B_HEADS, LANES), 1)
    ghead = lax.broadcasted_iota(jnp.int32, (B_HEADS, LANES), 0)
    gate = lambda c: jnp.sum(jnp.where(glane == 3 * ghead + c, gts, 0.0), axis=1, keepdims=True)

    s = _rows_by_group([_dot_t(q, kv_ref[0, g]) for g in range(B_KV_HEADS)])
    n_c = lax.broadcasted_iota(jnp.int32, (B_HEADS, ncmp), 1)
    p = _masked_softmax(s, n_c * CMP_STRIDE + (CMP_LEN - 1) <= past)
    o_cmp = _rows_by_group([_dot(p.astype(BF16), kv_ref[0, B_KV_HEADS + g]) for g in range(B_KV_HEADS)])

    head = lax.broadcasted_iota(jnp.int32, (B_HEADS, ncmp), 0)
    psum = jnp.zeros((B_HEADS, ncmp), F32)
    for g in range(B_KV_HEADS):
        tot = jnp.sum(jnp.where(head // B_GROUP == g, p, 0.0), axis=0, keepdims=True)
        psum = jnp.where(head == g, tot, psum)
    p_hi = psum.astype(BF16)
    p_lo = (psum - p_hi.astype(F32)).astype(BF16)
    imp = _dot(p_hi, imp_ref[...]) + _dot(p_lo, imp_ref[...])
    blk = lax.broadcasted_iota(jnp.int32, (B_HEADS, nsel_past), 1)
    cand = jnp.where((blk == 0) | (blk == nsel_past - 1), jnp.inf, imp)
    idx_ref[0] = _top_indices(cand, blk, min(SEL_TOPN, nsel_past + 1) - 1)

    wk = [win_ref[0][g * HEAD_DIM:(g + 1) * HEAD_DIM, :] for g in range(B_KV_HEADS)]
    wv = [win_ref[0][(B_KV_HEADS + g) * HEAD_DIM:(B_KV_HEADS + g + 1) * HEAD_DIM, :] for g in range(B_KV_HEADS)]
    nk = [wnew_ref[0][:, g * HEAD_DIM:(g + 1) * HEAD_DIM] for g in range(B_KV_HEADS)]
    nv = [wnew_ref[0][:, (B_KV_HEADS + g) * HEAD_DIM:(B_KV_HEADS + g + 1) * HEAD_DIM] for g in range(B_KV_HEADS)]
    s_w = _rows_by_group([_dot(qr.astype(BF16), k.astype(BF16)) for k in wk])
    s_n = _rows_by_group([jnp.sum(qr * _bf16_round(k), axis=1, keepdims=True) for k in nk])
    m = jnp.maximum(jnp.max(s_w, axis=1, keepdims=True), s_n)
    e_w, e_n = jnp.exp(s_w - m), jnp.exp(s_n - m)
    d = jnp.sum(e_w, axis=1, keepdims=True) + e_n
    o_win = _rows_by_group([_dot_t(e_w.astype(BF16), v.astype(BF16)) + _bf16_round(e_n) * _bf16_round(nvg)
                            for v, nvg in zip(wv, nv)]) / d

    ocw_ref[0] = jnp.concatenate([gate(0) * o_cmp, gate(2) * o_win], axis=1)


def _nsa_sample_select(qb_rows, qbr_rows, kvcmp, win_cache, win_new, gates, *, past):
    db, _, ncmp, _ = kvcmp.shape
    nsel_past = past // SEL_BLOCK
    impm = _importance_matrix(ncmp, nsel_past)
    row3 = lambda a: pl.BlockSpec((1,) + a.shape[1:], lambda b: (b,) + (0,) * (a.ndim - 1))
    kern = functools.partial(_nsa_sample_select_kernel, past=past, nsel_past=nsel_past, ncmp=ncmp)
    return pl.pallas_call(
        kern,
        out_shape=(jax.ShapeDtypeStruct((db, B_HEADS, LANES), jnp.int32),
                   jax.ShapeDtypeStruct((db, B_HEADS, LANES), F32)),
        grid=(db,),
        in_specs=[row3(qb_rows), row3(qbr_rows), row3(kvcmp), pl.BlockSpec(impm.shape, lambda b: (0, 0)),
                  row3(win_cache), row3(win_new), row3(gates)],
        out_specs=(pl.BlockSpec((1, B_HEADS, LANES), lambda b: (b, 0, 0)),
                   pl.BlockSpec((1, B_HEADS, LANES), lambda b: (b, 0, 0))),
        compiler_params=_cparams("parallel"),
        name="nsa_sample_select",
    )(qb_rows, qbr_rows, kvcmp, impm, win_cache, win_new, gates)


def _nsa_sample_attend_kernel(pt_ref, ix_ref, qr_ref, *refs, nslot, per_page):
    k_refs, v_refs = refs[:nslot], refs[nslot:2 * nslot]
    kn_ref, vn_ref, g_ref, ocw_ref, o_ref = refs[2 * nslot:]
    b, g = pl.program_id(0), pl.program_id(1)
    qs = qr_ref[0, pl.ds(g * B_GROUP, B_GROUP), :][:, :HEAD_DIM]

    page = k_refs[0].shape[2]
    lane = lax.broadcasted_iota(jnp.int32, (B_GROUP, page), 1)
    masks = [lane // SEL_BLOCK == ix_ref[(b * B_KV_HEADS + g) * nslot + s] % per_page for s in range(nslot)]
    o_sel = _gathered_attention(qs, [r[0] for r in k_refs], [r[0] for r in v_refs], masks,
                                kn_ref[0, 0], vn_ref[0, 0])
    gts = g_ref[0]
    hrow = lax.broadcasted_iota(jnp.int32, (B_GROUP, LANES), 0) + g * B_GROUP
    glane = lax.broadcasted_iota(jnp.int32, (B_GROUP, LANES), 1)
    g1 = jnp.sum(jnp.where(glane == 3 * hrow + 1, gts, 0.0), axis=1, keepdims=True)
    ocw = ocw_ref[0, pl.ds(g * B_GROUP, B_GROUP), :]
    o_ref[0, pl.ds(g * B_GROUP, B_GROUP), :] = g1 * o_sel + ocw[:, :HEAD_DIM] + ocw[:, HEAD_DIM:]


def _nsa_sample_attend(cache_t, page_table, idx, qbr_rows, ks_new, vs_new, gates, ocw, *, nslot):
    db = page_table.shape[0]
    page = cache_t.shape[2]
    per_page = page // SEL_BLOCK

    def kv_spec(s, row_blk0):
        def imap(b, g, pt, ix):
            blk = ix[(b * B_KV_HEADS + g) * nslot + s]
            return (pt[b, blk // per_page], row_blk0 + g, 0)
        return pl.BlockSpec((1, HEAD_DIM, page), imap)

    row = lambda a: pl.BlockSpec((1,) + a.shape[1:], lambda b, g, pt, ix: (b,) + (0,) * (a.ndim - 1))
    new = pl.BlockSpec((1, 1, 1, HEAD_DIM), lambda b, g, pt, ix: (b, g, 0, 0))
    return pl.pallas_call(
        functools.partial(_nsa_sample_attend_kernel, nslot=nslot, per_page=per_page),
        out_shape=jax.ShapeDtypeStruct((db, B_HEADS, HEAD_DIM), F32),
        grid_spec=pltpu.PrefetchScalarGridSpec(
            num_scalar_prefetch=2, grid=(db, B_KV_HEADS),
            in_specs=[row(qbr_rows)] + [kv_spec(s, 2 * B_KV_HEADS) for s in range(nslot)]
                     + [kv_spec(s, 3 * B_KV_HEADS) for s in range(nslot)] + [new, new, row(gates), row(ocw)],
            out_specs=pl.BlockSpec((1, B_HEADS, HEAD_DIM), lambda b, g, pt, ix: (b, 0, 0))),
        compiler_params=_cparams("parallel", "arbitrary"),
        name="nsa_sample_attend",
    )(page_table, idx, qbr_rows, *([cache_t] * (2 * nslot)), ks_new, vs_new, gates, ocw)


def _prep_weights(l, g_mix_pre, w_in, b_gate, cmp_pe, cmp_w1, cmp_b1, cmp_w2, w_out, g_mix_post,
                  g_ffn_pre, w_up, w_conv, b_conv, w_down, g_ffn_post):
    d_in = w_in.shape[-1]
    n_gate = b_gate.shape[-1]
    return dict(
        g_mix_pre=g_mix_pre[l][None],
        w_in=jnp.pad(w_in[l], ((0, 0), (0, D_IN_PAD - d_in))).astype(BF16),
        b_gate=jnp.pad(b_gate[l], (0, LANES - n_gate))[None],
        cmp=_compress_weights(cmp_pe[l], cmp_w1[l], cmp_b1[l], cmp_w2[l]),
        w_out=w_out[l].astype(BF16), g_mix_post=g_mix_post[l][None], g_ffn_pre=g_ffn_pre[l][None],
        w_up=w_up[l].astype(BF16), w_conv=w_conv[l], b_conv=b_conv[l][None],
        w_down=w_down[l].astype(BF16), g_ffn_post=g_ffn_post[l][None],
    )


def _group_gates(gates):
    t = gates.shape[0]
    return gates[:, :3 * B_HEADS].reshape(t, B_KV_HEADS, 3 * B_GROUP).transpose(1, 0, 2)


FFN_CK = 256
FFN_TM = 1024


def _prompt_layer(x2d, w):
    t = x2d.shape[0]
    (kva, nsa, win, gates, qa, kaug, va, qb, qbr, ksaug, vs, kw, vw, kcvc, kmean) = _inproj(
        x2d, jnp.arange(t), w["g_mix_pre"], w["w_in"], w["b_gate"], tm=512, with_kmean=True)
    o_a = _moba_prompt(qa, _gate_matrix(kmean.reshape(t // A_BLOCK, A_KV_HEADS * HEAD_DIM)), kaug, va)
    kvcmp = _compress(kcvc.reshape(4, t // CMP_STRIDE, CMP_STRIDE * HEAD_DIM), w["cmp"])
    o_b = _nsa_prompt(qb, qbr, kvcmp, ksaug, vs, kw, vw, _group_gates(gates))
    x1, h2 = _outproj(o_a, o_b, x2d, w["w_out"], w["g_mix_post"], w["g_ffn_pre"], tm=512)
    y, conv_state = _ffn_seq(h2, x1, w["w_up"], w["w_conv"], w["b_conv"], w["w_down"], w["g_ffn_post"],
                             tm=FFN_TM, ck=FFN_CK)
    keep = min(WINDOW, t)
    return (y, kva.reshape(t, 2, A_KV_HEADS, HEAD_DIM), nsa.reshape(t, 4, B_KV_HEADS, HEAD_DIM),
            win[t - keep:].reshape(keep, 2, B_KV_HEADS, HEAD_DIM), conv_state)


PAGES_PER_STEP = 16


def _sample_layer(x2d, cache_moba, cache_nsa, win_cache, conv_state, page_table, w):
    db = x2d.shape[0]
    n_pool, page = cache_moba.shape[:2]
    npg = page_table.shape[1]
    past = npg * page
    wb = win_cache.shape[1]
    assert wb == WINDOW and past % A_BLOCK == 0 and past // A_BLOCK >= A_TOPK and A_BLOCK % page == 0
    assert page % SEL_BLOCK == 0
    pg = min(PAGES_PER_STEP, npg)
    (kva, nsa, win, gates, qa, _, _, qb, qbr, _, _, _, _, _, _) = _inproj(
        x2d, jnp.full((db,), past, jnp.int32), w["g_mix_pre"], w["w_in"], w["b_gate"], tm=db, with_kmean=False)
    rows = lambda a: a.transpose(1, 0, 2).astype(F32)
    qa_rows, qb_rows, qbr_rows = rows(qa), rows(qb), rows(qbr)

    tpose = lambda c: c.transpose(0, 2, 3, 4, 1).reshape(c.shape[0], -1, c.shape[1])
    moba_t, nsa_t, win_t = tpose(cache_moba), tpose(cache_nsa), tpose(win_cache)

    kmean_t = _moba_kmean(moba_t, page_table, pg=pg)
    idx_a = _moba_gate(qa_rows, kmean_t, nblk=past // A_BLOCK)[:, :, :A_TOPK].reshape(-1)
    rep = A_HEADS // A_KV_HEADS
    per_head = lambda a: jnp.repeat(a.reshape(db, A_KV_HEADS, HEAD_DIM), rep, axis=1)
    o_a = _moba_sample(moba_t, page_table, idx_a, qa_rows, per_head(kva[:, :256]), per_head(kva[:, 256:]))

    kvcmp = _nsa_sample_compress(nsa_t, page_table, w["cmp"], pg=pg)
    gates3 = gates[:, None, :]
    win2d = win_cache.reshape(db, wb, 2 * B_KV_HEADS * HEAD_DIM)
    win_new = win[:, None, :]
    idx_b, ocw = _nsa_sample_select(qb_rows, qbr_rows, kvcmp, win_t, win_new, gates3, past=past)
    nslot = min(SEL_TOPN, past // SEL_BLOCK + 1) - 1
    per_group = lambda a: a.reshape(db, B_KV_HEADS, 1, HEAD_DIM)
    o_b = _nsa_sample_attend(nsa_t, page_table, idx_b[:, :B_KV_HEADS, :nslot].reshape(-1), qbr_rows,
                             per_group(nsa[:, 256:384]), per_group(nsa[:, 384:]), gates3, ocw, nslot=nslot)

    x1, h2 = _outproj(o_a.reshape(db, -1), o_b.reshape(db, -1), x2d, w["w_out"], w["g_mix_post"],
                      w["g_ffn_pre"], tm=db)
    y, conv_new = _ffn_step(h2, x1, conv_state, w["w_up"], w["w_conv"], w["b_conv"], w["w_down"],
                            w["g_ffn_post"], ck=FFN_CK)
    keep = min(WINDOW, wb + 1)
    win_all = jnp.concatenate([win2d, win_new], axis=1)[:, wb + 1 - keep:]
    return (y, kva.reshape(db, 2, A_KV_HEADS, HEAD_DIM), nsa.reshape(db, 4, B_KV_HEADS, HEAD_DIM),
            win_all.reshape(db, keep, 2, B_KV_HEADS, HEAD_DIM), conv_new)


def kernel(x_prompt, x_sample, cache_moba_kv, cache_nsa_kv, cache_nsa_win_kv, state_ffn_conv, page_table,
           g_mix_pre, w_in, b_gate, cmp_pe, cmp_w1, cmp_b1, cmp_w2, w_out, g_mix_post, g_ffn_pre, w_up,
           w_conv, b_conv, w_down, g_ffn_post):
    depth = w_in.shape[0]
    assert depth == 1 and x_prompt.shape[0] == 1
    w = _prep_weights(0, g_mix_pre, w_in, b_gate, cmp_pe, cmp_w1, cmp_b1, cmp_w2, w_out, g_mix_post,
                      g_ffn_pre, w_up, w_conv, b_conv, w_down, g_ffn_post)
    y_p, moba_p, nsa_p, win_p, conv_p = _prompt_layer(x_prompt[0], w)
    assert x_sample.shape[1] == 1
    y_s, moba_s, nsa_s, win_s, conv_s = _sample_layer(
        x_sample[:, 0], cache_moba_kv[0], cache_nsa_kv[0], cache_nsa_win_kv[0], state_ffn_conv[0],
        page_table, w)
    return (y_p[None], y_s[:, None], moba_p[None, None], moba_s[None, :, None],
            nsa_p[None, None], nsa_s[None, :, None], win_p[None, None], win_s[None],
            conv_p[None, None], conv_s[None])
```

```python
import functools
import math

import numpy as np
import jax
import jax.numpy as jnp
from jax import lax
from jax.experimental import pallas as pl
from jax.experimental.pallas import tpu as pltpu

HEAD_DIM = 64
ROPE_DIM = HEAD_DIM // 4
ROPE_THETA = 500000.0
A_HEADS = 8
A_KV_HEADS = 4
A_BLOCK = 256
A_TOPK = 3
B_HEADS = 8
B_KV_HEADS = 2
B_GROUP = B_HEADS // B_KV_HEADS
CMP_LEN = 32
CMP_STRIDE = 16
CMP_HIDDEN = 128
SEL_BLOCK = 64
SEL_TOPN = 16
WINDOW = 512
CONV_W = 3
RMS_EPS = 1e-6
SCALE = HEAD_DIM ** -0.5

LANES = 128
VMEM_LIMIT = 56 * 1024 * 1024
MASK_BIAS = -32768.0
NEG_INF = float("-inf")

BF16 = jnp.bfloat16
F32 = jnp.float32


def _cparams(*sem):
    return pltpu.CompilerParams(dimension_semantics=sem, vmem_limit_bytes=VMEM_LIMIT)


def _dot(a, b):
    return jnp.dot(a, b, preferred_element_type=F32)


def _dot_t(a, b):
    return lax.dot_general(a, b, (((1,), (1,)), ((), ())), preferred_element_type=F32)


def _rms(x, g):
    y = x * lax.rsqrt(jnp.mean(x * x, axis=-1, keepdims=True) + RMS_EPS)
    return y * g


def _gelu(x):
    c = math.sqrt(2.0 / math.pi)
    return 0.5 * x * (1.0 + jnp.tanh(c * (x + 0.044715 * (x * x * x))))


def _rope128(x, c, sa, sb):
    return x * c + pltpu.roll(x, LANES - ROPE_DIM // 2, 1) * sa + pltpu.roll(x, ROPE_DIM // 2, 1) * sb


C_QA, C_KA, C_VA, C_QB = 0, 512, 768, 1024
C_KC, C_VC, C_KS, C_VS, C_KW, C_VW, C_G = 1536, 1664, 1792, 1920, 2048, 2176, 2304
D_IN_PAD = 2432


def _inproj_kernel(x_ref, g_ref, w_ref, bg_ref, c_ref, sa_ref, sb_ref,
                   kva_ref, nsa_ref, win_ref, gates_ref,
                   qa_ref, kaug_ref, va_ref, qb_ref, qbr_ref, ksaug_ref, vs_ref, kw_ref, vw_ref,
                   kcvc_ref, kmean_ref, *, tm, with_kmean):
    i = pl.program_id(0)
    h = _rms(x_ref[...], g_ref[...]).astype(BF16)
    c, sa, sb = c_ref[...], sa_ref[...], sb_ref[...]

    def proj(c0, width):
        return _dot(h, w_ref[:, c0:c0 + width])

    lane = lax.broadcasted_iota(jnp.int32, (tm, LANES), 1)
    row = lax.broadcasted_iota(jnp.int32, (tm, LANES), 0) + i * tm
    zeros64 = jnp.zeros((tm, HEAD_DIM), BF16)

    for p in range(4):
        q = _rope128(proj(C_QA + p * LANES, LANES), c, sa, sb)
        qa_ref[2 * p] = jnp.concatenate([q[:, :HEAD_DIM].astype(BF16), zeros64], axis=1)
        qa_ref[2 * p + 1] = jnp.concatenate([q[:, HEAD_DIM:].astype(BF16), zeros64], axis=1)

    a_onehot = (lane - HEAD_DIM == row // A_BLOCK).astype(BF16)
    ksum = []
    for p in range(2):
        k = _rope128(proj(C_KA + p * LANES, LANES), c, sa, sb)
        v = proj(C_VA + p * LANES, LANES)
        kva_ref[:, p * LANES:(p + 1) * LANES] = k
        kva_ref[:, 256 + p * LANES:256 + (p + 1) * LANES] = v
        kb = k.astype(BF16)
        vb = v.astype(BF16)
        kaug_ref[2 * p] = jnp.concatenate([kb[:, :HEAD_DIM], a_onehot[:, HEAD_DIM:]], axis=1)
        kaug_ref[2 * p + 1] = jnp.concatenate([kb[:, HEAD_DIM:], a_onehot[:, HEAD_DIM:]], axis=1)
        va_ref[2 * p] = vb[:, :HEAD_DIM]
        va_ref[2 * p + 1] = vb[:, HEAD_DIM:]
        if with_kmean:
            ksum.append(jnp.sum(k.reshape(tm // A_BLOCK, A_BLOCK, LANES), axis=1))
    if with_kmean:
        kmean_ref[0] = jnp.concatenate(ksum, axis=1) * (1.0 / A_BLOCK)
    else:
        kmean_ref[...] = jnp.zeros(kmean_ref.shape, F32)

    for p in range(4):
        q = proj(C_QB + p * LANES, LANES) * SCALE
        qr = _rope128(q, c, sa, sb)
        qb_ref[2 * p] = q[:, :HEAD_DIM].astype(BF16)
        qb_ref[2 * p + 1] = q[:, HEAD_DIM:].astype(BF16)
        qbr_ref[2 * p] = jnp.concatenate([qr[:, :HEAD_DIM].astype(BF16), zeros64], axis=1)
        qbr_ref[2 * p + 1] = jnp.concatenate([qr[:, HEAD_DIM:].astype(BF16), zeros64], axis=1)

    kc = proj(C_KC, LANES)
    vc = proj(C_VC, LANES)
    ks = _rope128(proj(C_KS, LANES), c, sa, sb)
    vs = proj(C_VS, LANES)
    nsa_ref[:, 0:128] = kc
    nsa_ref[:, 128:256] = vc
    nsa_ref[:, 256:384] = ks
    nsa_ref[:, 384:512] = vs
    kcvc_ref[0] = kc[:, :HEAD_DIM]
    kcvc_ref[1] = kc[:, HEAD_DIM:]
    kcvc_ref[2] = vc[:, :HEAD_DIM]
    kcvc_ref[3] = vc[:, HEAD_DIM:]
    s_onehot = (lane - HEAD_DIM == (row // SEL_BLOCK) % HEAD_DIM).astype(BF16)
    ksb = ks.astype(BF16)
    vsb = vs.astype(BF16)
    ksaug_ref[0] = jnp.concatenate([ksb[:, :HEAD_DIM], s_onehot[:, HEAD_DIM:]], axis=1)
    ksaug_ref[1] = jnp.concatenate([ksb[:, HEAD_DIM:], s_onehot[:, HEAD_DIM:]], axis=1)
    vs_ref[0] = vsb[:, :HEAD_DIM]
    vs_ref[1] = vsb[:, HEAD_DIM:]

    kw = _rope128(proj(C_KW, LANES), c, sa, sb)
    vw = proj(C_VW, LANES)
    win_ref[:, 0:128] = kw
    win_ref[:, 128:256] = vw
    kwb = kw.astype(BF16)
    vwb = vw.astype(BF16)
    kw_ref[0] = kwb[:, :HEAD_DIM]
    kw_ref[1] = kwb[:, HEAD_DIM:]
    vw_ref[0] = vwb[:, :HEAD_DIM]
    vw_ref[1] = vwb[:, HEAD_DIM:]

    gates_ref[...] = jax.nn.sigmoid(proj(C_G, LANES) + bg_ref[...])


def _rope_tables(pos):
    half = ROPE_DIM // 2
    inv = ROPE_THETA ** (-2.0 * jnp.arange(half, dtype=F32) / ROPE_DIM)
    ang = pos.astype(F32)[:, None] * inv[None, :]
    cos, sin = jnp.cos(ang), jnp.sin(ang)
    t = pos.shape[0]
    ones = jnp.ones((t, HEAD_DIM - ROPE_DIM), F32)
    zeros = jnp.zeros((t, HEAD_DIM - ROPE_DIM), F32)
    zh = jnp.zeros((t, half), F32)
    c = jnp.concatenate([cos, cos, ones], axis=1)
    sa = jnp.concatenate([-sin, zh, zeros], axis=1)
    sb = jnp.concatenate([zh, sin, zeros], axis=1)
    return tuple(jnp.concatenate([a, a], axis=1) for a in (c, sa, sb))


def _inproj(x2d, pos, g, w_pad, bg_pad, *, tm, with_kmean):
    t, d = x2d.shape
    nt = t // tm
    c, sa, sb = _rope_tables(pos)
    row_spec = lambda w: pl.BlockSpec((tm, w), lambda i: (i, 0))
    head_spec = lambda n, w: pl.BlockSpec((n, tm, w), lambda i: (0, i, 0))
    full = lambda a: pl.BlockSpec(a.shape, lambda i: (0,) * a.ndim)
    nkm = max(tm // A_BLOCK, 1)
    out_shape = (
        jax.ShapeDtypeStruct((t, 512), F32),
        jax.ShapeDtypeStruct((t, 512), F32),
        jax.ShapeDtypeStruct((t, 256), F32),
        jax.ShapeDtypeStruct((t, LANES), F32),
        jax.ShapeDtypeStruct((A_HEADS, t, LANES), BF16),
        jax.ShapeDtypeStruct((A_KV_HEADS, t, LANES), BF16),
        jax.ShapeDtypeStruct((A_KV_HEADS, t, HEAD_DIM), BF16),
        jax.ShapeDtypeStruct((B_HEADS, t, HEAD_DIM), BF16),
        jax.ShapeDtypeStruct((B_HEADS, t, LANES), BF16),
        jax.ShapeDtypeStruct((B_KV_HEADS, t, LANES), BF16),
        jax.ShapeDtypeStruct((B_KV_HEADS, t, HEAD_DIM), BF16),
        jax.ShapeDtypeStruct((B_KV_HEADS, t, HEAD_DIM), BF16),
        jax.ShapeDtypeStruct((B_KV_HEADS, t, HEAD_DIM), BF16),
        jax.ShapeDtypeStruct((4, t, HEAD_DIM), F32),
        jax.ShapeDtypeStruct((nt, nkm, 256), F32),
    )
    out_specs = (
        row_spec(512), row_spec(512), row_spec(256), row_spec(LANES),
        head_spec(A_HEADS, LANES), head_spec(A_KV_HEADS, LANES), head_spec(A_KV_HEADS, HEAD_DIM),
        head_spec(B_HEADS, HEAD_DIM), head_spec(B_HEADS, LANES), head_spec(B_KV_HEADS, LANES),
        head_spec(B_KV_HEADS, HEAD_DIM), head_spec(B_KV_HEADS, HEAD_DIM), head_spec(B_KV_HEADS, HEAD_DIM),
        head_spec(4, HEAD_DIM),
        pl.BlockSpec((1, nkm, 256), lambda i: (i, 0, 0)),
    )
    return pl.pallas_call(
        functools.partial(_inproj_kernel, tm=tm, with_kmean=with_kmean),
        out_shape=out_shape,
        grid=(nt,),
        in_specs=[row_spec(d), full(g), full(w_pad), full(bg_pad),
                  row_spec(LANES), row_spec(LANES), row_spec(LANES)],
        out_specs=out_specs,
        compiler_params=_cparams("parallel"),
        name="inproj",
    )(x2d, g, w_pad, bg_pad, c, sa, sb)


def _outproj_kernel(oa_ref, ob_ref, x_ref, wa_ref, wb_ref, gpost_ref, gpre_ref, x1_ref, h2_ref):
    mix = _dot(oa_ref[...].astype(BF16), wa_ref[...]) + _dot(ob_ref[...].astype(BF16), wb_ref[...])
    x1 = x_ref[...] + _rms(mix, gpost_ref[...])
    x1_ref[...] = x1
    h2_ref[...] = _rms(x1, gpre_ref[...]).astype(BF16)


def _outproj(oa, ob, x2d, w_out_bf, g_post, g_pre, *, tm):
    t, d = x2d.shape
    half = oa.shape[1]
    row = lambda w: pl.BlockSpec((tm, w), lambda i: (i, 0))
    full = lambda a: pl.BlockSpec(a.shape, lambda i: (0,) * a.ndim)
    wa, wb = w_out_bf[:half], w_out_bf[half:]
    return pl.pallas_call(
        _outproj_kernel,
        out_shape=(jax.ShapeDtypeStruct((t, d), F32), jax.ShapeDtypeStruct((t, d), BF16)),
        grid=(t // tm,),
        in_specs=[row(half), row(half), row(d), full(wa), full(wb), full(g_post), full(g_pre)],
        out_specs=(row(d), row(d)),
        compiler_params=_cparams("parallel"),
        name="outproj",
    )(oa, ob, x2d, wa, wb, g_post, g_pre)


HALO = 8


def _ffn_seq_kernel(h_ref, halo_ref, x1_ref, wg_ref, wv_ref, cg_ref, cv_ref, bg_ref, bv_ref,
                    wd_ref, gpost_ref, y_ref, tailg_ref, tailv_ref, acc_ref, ug_ref, uv_ref, *, tm):
    i, c = pl.program_id(0), pl.program_id(1)

    @pl.when(c == 0)
    def _():
        acc_ref[...] = jnp.zeros(acc_ref.shape, F32)

    keep = (i > 0).astype(F32)

    def conv(w_ref, u_ref, cw_ref, cb_ref, tail_ref):
        u_ref[0:HALO] = _dot(halo_ref[...], w_ref[...]) * keep
        u_ref[HALO:HALO + tm] = _dot(h_ref[...], w_ref[...])
        tail_ref[0] = u_ref[tm:tm + HALO]
        cw = cw_ref[...]
        return (u_ref[pl.ds(HALO - 2, tm), :] * cw[0:1] + u_ref[pl.ds(HALO - 1, tm), :] * cw[1:2]
                + u_ref[pl.ds(HALO, tm), :] * cw[2:3] + cb_ref[...])

    gate = conv(wg_ref, ug_ref, cg_ref, bg_ref, tailg_ref)
    val = conv(wv_ref, uv_ref, cv_ref, bv_ref, tailv_ref)
    acc_ref[...] += _dot((_gelu(gate) * val).astype(BF16), wd_ref[...])

    @pl.when(c == pl.num_programs(1) - 1)
    def _():
        y_ref[...] = x1_ref[...] + _rms(acc_ref[...], gpost_ref[...])


def _ffn_step_kernel(h_ref, p0g_ref, p1g_ref, p0v_ref, p1v_ref, x1_ref, wg_ref, wv_ref, cg_ref, cv_ref,
                     bg_ref, bv_ref, wd_ref, gpost_ref, y_ref, upg_ref, upv_ref, acc_ref):
    c = pl.program_id(0)

    @pl.when(c == 0)
    def _():
        acc_ref[...] = jnp.zeros(acc_ref.shape, F32)

    def conv(w_ref, p0_ref, p1_ref, cw_ref, cb_ref, up_ref):
        u = _dot(h_ref[...], w_ref[...])
        up_ref[...] = u
        cw = cw_ref[...]
        return p0_ref[...] * cw[0:1] + p1_ref[...] * cw[1:2] + u * cw[2:3] + cb_ref[...]

    gate = conv(wg_ref, p0g_ref, p1g_ref, cg_ref, bg_ref, upg_ref)
    val = conv(wv_ref, p0v_ref, p1v_ref, cv_ref, bv_ref, upv_ref)
    acc_ref[...] += _dot((_gelu(gate) * val).astype(BF16), wd_ref[...])

    @pl.when(c == pl.num_programs(0) - 1)
    def _():
        y_ref[...] = x1_ref[...] + _rms(acc_ref[...], gpost_ref[...])


def _ffn_seq(h2, x1, w_up_bf, w_conv, b_conv2d, w_down_bf, g_post, *, tm, ck):
    t, d = x1.shape
    dff = w_down_bf.shape[0]
    nff = dff // ck
    nt = t // tm
    hb = tm // HALO
    gcol = lambda r: pl.BlockSpec((r, ck), lambda i, c: (0, c))
    vcol = lambda r: pl.BlockSpec((r, ck), lambda i, c: (0, nff + c))
    row = pl.BlockSpec((tm, d), lambda i, c: (i, 0))
    tail = pl.BlockSpec((1, HALO, ck), lambda i, c: (i, 0, c))
    y, tg, tv = pl.pallas_call(
        functools.partial(_ffn_seq_kernel, tm=tm),
        out_shape=(jax.ShapeDtypeStruct((t, d), F32),
                   jax.ShapeDtypeStruct((nt, HALO, dff), F32), jax.ShapeDtypeStruct((nt, HALO, dff), F32)),
        grid=(nt, nff),
        in_specs=[row, pl.BlockSpec((HALO, d), lambda i, c: (jnp.maximum(i * hb - 1, 0), 0)), row,
                  gcol(d), vcol(d), gcol(CONV_W), vcol(CONV_W), gcol(1), vcol(1),
                  pl.BlockSpec((ck, d), lambda i, c: (c, 0)),
                  pl.BlockSpec(g_post.shape, lambda i, c: (0, 0))],
        out_specs=(row, tail, tail),
        scratch_shapes=[pltpu.VMEM((tm, d), F32), pltpu.VMEM((tm + HALO, ck), F32),
                        pltpu.VMEM((tm + HALO, ck), F32)],
        compiler_params=_cparams("parallel", "arbitrary"),
        name="ffn_seq",
    )(h2, h2, x1, w_up_bf, w_up_bf, w_conv, w_conv, b_conv2d, b_conv2d, w_down_bf, g_post)
    state = jnp.concatenate([tg[-1, HALO - 2:], tv[-1, HALO - 2:]], axis=1)
    return y, state


def _ffn_step(h2, x1, prev, w_up_bf, w_conv, b_conv2d, w_down_bf, g_post, *, ck):
    t, d = x1.shape
    dff = w_down_bf.shape[0]
    nff = dff // ck
    p0, p1 = prev[:, 0], prev[:, 1]
    gcol = lambda r: pl.BlockSpec((r, ck), lambda c: (0, c))
    vcol = lambda r: pl.BlockSpec((r, ck), lambda c: (0, nff + c))
    row = pl.BlockSpec((t, d), lambda c: (0, 0))
    y, ug, uv = pl.pallas_call(
        _ffn_step_kernel,
        out_shape=(jax.ShapeDtypeStruct((t, d), F32),
                   jax.ShapeDtypeStruct((t, dff), F32), jax.ShapeDtypeStruct((t, dff), F32)),
        grid=(nff,),
        in_specs=[row, gcol(t), gcol(t), vcol(t), vcol(t), row,
                  gcol(d), vcol(d), gcol(CONV_W), vcol(CONV_W), gcol(1), vcol(1),
                  pl.BlockSpec((ck, d), lambda c: (c, 0)),
                  pl.BlockSpec(g_post.shape, lambda c: (0, 0))],
        out_specs=(row, pl.BlockSpec((t, ck), lambda c: (0, c)), pl.BlockSpec((t, ck), lambda c: (0, c))),
        scratch_shapes=[pltpu.VMEM((t, d), F32)],
        compiler_params=_cparams("arbitrary"),
        name="ffn_step",
    )(h2, p0, p1, p0, p1, x1, w_up_bf, w_up_bf, w_conv, w_conv, b_conv2d, b_conv2d, w_down_bf, g_post)
    state = jnp.stack([p1, jnp.concatenate([ug, uv], axis=1)], axis=1)
    return y, state


ATT_KT = 512
MOBA_TQ = ATT_KT


def _softmax_init(m_sc, l_sc, acc_sc):
    m_sc[...] = jnp.full(m_sc.shape, NEG_INF, F32)
    l_sc[...] = jnp.zeros(l_sc.shape, F32)
    acc_sc[...] = jnp.zeros(acc_sc.shape, F32)


def _two_pass_attention(q3_ref, k_ref, v_ref, n_full, tiles_per_win, qpos, m_sc, l_sc, acc_sc, p_sc):
    nwin, rows, _ = q3_ref.shape
    kt = p_sc.shape[2]
    groups = [slice(j * LANES, (j + 1) * LANES) for j in range(kt // LANES)]

    def scores(t, ntile=1):
        off = pl.multiple_of(t * kt, kt)
        w = t // tiles_per_win if nwin > 1 else 0
        return _dot_t(q3_ref[w], k_ref[0, pl.ds(off, ntile * kt), :])

    def causal(s, t):
        kpos = t * kt + lax.broadcasted_iota(jnp.int32, (rows, kt), 1)
        return jnp.where(kpos <= qpos, s, NEG_INF)

    def take_max(s):
        mx = s[:, :LANES]
        for j in range(1, s.shape[1] // LANES):
            mx = jnp.maximum(mx, s[:, j * LANES:(j + 1) * LANES])
        m_sc[...] = jnp.maximum(m_sc[...], mx)

    def probs(s):
        m = m_sc[...]
        ps = [jnp.exp(s[:, gs] - m) for gs in groups]
        tot = ps[0]
        for p in ps[1:]:
            tot = tot + p
        l_sc[...] += tot
        return jnp.concatenate(ps, axis=1).astype(BF16)

    def add_pv(p, t):
        off = pl.multiple_of(t * kt, kt)
        acc_sc[...] += _dot(p, v_ref[0, pl.ds(off, kt), :])

    _softmax_init(m_sc, l_sc, acc_sc)

    assert nwin == 1 or tiles_per_win % 2 == 0

    def max_body(u, carry):
        take_max(scores(2 * u, 2))
        return carry

    lax.fori_loop(0, n_full // 2, max_body, 0)
    t_odd = jnp.maximum(n_full - 1, 0)
    take_max(causal(scores(t_odd), t_odd))
    take_max(causal(scores(n_full), n_full))
    m_sc[...] = jnp.broadcast_to(jnp.max(m_sc[...], axis=1, keepdims=True), m_sc.shape)

    p_sc[1] = jnp.zeros(p_sc.shape[1:], BF16)

    def sum_body(t, carry):
        add_pv(p_sc[(t + 1) % 2], jnp.maximum(t - 1, 0))
        p_sc[t % 2] = probs(scores(t))
        return carry

    lax.fori_loop(0, n_full, sum_body, 0)
    add_pv(p_sc[(n_full + 1) % 2], jnp.maximum(n_full - 1, 0))
    add_pv(probs(causal(scores(n_full), n_full)), n_full)
    return acc_sc[...] / jnp.sum(l_sc[...], axis=1, keepdims=True)


def _top_select(v, lane, forced, rounds):
    sel = forced
    lane = lane.astype(F32)
    for _ in range(rounds):
        mx = jnp.max(v, axis=1, keepdims=True)
        idx = jnp.min(jnp.where(v == mx, lane, 1e9), axis=1, keepdims=True)
        pick = (lane == idx) & (mx > NEG_INF)
        sel = sel | pick
        v = jnp.where(pick, NEG_INF, v)
    return sel


def _masked_softmax(s, mask):
    s = jnp.where(mask, s, NEG_INF)
    m = jnp.max(s, axis=1, keepdims=True)
    m = jnp.where(m > NEG_INF, m, 0.0)
    e = jnp.where(mask, jnp.exp(s - m), 0.0)
    d = jnp.sum(e, axis=1, keepdims=True)
    return e / jnp.where(d > 0, d, 1.0)


def _moba_prompt_kernel(q_ref, km_ref, k_ref, v_ref, o_ref, m_sc, l_sc, acc_sc, q3_sc, p_sc):
    i = pl.program_id(1)
    rows = 2 * MOBA_TQ
    q = q_ref[...].reshape(rows, LANES)
    gate = _dot(q, km_ref[0])
    lane = lax.broadcasted_iota(jnp.int32, (rows, LANES), 1)
    qpos = i * MOBA_TQ + lax.broadcasted_iota(jnp.int32, (rows, 1), 0) % MOBA_TQ
    cur = qpos // A_BLOCK
    past = (lane >= HEAD_DIM) & (lane < HEAD_DIM + cur)
    sel = _top_select(jnp.where(past, gate, NEG_INF), lane, lane == HEAD_DIM + cur, A_TOPK)
    bias = jnp.where(sel | (lane < HEAD_DIM), 0.0, MASK_BIAS)
    q3_sc[0] = (q.astype(F32) * SCALE + bias).astype(BF16)

    o = _two_pass_attention(q3_sc, k_ref, v_ref, i, None, qpos, m_sc, l_sc, acc_sc, p_sc)
    o_ref[:, 0:HEAD_DIM] = o[:MOBA_TQ]
    o_ref[:, HEAD_DIM:LANES] = o[MOBA_TQ:]


def _gate_matrix(kmean):
    nblk = kmean.shape[0]
    km = kmean.reshape(nblk, A_KV_HEADS, HEAD_DIM).transpose(1, 2, 0)
    km = jnp.pad(km, ((0, 0), (0, LANES - HEAD_DIM), (HEAD_DIM, LANES - HEAD_DIM - nblk)))
    return km.astype(BF16)


def _moba_prompt(qa, km, kaug, va):
    t = qa.shape[1]
    nt = t // MOBA_TQ
    assert t // A_BLOCK <= HEAD_DIM, "key-block one-hot occupies 64 lanes"
    assert t % MOBA_TQ == 0 and MOBA_TQ % A_BLOCK == 0
    rows = 2 * MOBA_TQ
    return pl.pallas_call(
        _moba_prompt_kernel,
        out_shape=jax.ShapeDtypeStruct((t, A_HEADS * HEAD_DIM), F32),
        grid=(A_KV_HEADS, nt),
        in_specs=[pl.BlockSpec((2, MOBA_TQ, LANES), lambda g, i: (g, i, 0)),
                  pl.BlockSpec((1, LANES, LANES), lambda g, i: (g, 0, 0)),
                  pl.BlockSpec((1, t, LANES), lambda g, i: (g, 0, 0)),
                  pl.BlockSpec((1, t, HEAD_DIM), lambda g, i: (g, 0, 0))],
        out_specs=pl.BlockSpec((MOBA_TQ, LANES), lambda g, i: (i, g)),
        scratch_shapes=[pltpu.VMEM((rows, LANES), F32), pltpu.VMEM((rows, LANES), F32),
                        pltpu.VMEM((rows, HEAD_DIM), F32), pltpu.VMEM((1, rows, LANES), BF16),
                        pltpu.VMEM((2, rows, ATT_KT), BF16)],
        compiler_params=_cparams("parallel", "parallel"),
        name="moba_prompt",
    )(qa, km, kaug, va)


def _compress_kernel(x_ref, pelo_ref, pehi_ref, wlo_ref, whi_ref, b1_ref, w2_ref, o_ref, hi_sc, *, nch):
    x = x_ref[0]
    lo = _dot((x + pelo_ref[0]).astype(BF16), wlo_ref[0])
    hi_sc[0:nch] = _dot((x + pehi_ref[0]).astype(BF16), whi_ref[0])
    hi_sc[nch:nch + 8] = jnp.zeros((8, CMP_HIDDEN), F32)
    hid = _gelu(lo + hi_sc[pl.ds(1, nch), :] + b1_ref[0])
    o_ref[0] = _dot(hid.astype(BF16), w2_ref[0]).astype(o_ref.dtype)


def _compress_weights(cmp_pe, cmp_w1, cmp_b1, cmp_w2):
    flat = CMP_STRIDE * HEAD_DIM
    pelo = cmp_pe[:, :CMP_STRIDE].reshape(2, 1, flat)
    pehi = cmp_pe[:, CMP_STRIDE:].reshape(2, 1, flat)
    wlo = cmp_w1[:, :CMP_STRIDE].reshape(2, flat, CMP_HIDDEN).astype(BF16)
    whi = cmp_w1[:, CMP_STRIDE:].reshape(2, flat, CMP_HIDDEN).astype(BF16)
    return pelo, pehi, wlo, whi, cmp_b1.reshape(2, 1, CMP_HIDDEN), cmp_w2.astype(BF16)


def _compress(xch, cw):
    _, nch, flat = xch.shape
    pelo, pehi, wlo, whi, b1, w2 = cw
    kind = lambda shape: pl.BlockSpec((1,) + shape, lambda j: (j // B_KV_HEADS, 0, 0))
    return pl.pallas_call(
        functools.partial(_compress_kernel, nch=nch),
        out_shape=jax.ShapeDtypeStruct((4, nch, HEAD_DIM), BF16),
        grid=(4,),
        in_specs=[pl.BlockSpec((1, nch, flat), lambda j: (j, 0, 0)),
                  kind((1, flat)), kind((1, flat)), kind((flat, CMP_HIDDEN)), kind((flat, CMP_HIDDEN)),
                  kind((1, CMP_HIDDEN)), kind((CMP_HIDDEN, HEAD_DIM))],
        out_specs=pl.BlockSpec((1, nch, HEAD_DIM), lambda j: (j, 0, 0)),
        scratch_shapes=[pltpu.VMEM((nch + 8, CMP_HIDDEN), F32)],
        compiler_params=_cparams("parallel"),
        name="compress",
    )(xch, pelo, pehi, wlo, whi, b1, w2)


def _importance_matrix(ncmp, nsel):
    ratio = SEL_BLOCK // CMP_STRIDE
    n = np.arange(ncmp)[:, None]
    j = np.arange(nsel)[None, :]
    own = (n // ratio == j)
    last = (n % ratio == ratio - 1)
    m = np.where(own & ~last, 1.0, 0.0) + np.where(last & (own | (n // ratio == j - 1)), 0.5, 0.0)
    return jnp.asarray(m, BF16)


NSA_TQ = 256
WIN_BLOCKS = WINDOW // NSA_TQ + 1


def _nsa_prompt_kernel(*refs, nsel, ncmp):
    (qb_ref, qbr_ref, kc_ref, vc_ref, imp_ref, ks_ref, vs_ref) = refs[:7]
    kw_refs = refs[7:7 + WIN_BLOCKS]
    vw_refs = refs[7 + WIN_BLOCKS:7 + 2 * WIN_BLOCKS]
    g_ref, o_ref, m_sc, l_sc, acc_sc, q3_sc, p_sc = refs[7 + 2 * WIN_BLOCKS:]
    i = pl.program_id(1)
    rows = B_GROUP * NSA_TQ
    q0 = i * NSA_TQ
    q = qb_ref[...].reshape(rows, HEAD_DIM)
    qr = qbr_ref[...].reshape(rows, LANES)

    s = _dot_t(q, kc_ref[0])
    qpos_c = q0 + lax.broadcasted_iota(jnp.int32, (rows, ncmp), 0) % NSA_TQ
    n_c = lax.broadcasted_iota(jnp.int32, (rows, ncmp), 1)
    p = _masked_softmax(s, n_c * CMP_STRIDE + (CMP_LEN - 1) <= qpos_c)
    o_cmp = _dot(p.astype(BF16), vc_ref[0])

    psum = p[0:NSA_TQ]
    for j in range(1, B_GROUP):
        psum = psum + p[j * NSA_TQ:(j + 1) * NSA_TQ]
    p_hi = psum.astype(BF16)
    p_lo = (psum - p_hi.astype(F32)).astype(BF16)
    imp = _dot(p_hi, imp_ref[...]) + _dot(p_lo, imp_ref[...])
    blk = lax.broadcasted_iota(jnp.int32, (NSA_TQ, nsel), 1)
    cur = (q0 + lax.broadcasted_iota(jnp.int32, (NSA_TQ, nsel), 0)) // SEL_BLOCK
    forced = (blk == 0) | (blk == cur) | (blk == cur - 1)
    cand = jnp.where(blk > cur, NEG_INF, jnp.where(forced, jnp.inf, imp))
    sel = _top_select(cand, blk, jnp.zeros((NSA_TQ, nsel), jnp.bool_), min(SEL_TOPN, nsel))
    selbias = jnp.where(sel, 0.0, MASK_BIAS)

    kt_last = (q0 + NSA_TQ - 1) // ATT_KT
    tiles_per_win = HEAD_DIM * SEL_BLOCK // ATT_KT
    qr32 = qr.astype(F32)
    for w in range(-(-nsel // HEAD_DIM)):
        nb = min(HEAD_DIM, nsel - w * HEAD_DIM)
        pieces = [jnp.zeros((NSA_TQ, HEAD_DIM), F32), selbias[:, w * HEAD_DIM:w * HEAD_DIM + nb]]
        if nb < HEAD_DIM:
            pieces.append(jnp.zeros((NSA_TQ, HEAD_DIM - nb), F32))
        bias_w = jnp.concatenate(pieces, axis=1)
        q3_sc[w] = (qr32 + jnp.concatenate([bias_w] * B_GROUP, axis=0)).astype(BF16)

    qpos_col = q0 + lax.broadcasted_iota(jnp.int32, (rows, 1), 0) % NSA_TQ
    o_sel = _two_pass_attention(q3_sc, ks_ref, vs_ref, kt_last, tiles_per_win, qpos_col,
                                m_sc, l_sc, acc_sc, p_sc)

    kband = jnp.concatenate([r[0] for r in kw_refs], axis=0)
    vband = jnp.concatenate([r[0] for r in vw_refs], axis=0)
    nband = WIN_BLOCKS * NSA_TQ
    s = _dot_t(qr[:, :HEAD_DIM], kband)
    qpos = q0 + lax.broadcasted_iota(jnp.int32, (rows, nband), 0) % NSA_TQ
    kpos = q0 - WINDOW + lax.broadcasted_iota(jnp.int32, (rows, nband), 1)
    p = _masked_softmax(s, (kpos <= qpos) & (kpos >= qpos - WINDOW) & (kpos >= 0))
    o_win = _dot(p.astype(BF16), vband)

    gts = g_ref[0]
    for j in range(B_GROUP):
        sl = slice(j * NSA_TQ, (j + 1) * NSA_TQ)
        o = (gts[:, 3 * j:3 * j + 1] * o_cmp[sl] + gts[:, 3 * j + 1:3 * j + 2] * o_sel[sl]
             + gts[:, 3 * j + 2:3 * j + 3] * o_win[sl])
        o_ref[:, j * HEAD_DIM:(j + 1) * HEAD_DIM] = o


def _nsa_prompt(qb, qbr, kvcmp, ksaug, vs, kw, vw, gates):
    t = qb.shape[1]
    nt = t // NSA_TQ
    nsel = t // SEL_BLOCK
    ncmp = kvcmp.shape[1]
    rows = B_GROUP * NSA_TQ
    impm = _importance_matrix(ncmp, nsel)
    res = lambda w: pl.BlockSpec((1, t, w), lambda g, i: (g, 0, 0))
    band = [pl.BlockSpec((1, NSA_TQ, HEAD_DIM),
                         functools.partial(lambda g, i, j: (g, jnp.maximum(i - (WIN_BLOCKS - 1) + j, 0), 0), j=j))
            for j in range(WIN_BLOCKS)]
    return pl.pallas_call(
        functools.partial(_nsa_prompt_kernel, nsel=nsel, ncmp=ncmp),
        out_shape=jax.ShapeDtypeStruct((t, B_HEADS * HEAD_DIM), F32),
        grid=(B_KV_HEADS, nt),
        in_specs=[pl.BlockSpec((B_GROUP, NSA_TQ, HEAD_DIM), lambda g, i: (g, i, 0)),
                  pl.BlockSpec((B_GROUP, NSA_TQ, LANES), lambda g, i: (g, i, 0)),
                  pl.BlockSpec((1, ncmp, HEAD_DIM), lambda g, i: (g, 0, 0)),
                  pl.BlockSpec((1, ncmp, HEAD_DIM), lambda g, i: (B_KV_HEADS + g, 0, 0)),
                  pl.BlockSpec(impm.shape, lambda g, i: (0, 0)),
                  res(LANES), res(HEAD_DIM)] + band + band +
                 [pl.BlockSpec((1, NSA_TQ, 3 * B_GROUP), lambda g, i: (g, i, 0))],
        out_specs=pl.BlockSpec((NSA_TQ, B_GROUP * HEAD_DIM), lambda g, i: (i, g)),
        scratch_shapes=[pltpu.VMEM((rows, LANES), F32), pltpu.VMEM((rows, LANES), F32),
                        pltpu.VMEM((rows, HEAD_DIM), F32),
                        pltpu.VMEM((-(-nsel // HEAD_DIM), rows, LANES), BF16),
                        pltpu.VMEM((2, rows, ATT_KT), BF16)],
        compiler_params=_cparams("parallel", "parallel"),
        name="nsa_prompt",
    )(qb, qbr, kvcmp, kvcmp, impm, ksaug, vs, *([kw] * WIN_BLOCKS), *([vw] * WIN_BLOCKS), gates)


def _top_indices(v, lane, rounds):
    out = jnp.zeros((v.shape[0], LANES), jnp.int32)
    slot = lax.broadcasted_iota(jnp.int32, out.shape, 1)
    for r in range(rounds):
        mx = jnp.max(v, axis=1, keepdims=True)
        idx = jnp.min(jnp.where(v == mx, lane, 1 << 20), axis=1, keepdims=True)
        out = jnp.where(slot == r, idx, out)
        v = jnp.where(lane == idx, NEG_INF, v)
    return out


def _bf16_round(x):
    return x.astype(BF16).astype(F32)


def _moba_kmean_kernel(pt_ref, *refs, pg):
    pages, o_ref = refs[:pg], refs[pg]
    step = pl.program_id(1)
    per_blk = A_BLOCK // pages[0].shape[2]
    nb = pg // per_blk

    @pl.when(step == 0)
    def _():
        o_ref[...] = jnp.zeros(o_ref.shape, F32)

    lane = lax.broadcasted_iota(jnp.int32, o_ref.shape[1:], 1)
    out = o_ref[0]
    for j in range(nb):
        tot = pages[j * per_blk][0]
        for r in pages[j * per_blk + 1:(j + 1) * per_blk]:
            tot = tot + r[0]
        mean = jnp.sum(tot, axis=1, keepdims=True) * (1.0 / A_BLOCK)
        out = jnp.where(lane == step * nb + j, mean, out)
    o_ref[0] = out


def _moba_kmean(cache_t, page_table, *, pg):
    db, npg = page_table.shape
    page = cache_t.shape[2]
    kw = A_KV_HEADS * HEAD_DIM
    assert npg * page // A_BLOCK <= LANES
    specs = [pl.BlockSpec((1, kw, page), functools.partial(lambda b, s, pt, j: (pt[b, s * pg + j], 0, 0), j=j))
             for j in range(pg)]
    return pl.pallas_call(
        functools.partial(_moba_kmean_kernel, pg=pg),
        out_shape=jax.ShapeDtypeStruct((db, kw, LANES), F32),
        grid_spec=pltpu.PrefetchScalarGridSpec(
            num_scalar_prefetch=1, grid=(db, npg // pg), in_specs=specs,
            out_specs=pl.BlockSpec((1, kw, LANES), lambda b, s, pt: (b, 0, 0))),
        compiler_params=_cparams("parallel", "arbitrary"),
        name="moba_kmean",
    )(page_table, *([cache_t] * pg))


def _moba_gate_kernel(q_ref, km_ref, idx_ref, *, nblk):
    q = q_ref[0][:, :HEAD_DIM].astype(BF16)
    head = lax.broadcasted_iota(jnp.int32, (A_HEADS, LANES), 0)
    lane = lax.broadcasted_iota(jnp.int32, (A_HEADS, LANES), 1)
    gate = jnp.zeros((A_HEADS, LANES), F32)
    for g in range(A_KV_HEADS):
        kmg = km_ref[0][g * HEAD_DIM:(g + 1) * HEAD_DIM, :].astype(BF16)
        gate = jnp.where(head // (A_HEADS // A_KV_HEADS) == g, _dot(q, kmg), gate)
    idx_ref[0] = _top_indices(jnp.where(lane < nblk, gate, NEG_INF), lane, A_TOPK)


def _moba_gate(qa_rows, kmean_t, *, nblk):
    db, kw, _ = kmean_t.shape
    return pl.pallas_call(
        functools.partial(_moba_gate_kernel, nblk=nblk),
        out_shape=jax.ShapeDtypeStruct((db, A_HEADS, LANES), jnp.int32),
        grid=(db,),
        in_specs=[pl.BlockSpec((1, A_HEADS, LANES), lambda b: (b, 0, 0)),
                  pl.BlockSpec((1, kw, LANES), lambda b: (b, 0, 0))],
        out_specs=pl.BlockSpec((1, A_HEADS, LANES), lambda b: (b, 0, 0)),
        compiler_params=_cparams("parallel"),
        name="moba_gate",
    )(qa_rows, kmean_t)


def _gathered_attention(q, k_pages, v_pages, masks, k_own, v_own):
    qb = q.astype(BF16)
    ss = [_dot(qb, k.astype(BF16)) for k in k_pages]
    ss = [s if mk is None else jnp.where(mk, s, NEG_INF) for s, mk in zip(ss, masks)]
    s_own = jnp.sum(q * _bf16_round(k_own), axis=1, keepdims=True)
    m = s_own
    for s in ss:
        m = jnp.maximum(m, jnp.max(s, axis=1, keepdims=True))
    e_own = jnp.exp(s_own - m)
    d = e_own
    acc = _bf16_round(e_own) * _bf16_round(v_own)
    for s, v in zip(ss, v_pages):
        e = jnp.exp(s - m)
        d = d + jnp.sum(e, axis=1, keepdims=True)
        acc = acc + _dot_t(e.astype(BF16), v.astype(BF16))
    return acc / d


def _moba_sample_kernel(pt_ref, ix_ref, q_ref, *refs, npick):
    k_refs, v_refs = refs[:npick], refs[npick:2 * npick]
    kn_ref, vn_ref, o_ref, o_sc = refs[2 * npick:]
    h = pl.program_id(1)
    q = q_ref[0][:, :HEAD_DIM] * SCALE
    o_sc[...] = _gathered_attention(q, [r[0] for r in k_refs], [r[0] for r in v_refs], [None] * npick,
                                    kn_ref[0], vn_ref[0])
    o_ref[0, pl.ds(h, 1), :] = o_sc[pl.ds(h, 1), :]


def _moba_sample(cache_t, page_table, idx, qa_rows, k_new, v_new):
    db = page_table.shape[0]
    page = cache_t.shape[2]
    npage = A_BLOCK // page
    rep = A_HEADS // A_KV_HEADS

    def kv_spec(s, j, row_blk0):
        def imap(b, h, pt, ix):
            blk = ix[(b * A_HEADS + h) * A_TOPK + s]
            return (pt[b, blk * npage + j], row_blk0 + h // rep, 0)
        return pl.BlockSpec((1, HEAD_DIM, page), imap)

    picks = [(s, j) for s in range(A_TOPK) for j in range(npage)]
    per_seq = lambda w: pl.BlockSpec((1, A_HEADS, w), lambda b, h, pt, ix: (b, 0, 0))
    return pl.pallas_call(
        functools.partial(_moba_sample_kernel, npick=len(picks)),
        out_shape=jax.ShapeDtypeStruct((db, A_HEADS, HEAD_DIM), F32),
        grid_spec=pltpu.PrefetchScalarGridSpec(
            num_scalar_prefetch=2, grid=(db, A_HEADS),
            in_specs=[per_seq(LANES)]
                     + [kv_spec(s, j, 0) for s, j in picks] + [kv_spec(s, j, A_KV_HEADS) for s, j in picks]
                     + [per_seq(HEAD_DIM), per_seq(HEAD_DIM)],
            out_specs=per_seq(HEAD_DIM),
            scratch_shapes=[pltpu.VMEM((A_HEADS, HEAD_DIM), F32)]),
        compiler_params=_cparams("parallel", "arbitrary"),
        name="moba_sample",
    )(page_table, idx, qa_rows, *([cache_t] * (2 * len(picks))), k_new, v_new)


def _nsa_flatten_kernel(pt_ref, *refs, pg, nch):
    pages = refs[:pg]
    pelo_ref, pehi_ref, wlo_ref, whi_ref, b1_ref, w2_ref, o_ref, x_sc, hi_sc, rows_sc = refs[pg:]
    step = pl.program_id(1)
    page = pages[0].shape[2]
    nc = pg * page // CMP_STRIDE
    base = pl.multiple_of(step * nc, nc)
    lane = lax.broadcasted_iota(jnp.int32, (nc, LANES), 1)
    lo_half = lane < HEAD_DIM
    for j, r in enumerate(pages):
        for pr in range(2):
            rows_sc[pr, j * page:(j + 1) * page, :] = r[0, pr * LANES:(pr + 1) * LANES, :].T
    for u in range(CMP_STRIDE // 2):
        for pr in range(2):
            ap = rows_sc[pr, pl.ds(2 * u, nc, stride=CMP_STRIDE), :]
            bp = rows_sc[pr, pl.ds(2 * u + 1, nc, stride=CMP_STRIDE), :]
            x_sc[2 * pr, pl.ds(base, nc), u * LANES:(u + 1) * LANES] = jnp.where(
                lo_half, ap, pltpu.roll(bp, HEAD_DIM, 1))
            x_sc[2 * pr + 1, pl.ds(base, nc), u * LANES:(u + 1) * LANES] = jnp.where(
                lo_half, pltpu.roll(ap, HEAD_DIM, 1), bp)

    @pl.when(step == pl.num_programs(1) - 1)
    def _():
        hi_sc[nch:nch + 8] = jnp.zeros((8, CMP_HIDDEN), F32)
        for j in range(4):
            c = j // B_KV_HEADS
            x = x_sc[j]
            lo = _dot((x + pelo_ref[c]).astype(BF16), wlo_ref[c])
            hi_sc[0:nch] = _dot((x + pehi_ref[c]).astype(BF16), whi_ref[c])
            hid = _gelu(lo + hi_sc[pl.ds(1, nch), :] + b1_ref[c])
            o_ref[0, j] = _dot(hid.astype(BF16), w2_ref[c]).astype(o_ref.dtype)


def _nsa_sample_compress(cache_t, page_table, cw, *, pg):
    db, npg = page_table.shape
    page = cache_t.shape[2]
    assert page == LANES
    nch = npg * page // CMP_STRIDE
    flat = CMP_STRIDE * HEAD_DIM
    pelo, pehi, wlo, whi, b1, w2 = cw
    full = lambda a: pl.BlockSpec(a.shape, lambda b, s, pt: (0,) * a.ndim)
    specs = [pl.BlockSpec((1, 2 * LANES, page), functools.partial(lambda b, s, pt, j: (pt[b, s * pg + j], 0, 0), j=j))
             for j in range(pg)]
    return pl.pallas_call(
        functools.partial(_nsa_flatten_kernel, pg=pg, nch=nch),
        out_shape=jax.ShapeDtypeStruct((db, 4, nch, HEAD_DIM), BF16),
        grid_spec=pltpu.PrefetchScalarGridSpec(
            num_scalar_prefetch=1, grid=(db, npg // pg),
            in_specs=specs + [full(a) for a in (pelo, pehi, wlo, whi, b1, w2)],
            out_specs=pl.BlockSpec((1, 4, nch, HEAD_DIM), lambda b, s, pt: (b, 0, 0, 0)),
            scratch_shapes=[pltpu.VMEM((4, nch, flat), F32), pltpu.VMEM((nch + 8, CMP_HIDDEN), F32),
                            pltpu.VMEM((2, pg * page, LANES), F32)]),
        compiler_params=_cparams("parallel", "arbitrary"),
        name="nsa_sample_compress",
    )(page_table, *([cache_t] * pg), pelo, pehi, wlo, whi, b1, w2)


def _rows_by_group(per_group):
    head = lax.broadcasted_iota(jnp.int32, per_group[0].shape, 0)
    out = per_group[0]
    for g in range(1, B_KV_HEADS):
        out = jnp.where(head // B_GROUP == g, per_group[g], out)
    return out


def _nsa_sample_select_kernel(q_ref, qr_ref, kv_ref, imp_ref, win_ref, wnew_ref, g_ref, idx_ref, ocw_ref,
                              *, past, nsel_past, ncmp):
    q = q_ref[0][:, :HEAD_DIM].astype(BF16)
    qr = qr_ref[0][:, :HEAD_DIM]
    gts = g_ref[0]
    glane = lax.broadcasted_iota(jnp.int32, (B_HEADS, LANES), 1)
    ghead = lax.broadcasted_iota(jnp.int32, (B_HEADS, LANES), 0)
    gate = lambda c: jnp.sum(jnp.where(glane == 3 * ghead + c, gts, 0.0), axis=1, keepdims=True)

    s = _rows_by_group([_dot_t(q, kv_ref[0, g]) for g in range(B_KV_HEADS)])
    n_c = lax.broadcasted_iota(jnp.int32, (B_HEADS, ncmp), 1)
    p = _masked_softmax(s, n_c * CMP_STRIDE + (CMP_LEN - 1) <= past)
    o_cmp = _rows_by_group([_dot(p.astype(BF16), kv_ref[0, B_KV_HEADS + g]) for g in range(B_KV_HEADS)])

    head = lax.broadcasted_iota(jnp.int32, (B_HEADS, ncmp), 0)
    psum = jnp.zeros((B_HEADS, ncmp), F32)
    for g in range(B_KV_HEADS):
        tot = jnp.sum(jnp.where(head // B_GROUP == g, p, 0.0), axis=0, keepdims=True)
        psum = jnp.where(head == g, tot, psum)
    p_hi = psum.astype(BF16)
    p_lo = (psum - p_hi.astype(F32)).astype(BF16)
    imp = _dot(p_hi, imp_ref[...]) + _dot(p_lo, imp_ref[...])
    blk = lax.broadcasted_iota(jnp.int32, (B_HEADS, nsel_past), 1)
    cand = jnp.where((blk == 0) | (blk == nsel_past - 1), jnp.inf, imp)
    idx_ref[0] = _top_indices(cand, blk, min(SEL_TOPN, nsel_past + 1) - 1)

    wk = [win_ref[0][g * HEAD_DIM:(g + 1) * HEAD_DIM, :] for g in range(B_KV_HEADS)]
    wv = [win_ref[0][(B_KV_HEADS + g) * HEAD_DIM:(B_KV_HEADS + g + 1) * HEAD_DIM, :] for g in range(B_KV_HEADS)]
    nk = [wnew_ref[0][:, g * HEAD_DIM:(g + 1) * HEAD_DIM] for g in range(B_KV_HEADS)]
    nv = [wnew_ref[0][:, (B_KV_HEADS + g) * HEAD_DIM:(B_KV_HEADS + g + 1) * HEAD_DIM] for g in range(B_KV_HEADS)]
    s_w = _rows_by_group([_dot(qr.astype(BF16), k.astype(BF16)) for k in wk])
    s_n = _rows_by_group([jnp.sum(qr * _bf16_round(k), axis=1, keepdims=True) for k in nk])
    m = jnp.maximum(jnp.max(s_w, axis=1, keepdims=True), s_n)
    e_w, e_n = jnp.exp(s_w - m), jnp.exp(s_n - m)
    d = jnp.sum(e_w, axis=1, keepdims=True) + e_n
    o_win = _rows_by_group([_dot_t(e_w.astype(BF16), v.astype(BF16)) + _bf16_round(e_n) * _bf16_round(nvg)
                            for v, nvg in zip(wv, nv)]) / d

    ocw_ref[0] = jnp.concatenate([gate(0) * o_cmp, gate(2) * o_win], axis=1)


def _nsa_sample_select(qb_rows, qbr_rows, kvcmp, win_cache, win_new, gates, *, past):
    db, _, ncmp, _ = kvcmp.shape
    nsel_past = past // SEL_BLOCK
    impm = _importance_matrix(ncmp, nsel_past)
    row3 = lambda a: pl.BlockSpec((1,) + a.shape[1:], lambda b: (b,) + (0,) * (a.ndim - 1))
    kern = functools.partial(_nsa_sample_select_kernel, past=past, nsel_past=nsel_past, ncmp=ncmp)
    return pl.pallas_call(
        kern,
        out_shape=(jax.ShapeDtypeStruct((db, B_HEADS, LANES), jnp.int32),
                   jax.ShapeDtypeStruct((db, B_HEADS, LANES), F32)),
        grid=(db,),
        in_specs=[row3(qb_rows), row3(qbr_rows), row3(kvcmp), pl.BlockSpec(impm.shape, lambda b: (0, 0)),
                  row3(win_cache), row3(win_new), row3(gates)],
        out_specs=(pl.BlockSpec((1, B_HEADS, LANES), lambda b: (b, 0, 0)),
                   pl.BlockSpec((1, B_HEADS, LANES), lambda b: (b, 0, 0))),
        compiler_params=_cparams("parallel"),
        name="nsa_sample_select",
    )(qb_rows, qbr_rows, kvcmp, impm, win_cache, win_new, gates)


def _nsa_sample_attend_kernel(pt_ref, ix_ref, qr_ref, *refs, nslot, per_page):
    k_refs, v_refs = refs[:nslot], refs[nslot:2 * nslot]
    kn_ref, vn_ref, g_ref, ocw_ref, o_ref = refs[2 * nslot:]
    b, g = pl.program_id(0), pl.program_id(1)
    qs = qr_ref[0, pl.ds(g * B_GROUP, B_GROUP), :][:, :HEAD_DIM]

    page = k_refs[0].shape[2]
    lane = lax.broadcasted_iota(jnp.int32, (B_GROUP, page), 1)
    masks = [lane // SEL_BLOCK == ix_ref[(b * B_KV_HEADS + g) * nslot + s] % per_page for s in range(nslot)]
    o_sel = _gathered_attention(qs, [r[0] for r in k_refs], [r[0] for r in v_refs], masks,
                                kn_ref[0, 0], vn_ref[0, 0])
    gts = g_ref[0]
    hrow = lax.broadcasted_iota(jnp.int32, (B_GROUP, LANES), 0) + g * B_GROUP
    glane = lax.broadcasted_iota(jnp.int32, (B_GROUP, LANES), 1)
    g1 = jnp.sum(jnp.where(glane == 3 * hrow + 1, gts, 0.0), axis=1, keepdims=True)
    ocw = ocw_ref[0, pl.ds(g * B_GROUP, B_GROUP), :]
    o_ref[0, pl.ds(g * B_GROUP, B_GROUP), :] = g1 * o_sel + ocw[:, :HEAD_DIM] + ocw[:, HEAD_DIM:]


def _nsa_sample_attend(cache_t, page_table, idx, qbr_rows, ks_new, vs_new, gates, ocw, *, nslot):
    db = page_table.shape[0]
    page = cache_t.shape[2]
    per_page = page // SEL_BLOCK

    def kv_spec(s, row_blk0):
        def imap(b, g, pt, ix):
            blk = ix[(b * B_KV_HEADS + g) * nslot + s]
            return (pt[b, blk // per_page], row_blk0 + g, 0)
        return pl.BlockSpec((1, HEAD_DIM, page), imap)

    row = lambda a: pl.BlockSpec((1,) + a.shape[1:], lambda b, g, pt, ix: (b,) + (0,) * (a.ndim - 1))
    new = pl.BlockSpec((1, 1, 1, HEAD_DIM), lambda b, g, pt, ix: (b, g, 0, 0))
    return pl.pallas_call(
        functools.partial(_nsa_sample_attend_kernel, nslot=nslot, per_page=per_page),
        out_shape=jax.ShapeDtypeStruct((db, B_HEADS, HEAD_DIM), F32),
        grid_spec=pltpu.PrefetchScalarGridSpec(
            num_scalar_prefetch=2, grid=(db, B_KV_HEADS),
            in_specs=[row(qbr_rows)] + [kv_spec(s, 2 * B_KV_HEADS) for s in range(nslot)]
                     + [kv_spec(s, 3 * B_KV_HEADS) for s in range(nslot)] + [new, new, row(gates), row(ocw)],
            out_specs=pl.BlockSpec((1, B_HEADS, HEAD_DIM), lambda b, g, pt, ix: (b, 0, 0))),
        compiler_params=_cparams("parallel", "arbitrary"),
        name="nsa_sample_attend",
    )(page_table, idx, qbr_rows, *([cache_t] * (2 * nslot)), ks_new, vs_new, gates, ocw)


def _prep_weights(l, g_mix_pre, w_in, b_gate, cmp_pe, cmp_w1, cmp_b1, cmp_w2, w_out, g_mix_post,
                  g_ffn_pre, w_up, w_conv, b_conv, w_down, g_ffn_post):
    d_in = w_in.shape[-1]
    n_gate = b_gate.shape[-1]
    return dict(
        g_mix_pre=g_mix_pre[l][None],
        w_in=jnp.pad(w_in[l], ((0, 0), (0, D_IN_PAD - d_in))).astype(BF16),
        b_gate=jnp.pad(b_gate[l], (0, LANES - n_gate))[None],
        cmp=_compress_weights(cmp_pe[l], cmp_w1[l], cmp_b1[l], cmp_w2[l]),
        w_out=w_out[l].astype(BF16), g_mix_post=g_mix_post[l][None], g_ffn_pre=g_ffn_pre[l][None],
        w_up=w_up[l].astype(BF16), w_conv=w_conv[l], b_conv=b_conv[l][None],
        w_down=w_down[l].astype(BF16), g_ffn_post=g_ffn_post[l][None],
    )


def _group_gates(gates):
    t = gates.shape[0]
    return gates[:, :3 * B_HEADS].reshape(t, B_KV_HEADS, 3 * B_GROUP).transpose(1, 0, 2)


FFN_CK = 1408
FFN_TM = 512


def _prompt_layer(x2d, w):
    t = x2d.shape[0]
    (kva, nsa, win, gates, qa, kaug, va, qb, qbr, ksaug, vs, kw, vw, kcvc, kmean) = _inproj(
        x2d, jnp.arange(t), w["g_mix_pre"], w["w_in"], w["b_gate"], tm=512, with_kmean=True)
    o_a = _moba_prompt(qa, _gate_matrix(kmean.reshape(t // A_BLOCK, A_KV_HEADS * HEAD_DIM)), kaug, va)
    kvcmp = _compress(kcvc.reshape(4, t // CMP_STRIDE, CMP_STRIDE * HEAD_DIM), w["cmp"])
    o_b = _nsa_prompt(qb, qbr, kvcmp, ksaug, vs, kw, vw, _group_gates(gates))
    x1, h2 = _outproj(o_a, o_b, x2d, w["w_out"], w["g_mix_post"], w["g_ffn_pre"], tm=512)
    y, conv_state = _ffn_seq(h2, x1, w["w_up"], w["w_conv"], w["b_conv"], w["w_down"], w["g_ffn_post"],
                             tm=FFN_TM, ck=FFN_CK)
    keep = min(WINDOW, t)
    return (y, kva.reshape(t, 2, A_KV_HEADS, HEAD_DIM), nsa.reshape(t, 4, B_KV_HEADS, HEAD_DIM),
            win[t - keep:].reshape(keep, 2, B_KV_HEADS, HEAD_DIM), conv_state)


PAGES_PER_STEP = 16


def _sample_layer(x2d, cache_moba, cache_nsa, win_cache, conv_state, page_table, w):
    db = x2d.shape[0]
    n_pool, page = cache_moba.shape[:2]
    npg = page_table.shape[1]
    past = npg * page
    wb = win_cache.shape[1]
    assert wb == WINDOW and past % A_BLOCK == 0 and past // A_BLOCK >= A_TOPK and A_BLOCK % page == 0
    assert page % SEL_BLOCK == 0
    pg = min(PAGES_PER_STEP, npg)
    (kva, nsa, win, gates, qa, _, _, qb, qbr, _, _, _, _, _, _) = _inproj(
        x2d, jnp.full((db,), past, jnp.int32), w["g_mix_pre"], w["w_in"], w["b_gate"], tm=db, with_kmean=False)
    rows = lambda a: a.transpose(1, 0, 2).astype(F32)
    qa_rows, qb_rows, qbr_rows = rows(qa), rows(qb), rows(qbr)

    tpose = lambda c: c.transpose(0, 2, 3, 4, 1).reshape(c.shape[0], -1, c.shape[1])
    moba_t, nsa_t, win_t = tpose(cache_moba), tpose(cache_nsa), tpose(win_cache)

    kmean_t = _moba_kmean(moba_t, page_table, pg=pg)
    idx_a = _moba_gate(qa_rows, kmean_t, nblk=past // A_BLOCK)[:, :, :A_TOPK].reshape(-1)
    rep = A_HEADS // A_KV_HEADS
    per_head = lambda a: jnp.repeat(a.reshape(db, A_KV_HEADS, HEAD_DIM), rep, axis=1)
    o_a = _moba_sample(moba_t, page_table, idx_a, qa_rows, per_head(kva[:, :256]), per_head(kva[:, 256:]))

    kvcmp = _nsa_sample_compress(nsa_t, page_table, w["cmp"], pg=pg)
    gates3 = gates[:, None, :]
    win2d = win_cache.reshape(db, wb, 2 * B_KV_HEADS * HEAD_DIM)
    win_new = win[:, None, :]
    idx_b, ocw = _nsa_sample_select(qb_rows, qbr_rows, kvcmp, win_t, win_new, gates3, past=past)
    nslot = min(SEL_TOPN, past // SEL_BLOCK + 1) - 1
    per_group = lambda a: a.reshape(db, B_KV_HEADS, 1, HEAD_DIM)
    o_b = _nsa_sample_attend(nsa_t, page_table, idx_b[:, :B_KV_HEADS, :nslot].reshape(-1), qbr_rows,
                             per_group(nsa[:, 256:384]), per_group(nsa[:, 384:]), gates3, ocw, nslot=nslot)

    x1, h2 = _outproj(o_a.reshape(db, -1), o_b.reshape(db, -1), x2d, w["w_out"], w["g_mix_post"],
                      w["g_ffn_pre"], tm=db)
    y, conv_new = _ffn_step(h2, x1, conv_state, w["w_up"], w["w_conv"], w["b_conv"], w["w_down"],
                            w["g_ffn_post"], ck=FFN_CK)
    keep = min(WINDOW, wb + 1)
    win_all = jnp.concatenate([win2d, win_new], axis=1)[:, wb + 1 - keep:]
    return (y, kva.reshape(db, 2, A_KV_HEADS, HEAD_DIM), nsa.reshape(db, 4, B_KV_HEADS, HEAD_DIM),
            win_all.reshape(db, keep, 2, B_KV_HEADS, HEAD_DIM), conv_new)


def kernel(x_prompt, x_sample, cache_moba_kv, cache_nsa_kv, cache_nsa_win_kv, state_ffn_conv, page_table,
           g_mix_pre, w_in, b_gate, cmp_pe, cmp_w1, cmp_b1, cmp_w2, w_out, g_mix_post, g_ffn_pre, w_up,
           w_conv, b_conv, w_down, g_ffn_post):
    depth = w_in.shape[0]
    assert depth == 1 and x_prompt.shape[0] == 1
    w = _prep_weights(0, g_mix_pre, w_in, b_gate, cmp_pe, cmp_w1, cmp_b1, cmp_w2, w_out, g_mix_post,
                      g_ffn_pre, w_up, w_conv, b_conv, w_down, g_ffn_post)
    y_p, moba_p, nsa_p, win_p, conv_p = _prompt_layer(x_prompt[0], w)
    assert x_sample.shape[1] == 1
    y_s, moba_s, nsa_s, win_s, conv_s = _sample_layer(
        x_sample[:, 0], cache_moba_kv[0], cache_nsa_kv[0], cache_nsa_win_kv[0], state_ffn_conv[0],
        page_table, w)
    return (y_p[None], y_s[:, None], moba_p[None, None], moba_s[None, :, None],
            nsa_p[None, None], nsa_s[None, :, None], win_p[None, None], win_s[None],
            conv_p[None, None], conv_s[None])
```

```python
import functools
import math

import numpy as np
import jax
import jax.numpy as jnp
from jax import lax
from jax.experimental import pallas as pl
from jax.experimental.pallas import tpu as pltpu

HEAD_DIM = 64
ROPE_DIM = HEAD_DIM // 4
ROPE_THETA = 500000.0
A_HEADS = 8
A_KV_HEADS = 4
A_BLOCK = 256
A_TOPK = 3
B_HEADS = 8
B_KV_HEADS = 2
B_GROUP = B_HEADS // B_KV_HEADS
CMP_LEN = 32
CMP_STRIDE = 16
CMP_HIDDEN = 128
SEL_BLOCK = 64
SEL_TOPN = 16
WINDOW = 512
CONV_W = 3
RMS_EPS = 1e-6
SCALE = HEAD_DIM ** -0.5

LANES = 128
VMEM_LIMIT = 56 * 1024 * 1024
MASK_BIAS = -2.0 ** 127
NEG_INF = float("-inf")

BF16 = jnp.bfloat16
F32 = jnp.float32


def _cparams(*sem):
    return pltpu.CompilerParams(dimension_semantics=sem, vmem_limit_bytes=VMEM_LIMIT)


def _dot(a, b):
    return jnp.dot(a, b, preferred_element_type=F32)


def _dot_t(a, b):
    return lax.dot_general(a, b, (((1,), (1,)), ((), ())), preferred_element_type=F32)


def _rms(x, g):
    y = x * lax.rsqrt(jnp.mean(x * x, axis=-1, keepdims=True) + RMS_EPS)
    return y * g


def _gelu(x):
    c = math.sqrt(2.0 / math.pi)
    return 0.5 * x * (1.0 + jnp.tanh(c * (x + 0.044715 * (x * x * x))))


def _rope128(x, c, sa, sb):
    return x * c + pltpu.roll(x, LANES - ROPE_DIM // 2, 1) * sa + pltpu.roll(x, ROPE_DIM // 2, 1) * sb


C_QA, C_KA, C_VA, C_QB = 0, 512, 768, 1024
C_KC, C_VC, C_KS, C_VS, C_KW, C_VW, C_G = 1536, 1664, 1792, 1920, 2048, 2176, 2304
D_IN_PAD = 2432


def _inproj_kernel(x_ref, g_ref, w_ref, bg_ref, c_ref, sa_ref, sb_ref,
                   kva_ref, nsa_ref, win_ref, gates_ref,
                   qa_ref, kaug_ref, va_ref, qb_ref, qbr_ref, ksaug_ref, vs_ref, kw_ref, vw_ref,
                   kcvc_ref, kmean_ref, *, tm, with_kmean):
    i = pl.program_id(0)
    h = _rms(x_ref[...], g_ref[...]).astype(BF16)
    c, sa, sb = c_ref[...], sa_ref[...], sb_ref[...]

    def proj(c0, width):
        return _dot(h, w_ref[:, c0:c0 + width])

    lane = lax.broadcasted_iota(jnp.int32, (tm, LANES), 1)
    row = lax.broadcasted_iota(jnp.int32, (tm, LANES), 0) + i * tm
    zeros64 = jnp.zeros((tm, HEAD_DIM), BF16)

    for p in range(4):
        q = _rope128(proj(C_QA + p * LANES, LANES), c, sa, sb)
        qa_ref[2 * p] = jnp.concatenate([q[:, :HEAD_DIM].astype(BF16), zeros64], axis=1)
        qa_ref[2 * p + 1] = jnp.concatenate([q[:, HEAD_DIM:].astype(BF16), zeros64], axis=1)

    a_onehot = (lane - HEAD_DIM == row // A_BLOCK).astype(BF16)
    ksum = []
    for p in range(2):
        k = _rope128(proj(C_KA + p * LANES, LANES), c, sa, sb)
        v = proj(C_VA + p * LANES, LANES)
        kva_ref[:, p * LANES:(p + 1) * LANES] = k
        kva_ref[:, 256 + p * LANES:256 + (p + 1) * LANES] = v
        kb = k.astype(BF16)
        vb = v.astype(BF16)
        kaug_ref[2 * p] = jnp.concatenate([kb[:, :HEAD_DIM], a_onehot[:, HEAD_DIM:]], axis=1)
        kaug_ref[2 * p + 1] = jnp.concatenate([kb[:, HEAD_DIM:], a_onehot[:, HEAD_DIM:]], axis=1)
        va_ref[2 * p] = vb[:, :HEAD_DIM]
        va_ref[2 * p + 1] = vb[:, HEAD_DIM:]
        if with_kmean:
            ksum.append(jnp.sum(k.reshape(tm // A_BLOCK, A_BLOCK, LANES), axis=1))
    if with_kmean:
        kmean_ref[0] = jnp.concatenate(ksum, axis=1) * (1.0 / A_BLOCK)
    else:
        kmean_ref[...] = jnp.zeros(kmean_ref.shape, F32)

    for p in range(4):
        q = proj(C_QB + p * LANES, LANES) * SCALE
        qr = _rope128(q, c, sa, sb)
        qb_ref[2 * p] = q[:, :HEAD_DIM].astype(BF16)
        qb_ref[2 * p + 1] = q[:, HEAD_DIM:].astype(BF16)
        qbr_ref[2 * p] = jnp.concatenate([qr[:, :HEAD_DIM].astype(BF16), zeros64], axis=1)
        qbr_ref[2 * p + 1] = jnp.concatenate([qr[:, HEAD_DIM:].astype(BF16), zeros64], axis=1)

    kc = proj(C_KC, LANES)
    vc = proj(C_VC, LANES)
    ks = _rope128(proj(C_KS, LANES), c, sa, sb)
    vs = proj(C_VS, LANES)
    nsa_ref[:, 0:128] = kc
    nsa_ref[:, 128:256] = vc
    nsa_ref[:, 256:384] = ks
    nsa_ref[:, 384:512] = vs
    kcvc_ref[0] = kc[:, :HEAD_DIM]
    kcvc_ref[1] = kc[:, HEAD_DIM:]
    kcvc_ref[2] = vc[:, :HEAD_DIM]
    kcvc_ref[3] = vc[:, HEAD_DIM:]
    s_onehot = (lane - HEAD_DIM == (row // SEL_BLOCK) % HEAD_DIM).astype(BF16)
    ksb = ks.astype(BF16)
    vsb = vs.astype(BF16)
    ksaug_ref[0] = jnp.concatenate([ksb[:, :HEAD_DIM], s_onehot[:, HEAD_DIM:]], axis=1)
    ksaug_ref[1] = jnp.concatenate([ksb[:, HEAD_DIM:], s_onehot[:, HEAD_DIM:]], axis=1)
    vs_ref[0] = vsb[:, :HEAD_DIM]
    vs_ref[1] = vsb[:, HEAD_DIM:]

    kw = _rope128(proj(C_KW, LANES), c, sa, sb)
    vw = proj(C_VW, LANES)
    win_ref[:, 0:128] = kw
    win_ref[:, 128:256] = vw
    kwb = kw.astype(BF16)
    vwb = vw.astype(BF16)
    kw_ref[0] = kwb[:, :HEAD_DIM]
    kw_ref[1] = kwb[:, HEAD_DIM:]
    vw_ref[0] = vwb[:, :HEAD_DIM]
    vw_ref[1] = vwb[:, HEAD_DIM:]

    gates_ref[...] = jax.nn.sigmoid(proj(C_G, LANES) + bg_ref[...])


def _rope_tables(pos):
    half = ROPE_DIM // 2
    inv = ROPE_THETA ** (-2.0 * jnp.arange(half, dtype=F32) / ROPE_DIM)
    ang = pos.astype(F32)[:, None] * inv[None, :]
    cos, sin = jnp.cos(ang), jnp.sin(ang)
    t = pos.shape[0]
    ones = jnp.ones((t, HEAD_DIM - ROPE_DIM), F32)
    zeros = jnp.zeros((t, HEAD_DIM - ROPE_DIM), F32)
    zh = jnp.zeros((t, half), F32)
    c = jnp.concatenate([cos, cos, ones], axis=1)
    sa = jnp.concatenate([-sin, zh, zeros], axis=1)
    sb = jnp.concatenate([zh, sin, zeros], axis=1)
    return tuple(jnp.concatenate([a, a], axis=1) for a in (c, sa, sb))


def _inproj(x2d, pos, g, w_pad, bg_pad, *, tm, with_kmean):
    t, d = x2d.shape
    nt = t // tm
    c, sa, sb = _rope_tables(pos)
    row_spec = lambda w: pl.BlockSpec((tm, w), lambda i: (i, 0))
    head_spec = lambda n, w: pl.BlockSpec((n, tm, w), lambda i: (0, i, 0))
    full = lambda a: pl.BlockSpec(a.shape, lambda i: (0,) * a.ndim)
    nkm = max(tm // A_BLOCK, 1)
    out_shape = (
        jax.ShapeDtypeStruct((t, 512), F32),
        jax.ShapeDtypeStruct((t, 512), F32),
        jax.ShapeDtypeStruct((t, 256), F32),
        jax.ShapeDtypeStruct((t, LANES), F32),
        jax.ShapeDtypeStruct((A_HEADS, t, LANES), BF16),
        jax.ShapeDtypeStruct((A_KV_HEADS, t, LANES), BF16),
        jax.ShapeDtypeStruct((A_KV_HEADS, t, HEAD_DIM), BF16),
        jax.ShapeDtypeStruct((B_HEADS, t, HEAD_DIM), BF16),
        jax.ShapeDtypeStruct((B_HEADS, t, LANES), BF16),
        jax.ShapeDtypeStruct((B_KV_HEADS, t, LANES), BF16),
        jax.ShapeDtypeStruct((B_KV_HEADS, t, HEAD_DIM), BF16),
        jax.ShapeDtypeStruct((B_KV_HEADS, t, HEAD_DIM), BF16),
        jax.ShapeDtypeStruct((B_KV_HEADS, t, HEAD_DIM), BF16),
        jax.ShapeDtypeStruct((4, t, HEAD_DIM), F32),
        jax.ShapeDtypeStruct((nt, nkm, 256), F32),
    )
    out_specs = (
        row_spec(512), row_spec(512), row_spec(256), row_spec(LANES),
        head_spec(A_HEADS, LANES), head_spec(A_KV_HEADS, LANES), head_spec(A_KV_HEADS, HEAD_DIM),
        head_spec(B_HEADS, HEAD_DIM), head_spec(B_HEADS, LANES), head_spec(B_KV_HEADS, LANES),
        head_spec(B_KV_HEADS, HEAD_DIM), head_spec(B_KV_HEADS, HEAD_DIM), head_spec(B_KV_HEADS, HEAD_DIM),
        head_spec(4, HEAD_DIM),
        pl.BlockSpec((1, nkm, 256), lambda i: (i, 0, 0)),
    )
    return pl.pallas_call(
        functools.partial(_inproj_kernel, tm=tm, with_kmean=with_kmean),
        out_shape=out_shape,
        grid=(nt,),
        in_specs=[row_spec(d), full(g), full(w_pad), full(bg_pad),
                  row_spec(LANES), row_spec(LANES), row_spec(LANES)],
        out_specs=out_specs,
        compiler_params=_cparams("parallel"),
        name="inproj",
    )(x2d, g, w_pad, bg_pad, c, sa, sb)


def _outproj_kernel(oa_ref, ob_ref, x_ref, wa_ref, wb_ref, gpost_ref, gpre_ref, x1_ref, h2_ref):
    mix = _dot(oa_ref[...].astype(BF16), wa_ref[...]) + _dot(ob_ref[...].astype(BF16), wb_ref[...])
    x1 = x_ref[...] + _rms(mix, gpost_ref[...])
    x1_ref[...] = x1
    h2_ref[...] = _rms(x1, gpre_ref[...]).astype(BF16)


def _outproj(oa, ob, x2d, w_out_bf, g_post, g_pre, *, tm):
    t, d = x2d.shape
    half = oa.shape[1]
    row = lambda w: pl.BlockSpec((tm, w), lambda i: (i, 0))
    full = lambda a: pl.BlockSpec(a.shape, lambda i: (0,) * a.ndim)
    wa, wb = w_out_bf[:half], w_out_bf[half:]
    return pl.pallas_call(
        _outproj_kernel,
        out_shape=(jax.ShapeDtypeStruct((t, d), F32), jax.ShapeDtypeStruct((t, d), BF16)),
        grid=(t // tm,),
        in_specs=[row(half), row(half), row(d), full(wa), full(wb), full(g_post), full(g_pre)],
        out_specs=(row(d), row(d)),
        compiler_params=_cparams("parallel"),
        name="outproj",
    )(oa, ob, x2d, wa, wb, g_post, g_pre)


HALO = 8


def _ffn_seq_kernel(h_ref, halo_ref, x1_ref, wg_ref, wv_ref, cg_ref, cv_ref, bg_ref, bv_ref,
                    wd_ref, gpost_ref, y_ref, tailg_ref, tailv_ref, acc_ref, ug_ref, uv_ref, *, tm):
    i, c = pl.program_id(0), pl.program_id(1)

    @pl.when(c == 0)
    def _():
        acc_ref[...] = jnp.zeros(acc_ref.shape, F32)

    keep = (i > 0).astype(F32)

    def conv(w_ref, u_ref, cw_ref, cb_ref, tail_ref):
        u_ref[0:HALO] = _dot(halo_ref[...], w_ref[...]) * keep
        u_ref[HALO:HALO + tm] = _dot(h_ref[...], w_ref[...])
        tail_ref[0] = u_ref[tm:tm + HALO]
        cw = cw_ref[...]
        return (u_ref[pl.ds(HALO - 2, tm), :] * cw[0:1] + u_ref[pl.ds(HALO - 1, tm), :] * cw[1:2]
                + u_ref[pl.ds(HALO, tm), :] * cw[2:3] + cb_ref[...])

    gate = conv(wg_ref, ug_ref, cg_ref, bg_ref, tailg_ref)
    val = conv(wv_ref, uv_ref, cv_ref, bv_ref, tailv_ref)
    acc_ref[...] += _dot((_gelu(gate) * val).astype(BF16), wd_ref[...])

    @pl.when(c == pl.num_programs(1) - 1)
    def _():
        y_ref[...] = x1_ref[...] + _rms(acc_ref[...], gpost_ref[...])


def _ffn_step_kernel(h_ref, p0g_ref, p1g_ref, p0v_ref, p1v_ref, x1_ref, wg_ref, wv_ref, cg_ref, cv_ref,
                     bg_ref, bv_ref, wd_ref, gpost_ref, y_ref, upg_ref, upv_ref, acc_ref):
    c = pl.program_id(0)

    @pl.when(c == 0)
    def _():
        acc_ref[...] = jnp.zeros(acc_ref.shape, F32)

    def conv(w_ref, p0_ref, p1_ref, cw_ref, cb_ref, up_ref):
        u = _dot(h_ref[...], w_ref[...])
        up_ref[...] = u
        cw = cw_ref[...]
        return p0_ref[...] * cw[0:1] + p1_ref[...] * cw[1:2] + u * cw[2:3] + cb_ref[...]

    gate = conv(wg_ref, p0g_ref, p1g_ref, cg_ref, bg_ref, upg_ref)
    val = conv(wv_ref, p0v_ref, p1v_ref, cv_ref, bv_ref, upv_ref)
    acc_ref[...] += _dot((_gelu(gate) * val).astype(BF16), wd_ref[...])

    @pl.when(c == pl.num_programs(0) - 1)
    def _():
        y_ref[...] = x1_ref[...] + _rms(acc_ref[...], gpost_ref[...])


def _ffn_seq(h2, x1, w_up_bf, w_conv, b_conv2d, w_down_bf, g_post, *, tm, ck):
    t, d = x1.shape
    dff = w_down_bf.shape[0]
    nff = dff // ck
    nt = t // tm
    hb = tm // HALO
    gcol = lambda r: pl.BlockSpec((r, ck), lambda i, c: (0, c))
    vcol = lambda r: pl.BlockSpec((r, ck), lambda i, c: (0, nff + c))
    row = pl.BlockSpec((tm, d), lambda i, c: (i, 0))
    tail = pl.BlockSpec((1, HALO, ck), lambda i, c: (i, 0, c))
    y, tg, tv = pl.pallas_call(
        functools.partial(_ffn_seq_kernel, tm=tm),
        out_shape=(jax.ShapeDtypeStruct((t, d), F32),
                   jax.ShapeDtypeStruct((nt, HALO, dff), F32), jax.ShapeDtypeStruct((nt, HALO, dff), F32)),
        grid=(nt, nff),
        in_specs=[row, pl.BlockSpec((HALO, d), lambda i, c: (jnp.maximum(i * hb - 1, 0), 0)), row,
                  gcol(d), vcol(d), gcol(CONV_W), vcol(CONV_W), gcol(1), vcol(1),
                  pl.BlockSpec((ck, d), lambda i, c: (c, 0)),
                  pl.BlockSpec(g_post.shape, lambda i, c: (0, 0))],
        out_specs=(row, tail, tail),
        scratch_shapes=[pltpu.VMEM((tm, d), F32), pltpu.VMEM((tm + HALO, ck), F32),
                        pltpu.VMEM((tm + HALO, ck), F32)],
        compiler_params=_cparams("parallel", "arbitrary"),
        name="ffn_seq",
    )(h2, h2, x1, w_up_bf, w_up_bf, w_conv, w_conv, b_conv2d, b_conv2d, w_down_bf, g_post)
    state = jnp.concatenate([tg[-1, HALO - 2:], tv[-1, HALO - 2:]], axis=1)
    return y, state


def _ffn_step(h2, x1, prev, w_up_bf, w_conv, b_conv2d, w_down_bf, g_post, *, ck):
    t, d = x1.shape
    dff = w_down_bf.shape[0]
    nff = dff // ck
    p0, p1 = prev[:, 0], prev[:, 1]
    gcol = lambda r: pl.BlockSpec((r, ck), lambda c: (0, c))
    vcol = lambda r: pl.BlockSpec((r, ck), lambda c: (0, nff + c))
    row = pl.BlockSpec((t, d), lambda c: (0, 0))
    y, ug, uv = pl.pallas_call(
        _ffn_step_kernel,
        out_shape=(jax.ShapeDtypeStruct((t, d), F32),
                   jax.ShapeDtypeStruct((t, dff), F32), jax.ShapeDtypeStruct((t, dff), F32)),
        grid=(nff,),
        in_specs=[row, gcol(t), gcol(t), vcol(t), vcol(t), row,
                  gcol(d), vcol(d), gcol(CONV_W), vcol(CONV_W), gcol(1), vcol(1),
                  pl.BlockSpec((ck, d), lambda c: (c, 0)),
                  pl.BlockSpec(g_post.shape, lambda c: (0, 0))],
        out_specs=(row, pl.BlockSpec((t, ck), lambda c: (0, c)), pl.BlockSpec((t, ck), lambda c: (0, c))),
        scratch_shapes=[pltpu.VMEM((t, d), F32)],
        compiler_params=_cparams("arbitrary"),
        name="ffn_step",
    )(h2, p0, p1, p0, p1, x1, w_up_bf, w_up_bf, w_conv, w_conv, b_conv2d, b_conv2d, w_down_bf, g_post)
    state = jnp.stack([p1, jnp.concatenate([ug, uv], axis=1)], axis=1)
    return y, state


ATT_KT = 512
MOBA_TQ = ATT_KT


def _softmax_init(m_sc, l_sc, acc_sc):
    m_sc[...] = jnp.full(m_sc.shape, NEG_INF, F32)
    l_sc[...] = jnp.zeros(l_sc.shape, F32)
    acc_sc[...] = jnp.zeros(acc_sc.shape, F32)


def _two_pass_attention(q3_ref, k_ref, v_ref, n_full, tiles_per_win, qpos, m_sc, l_sc, acc_sc, p_sc):
    nwin, rows, _ = q3_ref.shape
    kt = p_sc.shape[2]
    groups = [slice(j * LANES, (j + 1) * LANES) for j in range(kt // LANES)]

    def scores(t, ntile=1):
        off = pl.multiple_of(t * kt, kt)
        w = t // tiles_per_win if nwin > 1 else 0
        return _dot_t(q3_ref[w], k_ref[0, pl.ds(off, ntile * kt), :])

    def causal(s, t):
        kpos = t * kt + lax.broadcasted_iota(jnp.int32, (rows, kt), 1)
        return jnp.where(kpos <= qpos, s, NEG_INF)

    def take_max(s):
        mx = s[:, :LANES]
        for j in range(1, s.shape[1] // LANES):
            mx = jnp.maximum(mx, s[:, j * LANES:(j + 1) * LANES])
        m_sc[...] = jnp.maximum(m_sc[...], mx)

    def probs(s):
        m = m_sc[...]
        ps = [jnp.exp(s[:, gs] - m) for gs in groups]
        tot = ps[0]
        for p in ps[1:]:
            tot = tot + p
        l_sc[...] += tot
        return jnp.concatenate(ps, axis=1).astype(BF16)

    def add_pv(p, t):
        off = pl.multiple_of(t * kt, kt)
        acc_sc[...] += _dot(p, v_ref[0, pl.ds(off, kt), :])

    _softmax_init(m_sc, l_sc, acc_sc)

    assert nwin == 1 or tiles_per_win % 2 == 0

    def max_body(u, carry):
        take_max(scores(2 * u, 2))
        return carry

    lax.fori_loop(0, n_full // 2, max_body, 0)
    t_odd = jnp.maximum(n_full - 1, 0)
    take_max(causal(scores(t_odd), t_odd))
    take_max(causal(scores(n_full), n_full))
    m_sc[...] = jnp.broadcast_to(jnp.max(m_sc[...], axis=1, keepdims=True), m_sc.shape)

    p_sc[1] = jnp.zeros(p_sc.shape[1:], BF16)

    def sum_body(t, carry):
        add_pv(p_sc[(t + 1) % 2], jnp.maximum(t - 1, 0))
        p_sc[t % 2] = probs(scores(t))
        return carry

    lax.fori_loop(0, n_full, sum_body, 0)
    add_pv(p_sc[(n_full + 1) % 2], jnp.maximum(n_full - 1, 0))
    add_pv(probs(causal(scores(n_full), n_full)), n_full)
    return acc_sc[...] / jnp.sum(l_sc[...], axis=1, keepdims=True)


def _top_select(v, lane, forced, rounds):
    sel = forced
    lane = lane.astype(F32)
    for _ in range(rounds):
        mx = jnp.max(v, axis=1, keepdims=True)
        idx = jnp.min(jnp.where(v == mx, lane, 1e9), axis=1, keepdims=True)
        pick = (lane == idx) & (mx > NEG_INF)
        sel = sel | pick
        v = jnp.where(pick, NEG_INF, v)
    return sel


def _masked_softmax(s, mask):
    s = jnp.where(mask, s, NEG_INF)
    m = jnp.max(s, axis=1, keepdims=True)
    m = jnp.where(m > NEG_INF, m, 0.0)
    e = jnp.where(mask, jnp.exp(s - m), 0.0)
    d = jnp.sum(e, axis=1, keepdims=True)
    return e / jnp.where(d > 0, d, 1.0)


def _moba_prompt_kernel(q_ref, km_ref, k_ref, v_ref, o_ref, m_sc, l_sc, acc_sc, q3_sc, p_sc):
    i = pl.program_id(1)
    rows = 2 * MOBA_TQ
    q = q_ref[...].reshape(rows, LANES)
    gate = _dot(q, km_ref[0])
    lane = lax.broadcasted_iota(jnp.int32, (rows, LANES), 1)
    qpos = i * MOBA_TQ + lax.broadcasted_iota(jnp.int32, (rows, 1), 0) % MOBA_TQ
    cur = qpos // A_BLOCK
    past = (lane >= HEAD_DIM) & (lane < HEAD_DIM + cur)
    sel = _top_select(jnp.where(past, gate, NEG_INF), lane, lane == HEAD_DIM + cur, A_TOPK)
    bias = jnp.where(sel | (lane < HEAD_DIM), 0.0, MASK_BIAS)
    q3_sc[0] = (q.astype(F32) * SCALE + bias).astype(BF16)

    o = _two_pass_attention(q3_sc, k_ref, v_ref, i, None, qpos, m_sc, l_sc, acc_sc, p_sc)
    o_ref[:, 0:HEAD_DIM] = o[:MOBA_TQ]
    o_ref[:, HEAD_DIM:LANES] = o[MOBA_TQ:]


def _gate_matrix(kmean):
    nblk = kmean.shape[0]
    km = kmean.reshape(nblk, A_KV_HEADS, HEAD_DIM).transpose(1, 2, 0)
    km = jnp.pad(km, ((0, 0), (0, LANES - HEAD_DIM), (HEAD_DIM, LANES - HEAD_DIM - nblk)))
    return km.astype(BF16)


def _moba_prompt(qa, km, kaug, va):
    t = qa.shape[1]
    nt = t // MOBA_TQ
    assert t // A_BLOCK <= HEAD_DIM, "key-block one-hot occupies 64 lanes"
    assert t % MOBA_TQ == 0 and MOBA_TQ % A_BLOCK == 0
    rows = 2 * MOBA_TQ
    return pl.pallas_call(
        _moba_prompt_kernel,
        out_shape=jax.ShapeDtypeStruct((t, A_HEADS * HEAD_DIM), F32),
        grid=(A_KV_HEADS, nt),
        in_specs=[pl.BlockSpec((2, MOBA_TQ, LANES), lambda g, i: (g, i, 0)),
                  pl.BlockSpec((1, LANES, LANES), lambda g, i: (g, 0, 0)),
                  pl.BlockSpec((1, t, LANES), lambda g, i: (g, 0, 0)),
                  pl.BlockSpec((1, t, HEAD_DIM), lambda g, i: (g, 0, 0))],
        out_specs=pl.BlockSpec((MOBA_TQ, LANES), lambda g, i: (i, g)),
        scratch_shapes=[pltpu.VMEM((rows, LANES), F32), pltpu.VMEM((rows, LANES), F32),
                        pltpu.VMEM((rows, HEAD_DIM), F32), pltpu.VMEM((1, rows, LANES), BF16),
                        pltpu.VMEM((2, rows, ATT_KT), BF16)],
        compiler_params=_cparams("parallel", "parallel"),
        name="moba_prompt",
    )(qa, km, kaug, va)


def _compress_kernel(x_ref, pelo_ref, pehi_ref, wlo_ref, whi_ref, b1_ref, w2_ref, o_ref, hi_sc, *, nch):
    x = x_ref[0]
    lo = _dot((x + pelo_ref[0]).astype(BF16), wlo_ref[0])
    hi_sc[0:nch] = _dot((x + pehi_ref[0]).astype(BF16), whi_ref[0])
    hi_sc[nch:nch + 8] = jnp.zeros((8, CMP_HIDDEN), F32)
    hid = _gelu(lo + hi_sc[pl.ds(1, nch), :] + b1_ref[0])
    o_ref[0] = _dot(hid.astype(BF16), w2_ref[0]).astype(o_ref.dtype)


def _compress_weights(cmp_pe, cmp_w1, cmp_b1, cmp_w2):
    flat = CMP_STRIDE * HEAD_DIM
    pelo = cmp_pe[:, :CMP_STRIDE].reshape(2, 1, flat)
    pehi = cmp_pe[:, CMP_STRIDE:].reshape(2, 1, flat)
    wlo = cmp_w1[:, :CMP_STRIDE].reshape(2, flat, CMP_HIDDEN).astype(BF16)
    whi = cmp_w1[:, CMP_STRIDE:].reshape(2, flat, CMP_HIDDEN).astype(BF16)
    return pelo, pehi, wlo, whi, cmp_b1.reshape(2, 1, CMP_HIDDEN), cmp_w2.astype(BF16)


def _compress(xch, cw):
    _, nch, flat = xch.shape
    pelo, pehi, wlo, whi, b1, w2 = cw
    kind = lambda shape: pl.BlockSpec((1,) + shape, lambda j: (j // B_KV_HEADS, 0, 0))
    return pl.pallas_call(
        functools.partial(_compress_kernel, nch=nch),
        out_shape=jax.ShapeDtypeStruct((4, nch, HEAD_DIM), BF16),
        grid=(4,),
        in_specs=[pl.BlockSpec((1, nch, flat), lambda j: (j, 0, 0)),
                  kind((1, flat)), kind((1, flat)), kind((flat, CMP_HIDDEN)), kind((flat, CMP_HIDDEN)),
                  kind((1, CMP_HIDDEN)), kind((CMP_HIDDEN, HEAD_DIM))],
        out_specs=pl.BlockSpec((1, nch, HEAD_DIM), lambda j: (j, 0, 0)),
        scratch_shapes=[pltpu.VMEM((nch + 8, CMP_HIDDEN), F32)],
        compiler_params=_cparams("parallel"),
        name="compress",
    )(xch, pelo, pehi, wlo, whi, b1, w2)


def _importance_matrix(ncmp, nsel):
    ratio = SEL_BLOCK // CMP_STRIDE
    n = np.arange(ncmp)[:, None]
    j = np.arange(nsel)[None, :]
    own = (n // ratio == j)
    last = (n % ratio == ratio - 1)
    m = np.where(own & ~last, 1.0, 0.0) + np.where(last & (own | (n // ratio == j - 1)), 0.5, 0.0)
    return jnp.asarray(m, BF16)


NSA_TQ = 256
WIN_BLOCKS = WINDOW // NSA_TQ + 1


def _nsa_prompt_kernel(*refs, nsel, ncmp):
    (qb_ref, qbr_ref, kc_ref, vc_ref, imp_ref, ks_ref, vs_ref) = refs[:7]
    kw_refs = refs[7:7 + WIN_BLOCKS]
    vw_refs = refs[7 + WIN_BLOCKS:7 + 2 * WIN_BLOCKS]
    g_ref, o_ref, m_sc, l_sc, acc_sc, q3_sc, p_sc = refs[7 + 2 * WIN_BLOCKS:]
    i = pl.program_id(1)
    rows = B_GROUP * NSA_TQ
    q0 = i * NSA_TQ
    q = qb_ref[...].reshape(rows, HEAD_DIM)
    qr = qbr_ref[...].reshape(rows, LANES)

    s = _dot_t(q, kc_ref[0])
    qpos_c = q0 + lax.broadcasted_iota(jnp.int32, (rows, ncmp), 0) % NSA_TQ
    n_c = lax.broadcasted_iota(jnp.int32, (rows, ncmp), 1)
    p = _masked_softmax(s, n_c * CMP_STRIDE + (CMP_LEN - 1) <= qpos_c)
    o_cmp = _dot(p.astype(BF16), vc_ref[0])

    psum = p[0:NSA_TQ]
    for j in range(1, B_GROUP):
        psum = psum + p[j * NSA_TQ:(j + 1) * NSA_TQ]
    p_hi = psum.astype(BF16)
    p_lo = (psum - p_hi.astype(F32)).astype(BF16)
    imp = _dot(p_hi, imp_ref[...]) + _dot(p_lo, imp_ref[...])
    blk = lax.broadcasted_iota(jnp.int32, (NSA_TQ, nsel), 1)
    cur = (q0 + lax.broadcasted_iota(jnp.int32, (NSA_TQ, nsel), 0)) // SEL_BLOCK
    forced = (blk == 0) | (blk == cur) | (blk == cur - 1)
    cand = jnp.where(blk > cur, NEG_INF, jnp.where(forced, jnp.inf, imp))
    sel = _top_select(cand, blk, jnp.zeros((NSA_TQ, nsel), jnp.bool_), min(SEL_TOPN, nsel))
    selbias = jnp.where(sel, 0.0, MASK_BIAS)

    kt_last = (q0 + NSA_TQ - 1) // ATT_KT
    tiles_per_win = HEAD_DIM * SEL_BLOCK // ATT_KT
    qr32 = qr.astype(F32)
    for w in range(-(-nsel // HEAD_DIM)):
        nb = min(HEAD_DIM, nsel - w * HEAD_DIM)
        pieces = [jnp.zeros((NSA_TQ, HEAD_DIM), F32), selbias[:, w * HEAD_DIM:w * HEAD_DIM + nb]]
        if nb < HEAD_DIM:
            pieces.append(jnp.zeros((NSA_TQ, HEAD_DIM - nb), F32))
        bias_w = jnp.concatenate(pieces, axis=1)
        q3_sc[w] = (qr32 + jnp.concatenate([bias_w] * B_GROUP, axis=0)).astype(BF16)

    qpos_col = q0 + lax.broadcasted_iota(jnp.int32, (rows, 1), 0) % NSA_TQ
    o_sel = _two_pass_attention(q3_sc, ks_ref, vs_ref, kt_last, tiles_per_win, qpos_col,
                                m_sc, l_sc, acc_sc, p_sc)

    kband = jnp.concatenate([r[0] for r in kw_refs], axis=0)
    vband = jnp.concatenate([r[0] for r in vw_refs], axis=0)
    nband = WIN_BLOCKS * NSA_TQ
    s = _dot_t(qr[:, :HEAD_DIM], kband)
    qpos = q0 + lax.broadcasted_iota(jnp.int32, (rows, nband), 0) % NSA_TQ
    kpos = q0 - WINDOW + lax.broadcasted_iota(jnp.int32, (rows, nband), 1)
    p = _masked_softmax(s, (kpos <= qpos) & (kpos >= qpos - WINDOW) & (kpos >= 0))
    o_win = _dot(p.astype(BF16), vband)

    gts = g_ref[0]
    for j in range(B_GROUP):
        sl = slice(j * NSA_TQ, (j + 1) * NSA_TQ)
        o = (gts[:, 3 * j:3 * j + 1] * o_cmp[sl] + gts[:, 3 * j + 1:3 * j + 2] * o_sel[sl]
             + gts[:, 3 * j + 2:3 * j + 3] * o_win[sl])
        o_ref[:, j * HEAD_DIM:(j + 1) * HEAD_DIM] = o


def _nsa_prompt(qb, qbr, kvcmp, ksaug, vs, kw, vw, gates):
    t = qb.shape[1]
    nt = t // NSA_TQ
    nsel = t // SEL_BLOCK
    ncmp = kvcmp.shape[1]
    rows = B_GROUP * NSA_TQ
    impm = _importance_matrix(ncmp, nsel)
    res = lambda w: pl.BlockSpec((1, t, w), lambda g, i: (g, 0, 0))
    band = [pl.BlockSpec((1, NSA_TQ, HEAD_DIM),
                         functools.partial(lambda g, i, j: (g, jnp.maximum(i - (WIN_BLOCKS - 1) + j, 0), 0), j=j))
            for j in range(WIN_BLOCKS)]
    return pl.pallas_call(
        functools.partial(_nsa_prompt_kernel, nsel=nsel, ncmp=ncmp),
        out_shape=jax.ShapeDtypeStruct((t, B_HEADS * HEAD_DIM), F32),
        grid=(B_KV_HEADS, nt),
        in_specs=[pl.BlockSpec((B_GROUP, NSA_TQ, HEAD_DIM), lambda g, i: (g, i, 0)),
                  pl.BlockSpec((B_GROUP, NSA_TQ, LANES), lambda g, i: (g, i, 0)),
                  pl.BlockSpec((1, ncmp, HEAD_DIM), lambda g, i: (g, 0, 0)),
                  pl.BlockSpec((1, ncmp, HEAD_DIM), lambda g, i: (B_KV_HEADS + g, 0, 0)),
                  pl.BlockSpec(impm.shape, lambda g, i: (0, 0)),
                  res(LANES), res(HEAD_DIM)] + band + band +
                 [pl.BlockSpec((1, NSA_TQ, 3 * B_GROUP), lambda g, i: (g, i, 0))],
        out_specs=pl.BlockSpec((NSA_TQ, B_GROUP * HEAD_DIM), lambda g, i: (i, g)),
        scratch_shapes=[pltpu.VMEM((rows, LANES), F32), pltpu.VMEM((rows, LANES), F32),
                        pltpu.VMEM((rows, HEAD_DIM), F32),
                        pltpu.VMEM((-(-nsel // HEAD_DIM), rows, LANES), BF16),
                        pltpu.VMEM((2, rows, ATT_KT), BF16)],
        compiler_params=_cparams("parallel", "parallel"),
        name="nsa_prompt",
    )(qb, qbr, kvcmp, kvcmp, impm, ksaug, vs, *([kw] * WIN_BLOCKS), *([vw] * WIN_BLOCKS), gates)


def _top_indices(v, lane, rounds):
    out = jnp.zeros((v.shape[0], LANES), jnp.int32)
    slot = lax.broadcasted_iota(jnp.int32, out.shape, 1)
    for r in range(rounds):
        mx = jnp.max(v, axis=1, keepdims=True)
        idx = jnp.min(jnp.where(v == mx, lane, 1 << 20), axis=1, keepdims=True)
        out = jnp.where(slot == r, idx, out)
        v = jnp.where(lane == idx, NEG_INF, v)
    return out


def _bf16_round(x):
    return x.astype(BF16).astype(F32)


def _moba_kmean_kernel(pt_ref, *refs, pg):
    pages, o_ref = refs[:pg], refs[pg]
    step = pl.program_id(1)
    per_blk = A_BLOCK // pages[0].shape[2]
    nb = pg // per_blk

    @pl.when(step == 0)
    def _():
        o_ref[...] = jnp.zeros(o_ref.shape, F32)

    lane = lax.broadcasted_iota(jnp.int32, o_ref.shape[1:], 1)
    out = o_ref[0]
    for j in range(nb):
        tot = pages[j * per_blk][0]
        for r in pages[j * per_blk + 1:(j + 1) * per_blk]:
            tot = tot + r[0]
        mean = jnp.sum(tot, axis=1, keepdims=True) * (1.0 / A_BLOCK)
        out = jnp.where(lane == step * nb + j, mean, out)
    o_ref[0] = out


def _moba_kmean(cache_t, page_table, *, pg):
    db, npg = page_table.shape
    page = cache_t.shape[2]
    kw = A_KV_HEADS * HEAD_DIM
    assert npg * page // A_BLOCK <= LANES
    specs = [pl.BlockSpec((1, kw, page), functools.partial(lambda b, s, pt, j: (pt[b, s * pg + j], 0, 0), j=j))
             for j in range(pg)]
    return pl.pallas_call(
        functools.partial(_moba_kmean_kernel, pg=pg),
        out_shape=jax.ShapeDtypeStruct((db, kw, LANES), F32),
        grid_spec=pltpu.PrefetchScalarGridSpec(
            num_scalar_prefetch=1, grid=(db, npg // pg), in_specs=specs,
            out_specs=pl.BlockSpec((1, kw, LANES), lambda b, s, pt: (b, 0, 0))),
        compiler_params=_cparams("parallel", "arbitrary"),
        name="moba_kmean",
    )(page_table, *([cache_t] * pg))


def _moba_gate_kernel(q_ref, km_ref, idx_ref, *, nblk):
    q = q_ref[0][:, :HEAD_DIM].astype(BF16)
    head = lax.broadcasted_iota(jnp.int32, (A_HEADS, LANES), 0)
    lane = lax.broadcasted_iota(jnp.int32, (A_HEADS, LANES), 1)
    gate = jnp.zeros((A_HEADS, LANES), F32)
    for g in range(A_KV_HEADS):
        kmg = km_ref[0][g * HEAD_DIM:(g + 1) * HEAD_DIM, :].astype(BF16)
        gate = jnp.where(head // (A_HEADS // A_KV_HEADS) == g, _dot(q, kmg), gate)
    idx_ref[0] = _top_indices(jnp.where(lane < nblk, gate, NEG_INF), lane, A_TOPK)


def _moba_gate(qa_rows, kmean_t, *, nblk):
    db, kw, _ = kmean_t.shape
    return pl.pallas_call(
        functools.partial(_moba_gate_kernel, nblk=nblk),
        out_shape=jax.ShapeDtypeStruct((db, A_HEADS, LANES), jnp.int32),
        grid=(db,),
        in_specs=[pl.BlockSpec((1, A_HEADS, LANES), lambda b: (b, 0, 0)),
                  pl.BlockSpec((1, kw, LANES), lambda b: (b, 0, 0))],
        out_specs=pl.BlockSpec((1, A_HEADS, LANES), lambda b: (b, 0, 0)),
        compiler_params=_cparams("parallel"),
        name="moba_gate",
    )(qa_rows, kmean_t)


def _gathered_attention(q, k_pages, v_pages, masks, k_own, v_own):
    qb = q.astype(BF16)
    ss = [_dot(qb, k.astype(BF16)) for k in k_pages]
    ss = [s if mk is None else jnp.where(mk, s, NEG_INF) for s, mk in zip(ss, masks)]
    s_own = jnp.sum(q * _bf16_round(k_own), axis=1, keepdims=True)
    m = s_own
    for s in ss:
        m = jnp.maximum(m, jnp.max(s, axis=1, keepdims=True))
    e_own = jnp.exp(s_own - m)
    d = e_own
    acc = _bf16_round(e_own) * _bf16_round(v_own)
    for s, v in zip(ss, v_pages):
        e = jnp.exp(s - m)
        d = d + jnp.sum(e, axis=1, keepdims=True)
        acc = acc + _dot_t(e.astype(BF16), v.astype(BF16))
    return acc / d


def _moba_sample_kernel(pt_ref, ix_ref, q_ref, *refs, npick):
    k_refs, v_refs = refs[:npick], refs[npick:2 * npick]
    kn_ref, vn_ref, o_ref, o_sc = refs[2 * npick:]
    h = pl.program_id(1)
    q = q_ref[0][:, :HEAD_DIM] * SCALE
    o_sc[...] = _gathered_attention(q, [r[0] for r in k_refs], [r[0] for r in v_refs], [None] * npick,
                                    kn_ref[0], vn_ref[0])
    o_ref[0, pl.ds(h, 1), :] = o_sc[pl.ds(h, 1), :]


def _moba_sample(cache_t, page_table, idx, qa_rows, k_new, v_new):
    db = page_table.shape[0]
    page = cache_t.shape[2]
    npage = A_BLOCK // page
    rep = A_HEADS // A_KV_HEADS

    def kv_spec(s, j, row_blk0):
        def imap(b, h, pt, ix):
            blk = ix[(b * A_HEADS + h) * A_TOPK + s]
            return (pt[b, blk * npage + j], row_blk0 + h // rep, 0)
        return pl.BlockSpec((1, HEAD_DIM, page), imap)

    picks = [(s, j) for s in range(A_TOPK) for j in range(npage)]
    per_seq = lambda w: pl.BlockSpec((1, A_HEADS, w), lambda b, h, pt, ix: (b, 0, 0))
    return pl.pallas_call(
        functools.partial(_moba_sample_kernel, npick=len(picks)),
        out_shape=jax.ShapeDtypeStruct((db, A_HEADS, HEAD_DIM), F32),
        grid_spec=pltpu.PrefetchScalarGridSpec(
            num_scalar_prefetch=2, grid=(db, A_HEADS),
            in_specs=[per_seq(LANES)]
                     + [kv_spec(s, j, 0) for s, j in picks] + [kv_spec(s, j, A_KV_HEADS) for s, j in picks]
                     + [per_seq(HEAD_DIM), per_seq(HEAD_DIM)],
            out_specs=per_seq(HEAD_DIM),
            scratch_shapes=[pltpu.VMEM((A_HEADS, HEAD_DIM), F32)]),
        compiler_params=_cparams("parallel", "arbitrary"),
        name="moba_sample",
    )(page_table, idx, qa_rows, *([cache_t] * (2 * len(picks))), k_new, v_new)


def _nsa_flatten_kernel(pt_ref, *refs, pg, nch):
    pages = refs[:pg]
    pelo_ref, pehi_ref, wlo_ref, whi_ref, b1_ref, w2_ref, o_ref, x_sc, hi_sc, rows_sc = refs[pg:]
    step = pl.program_id(1)
    page = pages[0].shape[2]
    nc = pg * page // CMP_STRIDE
    base = pl.multiple_of(step * nc, nc)
    lane = lax.broadcasted_iota(jnp.int32, (nc, LANES), 1)
    lo_half = lane < HEAD_DIM
    for j, r in enumerate(pages):
        for pr in range(2):
            rows_sc[pr, j * page:(j + 1) * page, :] = r[0, pr * LANES:(pr + 1) * LANES, :].T
    for u in range(CMP_STRIDE // 2):
        for pr in range(2):
            ap = rows_sc[pr, pl.ds(2 * u, nc, stride=CMP_STRIDE), :]
            bp = rows_sc[pr, pl.ds(2 * u + 1, nc, stride=CMP_STRIDE), :]
            x_sc[2 * pr, pl.ds(base, nc), u * LANES:(u + 1) * LANES] = jnp.where(
                lo_half, ap, pltpu.roll(bp, HEAD_DIM, 1))
            x_sc[2 * pr + 1, pl.ds(base, nc), u * LANES:(u + 1) * LANES] = jnp.where(
                lo_half, pltpu.roll(ap, HEAD_DIM, 1), bp)

    @pl.when(step == pl.num_programs(1) - 1)
    def _():
        hi_sc[nch:nch + 8] = jnp.zeros((8, CMP_HIDDEN), F32)
        for j in range(4):
            c = j // B_KV_HEADS
            x = x_sc[j]
            lo = _dot((x + pelo_ref[c]).astype(BF16), wlo_ref[c])
            hi_sc[0:nch] = _dot((x + pehi_ref[c]).astype(BF16), whi_ref[c])
            hid = _gelu(lo + hi_sc[pl.ds(1, nch), :] + b1_ref[c])
            o_ref[0, j] = _dot(hid.astype(BF16), w2_ref[c]).astype(o_ref.dtype)


def _nsa_sample_compress(cache_t, page_table, cw, *, pg):
    db, npg = page_table.shape
    page = cache_t.shape[2]
    assert page == LANES
    nch = npg * page // CMP_STRIDE
    flat = CMP_STRIDE * HEAD_DIM
    pelo, pehi, wlo, whi, b1, w2 = cw
    full = lambda a: pl.BlockSpec(a.shape, lambda b, s, pt: (0,) * a.ndim)
    specs = [pl.BlockSpec((1, 2 * LANES, page), functools.partial(lambda b, s, pt, j: (pt[b, s * pg + j], 0, 0), j=j))
             for j in range(pg)]
    return pl.pallas_call(
        functools.partial(_nsa_flatten_kernel, pg=pg, nch=nch),
        out_shape=jax.ShapeDtypeStruct((db, 4, nch, HEAD_DIM), BF16),
        grid_spec=pltpu.PrefetchScalarGridSpec(
            num_scalar_prefetch=1, grid=(db, npg // pg),
            in_specs=specs + [full(a) for a in (pelo, pehi, wlo, whi, b1, w2)],
            out_specs=pl.BlockSpec((1, 4, nch, HEAD_DIM), lambda b, s, pt: (b, 0, 0, 0)),
            scratch_shapes=[pltpu.VMEM((4, nch, flat), F32), pltpu.VMEM((nch + 8, CMP_HIDDEN), F32),
                            pltpu.VMEM((2, pg * page, LANES), F32)]),
        compiler_params=_cparams("parallel", "arbitrary"),
        name="nsa_sample_compress",
    )(page_table, *([cache_t] * pg), pelo, pehi, wlo, whi, b1, w2)


def _rows_by_group(per_group):
    head = lax.broadcasted_iota(jnp.int32, per_group[0].shape, 0)
    out = per_group[0]
    for g in range(1, B_KV_HEADS):
        out = jnp.where(head // B_GROUP == g, per_group[g], out)
    return out


def _nsa_sample_select_kernel(q_ref, qr_ref, kv_ref, imp_ref, win_ref, wnew_ref, g_ref, idx_ref, ocw_ref,
                              *, past, nsel_past, ncmp):
    q = q_ref[0][:, :HEAD_DIM].astype(BF16)
    qr = qr_ref[0][:, :HEAD_DIM]
    gts = g_ref[0]
    glane = lax.broadcasted_iota(jnp.int32, (B_HEADS, LANES), 1)
    ghead = lax.broadcasted_iota(jnp.int32, (B_HEADS, LANES), 0)
    gate = lambda c: jnp.sum(jnp.where(glane == 3 * ghead + c, gts, 0.0), axis=1, keepdims=True)

    s = _rows_by_group([_dot_t(q, kv_ref[0, g]) for g in range(B_KV_HEADS)])
    n_c = lax.broadcasted_iota(jnp.int32, (B_HEADS, ncmp), 1)
    p = _masked_softmax(s, n_c * CMP_STRIDE + (CMP_LEN - 1) <= past)
    o_cmp = _rows_by_group([_dot(p.astype(BF16), kv_ref[0, B_KV_HEADS + g]) for g in range(B_KV_HEADS)])

    head = lax.broadcasted_iota(jnp.int32, (B_HEADS, ncmp), 0)
    psum = jnp.zeros((B_HEADS, ncmp), F32)
    for g in range(B_KV_HEADS):
        tot = jnp.sum(jnp.where(head // B_GROUP == g, p, 0.0), axis=0, keepdims=True)
        psum = jnp.where(head == g, tot, psum)
    p_hi = psum.astype(BF16)
    p_lo = (psum - p_hi.astype(F32)).astype(BF16)
    imp = _dot(p_hi, imp_ref[...]) + _dot(p_lo, imp_ref[...])
    blk = lax.broadcasted_iota(jnp.int32, (B_HEADS, nsel_past), 1)
    cand = jnp.where((blk == 0) | (blk == nsel_past - 1), jnp.inf, imp)
    idx_ref[0] = _top_indices(cand, blk, min(SEL_TOPN, nsel_past + 1) - 1)

    wk = [win_ref[0][g * HEAD_DIM:(g + 1) * HEAD_DIM, :] for g in range(B_KV_HEADS)]
    wv = [win_ref[0][(B_KV_HEADS + g) * HEAD_DIM:(B_KV_HEADS + g + 1) * HEAD_DIM, :] for g in range(B_KV_HEADS)]
    nk = [wnew_ref[0][:, g * HEAD_DIM:(g + 1) * HEAD_DIM] for g in range(B_KV_HEADS)]
    nv = [wnew_ref[0][:, (B_KV_HEADS + g) * HEAD_DIM:(B_KV_HEADS + g + 1) * HEAD_DIM] for g in range(B_KV_HEADS)]
    s_w = _rows_by_group([_dot(qr.astype(BF16), k.astype(BF16)) for k in wk])
    s_n = _rows_by_group([jnp.sum(qr * _bf16_round(k), axis=1, keepdims=True) for k in nk])
    m = jnp.maximum(jnp.max(s_w, axis=1, keepdims=True), s_n)
    e_w, e_n = jnp.exp(s_w - m), jnp.exp(s_n - m)
    d = jnp.sum(e_w, axis=1, keepdims=True) + e_n
    o_win = _rows_by_group([_dot_t(e_w.astype(BF16), v.astype(BF16)) + _bf16_round(e_n) * _bf16_round(nvg)
                            for v, nvg in zip(wv, nv)]) / d

    ocw_ref[0] = jnp.concatenate([gate(0) * o_cmp, gate(2) * o_win], axis=1)


def _nsa_sample_select(qb_rows, qbr_rows, kvcmp, win_cache, win_new, gates, *, past):
    db, _, ncmp, _ = kvcmp.shape
    nsel_past = past // SEL_BLOCK
    impm = _importance_matrix(ncmp, nsel_past)
    row3 = lambda a: pl.BlockSpec((1,) + a.shape[1:], lambda b: (b,) + (0,) * (a.ndim - 1))
    kern = functools.partial(_nsa_sample_select_kernel, past=past, nsel_past=nsel_past, ncmp=ncmp)
    return pl.pallas_call(
        kern,
        out_shape=(jax.ShapeDtypeStruct((db, B_HEADS, LANES), jnp.int32),
                   jax.ShapeDtypeStruct((db, B_HEADS, LANES), F32)),
        grid=(db,),
        in_specs=[row3(qb_rows), row3(qbr_rows), row3(kvcmp), pl.BlockSpec(impm.shape, lambda b: (0, 0)),
                  row3(win_cache), row3(win_new), row3(gates)],
        out_specs=(pl.BlockSpec((1, B_HEADS, LANES), lambda b: (b, 0, 0)),
                   pl.BlockSpec((1, B_HEADS, LANES), lambda b: (b, 0, 0))),
        compiler_params=_cparams("parallel"),
        name="nsa_sample_select",
    )(qb_rows, qbr_rows, kvcmp, impm, win_cache, win_new, gates)


def _nsa_sample_attend_kernel(pt_ref, ix_ref, qr_ref, *refs, nslot, per_page):
    k_refs, v_refs = refs[:nslot], refs[nslot:2 * nslot]
    kn_ref, vn_ref, g_ref, ocw_ref, o_ref = refs[2 * nslot:]
    b, g = pl.program_id(0), pl.program_id(1)
    qs = qr_ref[0, pl.ds(g * B_GROUP, B_GROUP), :][:, :HEAD_DIM]

    page = k_refs[0].shape[2]
    lane = lax.broadcasted_iota(jnp.int32, (B_GROUP, page), 1)
    masks = [lane // SEL_BLOCK == ix_ref[(b * B_KV_HEADS + g) * nslot + s] % per_page for s in range(nslot)]
    o_sel = _gathered_attention(qs, [r[0] for r in k_refs], [r[0] for r in v_refs], masks,
                                kn_ref[0, 0], vn_ref[0, 0])
    gts = g_ref[0]
    hrow = lax.broadcasted_iota(jnp.int32, (B_GROUP, LANES), 0) + g * B_GROUP
    glane = lax.broadcasted_iota(jnp.int32, (B_GROUP, LANES), 1)
    g1 = jnp.sum(jnp.where(glane == 3 * hrow + 1, gts, 0.0), axis=1, keepdims=True)
    ocw = ocw_ref[0, pl.ds(g * B_GROUP, B_GROUP), :]
    o_ref[0, pl.ds(g * B_GROUP, B_GROUP), :] = g1 * o_sel + ocw[:, :HEAD_DIM] + ocw[:, HEAD_DIM:]


def _nsa_sample_attend(cache_t, page_table, idx, qbr_rows, ks_new, vs_new, gates, ocw, *, nslot):
    db = page_table.shape[0]
    page = cache_t.shape[2]
    per_page = page // SEL_BLOCK

    def kv_spec(s, row_blk0):
        def imap(b, g, pt, ix):
            blk = ix[(b * B_KV_HEADS + g) * nslot + s]
            return (pt[b, blk // per_page], row_blk0 + g, 0)
        return pl.BlockSpec((1, HEAD_DIM, page), imap)

    row = lambda a: pl.BlockSpec((1,) + a.shape[1:], lambda b, g, pt, ix: (b,) + (0,) * (a.ndim - 1))
    new = pl.BlockSpec((1, 1, 1, HEAD_DIM), lambda b, g, pt, ix: (b, g, 0, 0))
    return pl.pallas_call(
        functools.partial(_nsa_sample_attend_kernel, nslot=nslot, per_page=per_page),
        out_shape=jax.ShapeDtypeStruct((db, B_HEADS, HEAD_DIM), F32),
        grid_spec=pltpu.PrefetchScalarGridSpec(
            num_scalar_prefetch=2, grid=(db, B_KV_HEADS),
            in_specs=[row(qbr_rows)] + [kv_spec(s, 2 * B_KV_HEADS) for s in range(nslot)]
                     + [kv_spec(s, 3 * B_KV_HEADS) for s in range(nslot)] + [new, new, row(gates), row(ocw)],
            out_specs=pl.BlockSpec((1, B_HEADS, HEAD_DIM), lambda b, g, pt, ix: (b, 0, 0))),
        compiler_params=_cparams("parallel", "arbitrary"),
        name="nsa_sample_attend",
    )(page_table, idx, qbr_rows, *([cache_t] * (2 * nslot)), ks_new, vs_new, gates, ocw)


def _prep_weights(l, g_mix_pre, w_in, b_gate, cmp_pe, cmp_w1, cmp_b1, cmp_w2, w_out, g_mix_post,
                  g_ffn_pre, w_up, w_conv, b_conv, w_down, g_ffn_post):
    d_in = w_in.shape[-1]
    n_gate = b_gate.shape[-1]
    return dict(
        g_mix_pre=g_mix_pre[l][None],
        w_in=jnp.pad(w_in[l], ((0, 0), (0, D_IN_PAD - d_in))).astype(BF16),
        b_gate=jnp.pad(b_gate[l], (0, LANES - n_gate))[None],
        cmp=_compress_weights(cmp_pe[l], cmp_w1[l], cmp_b1[l], cmp_w2[l]),
        w_out=w_out[l].astype(BF16), g_mix_post=g_mix_post[l][None], g_ffn_pre=g_ffn_pre[l][None],
        w_up=w_up[l].astype(BF16), w_conv=w_conv[l], b_conv=b_conv[l][None],
        w_down=w_down[l].astype(BF16), g_ffn_post=g_ffn_post[l][None],
    )


def _group_gates(gates):
    t = gates.shape[0]
    return gates[:, :3 * B_HEADS].reshape(t, B_KV_HEADS, 3 * B_GROUP).transpose(1, 0, 2)


FFN_CK = 1408
FFN_TM = 512


def _prompt_layer(x2d, w):
    t = x2d.shape[0]
    (kva, nsa, win, gates, qa, kaug, va, qb, qbr, ksaug, vs, kw, vw, kcvc, kmean) = _inproj(
        x2d, jnp.arange(t), w["g_mix_pre"], w["w_in"], w["b_gate"], tm=512, with_kmean=True)
    o_a = _moba_prompt(qa, _gate_matrix(kmean.reshape(t // A_BLOCK, A_KV_HEADS * HEAD_DIM)), kaug, va)
    kvcmp = _compress(kcvc.reshape(4, t // CMP_STRIDE, CMP_STRIDE * HEAD_DIM), w["cmp"])
    o_b = _nsa_prompt(qb, qbr, kvcmp, ksaug, vs, kw, vw, _group_gates(gates))
    x1, h2 = _outproj(o_a, o_b, x2d, w["w_out"], w["g_mix_post"], w["g_ffn_pre"], tm=512)
    y, conv_state = _ffn_seq(h2, x1, w["w_up"], w["w_conv"], w["b_conv"], w["w_down"], w["g_ffn_post"],
                             tm=FFN_TM, ck=FFN_CK)
    keep = min(WINDOW, t)
    return (y, kva.reshape(t, 2, A_KV_HEADS, HEAD_DIM), nsa.reshape(t, 4, B_KV_HEADS, HEAD_DIM),
            win[t - keep:].reshape(keep, 2, B_KV_HEADS, HEAD_DIM), conv_state)


PAGES_PER_STEP = 16


def _sample_layer(x2d, cache_moba, cache_nsa, win_cache, conv_state, page_table, w):
    db = x2d.shape[0]
    n_pool, page = cache_moba.shape[:2]
    npg = page_table.shape[1]
    past = npg * page
    wb = win_cache.shape[1]
    assert wb == WINDOW and past % A_BLOCK == 0 and past // A_BLOCK >= A_TOPK and A_BLOCK % page == 0
    assert page % SEL_BLOCK == 0
    pg = min(PAGES_PER_STEP, npg)
    (kva, nsa, win, gates, qa, _, _, qb, qbr, _, _, _, _, _, _) = _inproj(
        x2d, jnp.full((db,), past, jnp.int32), w["g_mix_pre"], w["w_in"], w["b_gate"], tm=db, with_kmean=False)
    rows = lambda a: a.transpose(1, 0, 2).astype(F32)
    qa_rows, qb_rows, qbr_rows = rows(qa), rows(qb), rows(qbr)

    tpose = lambda c: c.transpose(0, 2, 3, 4, 1).reshape(c.shape[0], -1, c.shape[1])
    moba_t, nsa_t, win_t = tpose(cache_moba), tpose(cache_nsa), tpose(win_cache)

    kmean_t = _moba_kmean(moba_t, page_table, pg=pg)
    idx_a = _moba_gate(qa_rows, kmean_t, nblk=past // A_BLOCK)[:, :, :A_TOPK].reshape(-1)
    rep = A_HEADS // A_KV_HEADS
    per_head = lambda a: jnp.repeat(a.reshape(db, A_KV_HEADS, HEAD_DIM), rep, axis=1)
    o_a = _moba_sample(moba_t, page_table, idx_a, qa_rows, per_head(kva[:, :256]), per_head(kva[:, 256:]))

    kvcmp = _nsa_sample_compress(nsa_t, page_table, w["cmp"], pg=pg)
    gates3 = gates[:, None, :]
    win2d = win_cache.reshape(db, wb, 2 * B_KV_HEADS * HEAD_DIM)
    win_new = win[:, None, :]
    idx_b, ocw = _nsa_sample_select(qb_rows, qbr_rows, kvcmp, win_t, win_new, gates3, past=past)
    nslot = min(SEL_TOPN, past // SEL_BLOCK + 1) - 1
    per_group = lambda a: a.reshape(db, B_KV_HEADS, 1, HEAD_DIM)
    o_b = _nsa_sample_attend(nsa_t, page_table, idx_b[:, :B_KV_HEADS, :nslot].reshape(-1), qbr_rows,
                             per_group(nsa[:, 256:384]), per_group(nsa[:, 384:]), gates3, ocw, nslot=nslot)

    x1, h2 = _outproj(o_a.reshape(db, -1), o_b.reshape(db, -1), x2d, w["w_out"], w["g_mix_post"],
                      w["g_ffn_pre"], tm=db)
    y, conv_new = _ffn_step(h2, x1, conv_state, w["w_up"], w["w_conv"], w["b_conv"], w["w_down"],
                            w["g_ffn_post"], ck=FFN_CK)
    keep = min(WINDOW, wb + 1)
    win_all = jnp.concatenate([win2d, win_new], axis=1)[:, wb + 1 - keep:]
    return (y, kva.reshape(db, 2, A_KV_HEADS, HEAD_DIM), nsa.reshape(db, 4, B_KV_HEADS, HEAD_DIM),
            win_all.reshape(db, keep, 2, B_KV_HEADS, HEAD_DIM), conv_new)


def kernel(x_prompt, x_sample, cache_moba_kv, cache_nsa_kv, cache_nsa_win_kv, state_ffn_conv, page_table,
           g_mix_pre, w_in, b_gate, cmp_pe, cmp_w1, cmp_b1, cmp_w2, w_out, g_mix_post, g_ffn_pre, w_up,
           w_conv, b_conv, w_down, g_ffn_post):
    depth = w_in.shape[0]
    assert depth == 1 and x_prompt.shape[0] == 1
    w = _prep_weights(0, g_mix_pre, w_in, b_gate, cmp_pe, cmp_w1, cmp_b1, cmp_w2, w_out, g_mix_post,
                      g_ffn_pre, w_up, w_conv, b_conv, w_down, g_ffn_post)
    y_p, moba_p, nsa_p, win_p, conv_p = _prompt_layer(x_prompt[0], w)
    assert x_sample.shape[1] == 1
    y_s, moba_s, nsa_s, win_s, conv_s = _sample_layer(
        x_sample[:, 0], cache_moba_kv[0], cache_nsa_kv[0], cache_nsa_win_kv[0], state_ffn_conv[0],
        page_table, w)
    return (y_p[None], y_s[:, None], moba_p[None, None], moba_s[None, :, None],
            nsa_p[None, None], nsa_s[None, :, None], win_p[None, None], win_s[None],
            conv_p[None, None], conv_s[None])
```
